```python
import math
import jax
import jax.numpy as jnp
from jax import lax
import numpy as np

D_MODEL = 1024
BATCH = 8
SEQ = 4096
DEPTH = 4

CHUNK = 64
Q_BLOCK = 128

ATTN_WIDTH = D_MODEL // 2
RWKV_WIDTH = D_MODEL - ATTN_WIDTH

ATTN_HEADS = 4
ATTN_HEAD_DIM = ATTN_WIDTH // (2 * ATTN_HEADS)
ROT_DIM = ATTN_HEAD_DIM // 4
ROPE_THETA = 500000.0
SUBLN_EPS = 1e-5

RWKV_HEAD_SIZE = 64
RWKV_HEADS = RWKV_WIDTH // RWKV_HEAD_SIZE
W_LORA = max(32, int(round(1.8 * RWKV_WIDTH ** 0.5 / 32)) * 32)
A_LORA = max(32, int(round(1.8 * RWKV_WIDTH ** 0.5 / 32)) * 32)
G_LORA = max(32, int(round(0.6 * RWKV_WIDTH ** 0.8 / 32)) * 32)
GN_EPS = 64e-5
RWKV_IN = 3 * RWKV_WIDTH + W_LORA + A_LORA + G_LORA
IN_W = 3 * ATTN_WIDTH + RWKV_IN

N_EXPERT_GROUPS = 4
EXPERTS_PER_GROUP = 8
N_EXPERTS = N_EXPERT_GROUPS * EXPERTS_PER_GROUP
TOP_K = 2
D_EXPERT = D_MODEL // 4
ROW_BLOCK = 128

NORM_EPS = 1e-6

kernel_name = "hymba_diffattn_rwkv7_hmoe_adaln"


def rms_norm(x, g, eps=NORM_EPS):
    xf = x.astype(jnp.float32)
    y = xf * lax.rsqrt(jnp.mean(xf * xf, axis=-1, keepdims=True) + eps)
    return (y * g.astype(jnp.float32)).astype(x.dtype)


def lambda_init_of(layer):
    return 0.8 - 0.6 * math.exp(-0.3 * layer)


def partial_rotary(t, cos, sin):
    half = ROT_DIM // 2
    t1 = t[..., :half]
    t2 = t[..., half:ROT_DIM]
    return jnp.concatenate([t1 * cos - t2 * sin, t2 * cos + t1 * sin, t[..., ROT_DIM:]], axis=-1)


def diff_attention(q, k, v, lam_vecs, subln_g, lambda_init):
    bsz, seq = q.shape[0], q.shape[1]
    qh = jnp.transpose(q, (0, 2, 3, 1, 4))
    kh = jnp.transpose(k, (0, 2, 3, 1, 4))
    vh = jnp.transpose(v, (0, 2, 1, 3))
    lv = lam_vecs.astype(jnp.float32)
    lam = jnp.exp(jnp.sum(lv[0] * lv[1])) - jnp.exp(jnp.sum(lv[2] * lv[3])) + lambda_init
    chunk_id = jnp.arange(seq) // CHUNK
    scale = ATTN_HEAD_DIM ** -0.5
    outs = []
    for blk in range(seq // Q_BLOCK):
        lo, hi = blk * Q_BLOCK, (blk + 1) * Q_BLOCK
        s = jnp.einsum('bhmqd,bhmkd->bhmqk', qh[:, :, :, lo:hi], kh[:, :, :, :hi]).astype(jnp.float32) * scale
        visible = chunk_id[None, :hi] <= chunk_id[lo:hi, None]
        p = jax.nn.softmax(jnp.where(visible, s, -jnp.inf), axis=-1)
        weights = p[:, :, 0] - lam * p[:, :, 1]
        outs.append(jnp.einsum('bhqk,bhkv->bhqv', weights.astype(v.dtype), vh[:, :, :hi]))
    o = jnp.concatenate(outs, axis=2)
    o = rms_norm(o, subln_g, eps=SUBLN_EPS) * (1.0 - lambda_init)
    return jnp.transpose(o, (0, 2, 1, 3)).reshape(bsz, seq, ATTN_HEADS * 2 * ATTN_HEAD_DIM)


def wkv7_scan(r, w, k, v, a, b):
    bsz, _, heads, n = r.shape

    def step(state, inp):
        r_t, w_t, k_t, v_t, a_t, b_t = inp
        sa = jnp.einsum('bhvk,bhk->bhv', state, a_t)
        state = (state * w_t[:, :, None, :] + sa[..., None] * b_t[:, :, None, :]
                 + v_t[..., None] * k_t[:, :, None, :])
        return state, jnp.einsum('bhvk,bhk->bhv', state, r_t)

    xs = tuple(jnp.moveaxis(t, 1, 0) for t in (r, w, k, v, a, b))
    s0 = jnp.zeros((bsz, heads, n, n), jnp.float32)
    _, ys = lax.scan(step, s0, xs)
    return jnp.moveaxis(ys, 0, 1)


def head_group_norm(y, g, b):
    mu = jnp.mean(y, axis=-1, keepdims=True)
    var = jnp.mean(jnp.square(y - mu), axis=-1, keepdims=True)
    yn = ((y - mu) * lax.rsqrt(var + GN_EPS)).reshape(y.shape[0], y.shape[1], -1)
    return yn * g.astype(jnp.float32) + b.astype(jnp.float32)


def rwkv7_time_mix(p, mu, w0, w_up, a0, a_up, g_up, k_k, k_a, r_k, lnx_g, lnx_b):
    bsz, seq, _ = p.shape
    f32 = jnp.float32
    C = RWKV_WIDTH
    p_prev = jnp.pad(p, ((0, 0), (1, 0), (0, 0)))[:, :seq]
    p = p + (p_prev - p) * mu
    r, k, v = p[..., :C], p[..., C:2 * C], p[..., 2 * C:3 * C]
    o = 3 * C
    w_lo = p[..., o:o + W_LORA]
    a_lo = p[..., o + W_LORA:o + W_LORA + A_LORA]
    g_lo = p[..., o + W_LORA + A_LORA:]
    w = -jax.nn.softplus(-(w0 + jnp.tanh(w_lo) @ w_up).astype(f32)) - 0.5
    decay = jnp.exp(-jnp.exp(w))
    a = jax.nn.sigmoid((a0 + a_lo @ a_up).astype(f32))
    g = (jax.nn.sigmoid(g_lo) @ g_up).astype(f32)
    hs = (bsz, seq, RWKV_HEADS, RWKV_HEAD_SIZE)
    kk = (k * k_k).astype(f32).reshape(hs)
    kk = kk / jnp.maximum(jnp.sqrt(jnp.sum(kk * kk, axis=-1, keepdims=True)), 1e-12)
    kf = k.astype(f32) * (1.0 + (a - 1.0) * k_a.astype(f32))
    rh, kh, vh, ah = rf = r.astype(f32).reshape(hs), kf.reshape(hs), v.astype(f32).reshape(hs), a.reshape(hs)
    y = wkv7_scan(rh, decay.reshape(hs), kh, vh, -kk, kk * ah)
    y = head_group_norm(y, lnx_g, lnx_b)
    bonus = jnp.sum(rh * kh * r_k.astype(f32), axis=-1, keepdims=True) * vh
    out = (y + bonus.reshape(bsz, seq, C)) * g
    return out.astype(p.dtype)


def routed_experts(ht, gate_w, expert_id, w_gate, w_up, w_down):
    T, D = ht.shape
    M = T * TOP_K
    flat_e = expert_id.reshape(M)
    flat_tok = jnp.repeat(jnp.arange(T, dtype=jnp.int32), TOP_K)
    flat_w = gate_w.reshape(M)
    order = jnp.argsort(flat_e)
    se = flat_e[order]
    counts = jnp.bincount(flat_e, length=N_EXPERTS)
    padded = (counts + ROW_BLOCK - 1) // ROW_BLOCK * ROW_BLOCK
    pad_end = jnp.cumsum(padded)
    pad_start = pad_end - padded
    start = jnp.cumsum(counts) - counts
    dest = pad_start[se] + (jnp.arange(M) - start[se])
    R = -(-M // ROW_BLOCK) * ROW_BLOCK + N_EXPERTS * ROW_BLOCK
    n_blocks = R // ROW_BLOCK
    row_tok = jnp.full((R,), T, dtype=jnp.int32).at[dest].set(flat_tok[order])
    row_w = jnp.zeros((R,), ht.dtype).at[dest].set(flat_w[order])
    block_e = jnp.minimum(jnp.searchsorted(pad_end, jnp.arange(n_blocks) * ROW_BLOCK, side='right'),
                          N_EXPERTS - 1)
    h_pad = jnp.concatenate([ht, jnp.zeros((1, D), ht.dtype)], axis=0)

    def block_mlp(args):
        tok, e = args
        xb = h_pad[tok]
        hid = jax.nn.silu(xb @ w_gate[e]) * (xb @ w_up[e])
        return hid @ w_down[e]

    yb = lax.map(block_mlp, (row_tok.reshape(n_blocks, ROW_BLOCK), block_e))
    y = jax.ops.segment_sum(yb.reshape(R, D) * row_w[:, None], row_tok, num_segments=T + 1)
    return y[:T]


def hier_moe(h, w_group, b_group, w_router, b_router, w_gate, w_up, w_down):
    bsz, seq, D = h.shape
    T = bsz * seq
    ht = h.reshape(T, D)
    g_prob = jax.nn.softmax((ht @ w_group + b_group).astype(jnp.float32), axis=-1)
    g_top, g_idx = lax.top_k(g_prob, 1)
    e_logits = (ht @ w_router + b_router).astype(jnp.float32).reshape(T, N_EXPERT_GROUPS, EXPERTS_PER_GROUP)
    in_group = e_logits[jnp.arange(T), g_idx[:, 0]]
    top_v, top_i = lax.top_k(in_group, TOP_K)
    gate_w = jax.nn.softmax(top_v, axis=-1) * g_top
    expert_id = g_idx * EXPERTS_PER_GROUP + top_i
    y = routed_experts(ht, gate_w.astype(h.dtype), expert_id, w_gate, w_up, w_down)
    return y.reshape(bsz, seq, D)


def setup_inputs(seed: int = 0) -> dict:
    key = jax.random.key(seed)
    ks = list(jax.random.split(key, 40))
    L, D = DEPTH, D_MODEL

    def nrm(i, shape, std):
        return std * jax.random.normal(ks[i], shape, jnp.float32)

    def uni(i, shape, lo, hi):
        return jax.random.uniform(ks[i], shape, jnp.float32, lo, hi)

    positions = (jax.random.randint(ks[2], (BATCH, 1), 0, 8192, dtype=jnp.int32)
                 + jnp.arange(SEQ, dtype=jnp.int32)[None, :])
    return {
        "x": nrm(0, (BATCH, SEQ, D), 1.0),
        "c": nrm(1, (BATCH, D), 1.0),
        "positions": positions,
        "ada_w": nrm(3, (L, D, 6 * D), 0.5 * D ** -0.5),
        "ada_b": nrm(4, (L, 6 * D), 0.02),
        "norm1_g": 1.0 + nrm(5, (L, D), 0.05),
        "norm2_g": 1.0 + nrm(6, (L, D), 0.05),
        "w_in": nrm(7, (L, D, IN_W), D ** -0.5),
        "w_out": nrm(8, (L, D, D), D ** -0.5),
        "attn_lambda": nrm(9, (L, 4, ATTN_HEAD_DIM), 0.1),
        "attn_subln_g": 1.0 + nrm(10, (L, 2 * ATTN_HEAD_DIM), 0.05),
        "rwkv_shift_mu": uni(11, (L, RWKV_IN), 0.0, 1.0),
        "rwkv_w0": uni(12, (L, RWKV_WIDTH), -6.0, -1.0),
        "rwkv_w_up": nrm(13, (L, W_LORA, RWKV_WIDTH), 0.5 * W_LORA ** -0.5),
        "rwkv_a0": nrm(14, (L, RWKV_WIDTH), 0.1),
        "rwkv_a_up": nrm(15, (L, A_LORA, RWKV_WIDTH), A_LORA ** -0.5),
        "rwkv_g_up": nrm(16, (L, G_LORA, RWKV_WIDTH), G_LORA ** -0.5),
        "rwkv_k_k": 0.85 + nrm(17, (L, RWKV_WIDTH), 0.05),
        "rwkv_k_a": 1.0 + nrm(18, (L, RWKV_WIDTH), 0.05),
        "rwkv_r_k": nrm(19, (L, RWKV_HEADS, RWKV_HEAD_SIZE), 0.1),
        "rwkv_lnx_g": 1.0 + nrm(20, (L, RWKV_WIDTH), 0.05),
        "rwkv_lnx_b": nrm(21, (L, RWKV_WIDTH), 0.02),
        "moe_w_group": nrm(22, (L, D, N_EXPERT_GROUPS), D ** -0.5),
        "moe_b_group": nrm(23, (L, N_EXPERT_GROUPS), 0.01),
        "moe_w_router": nrm(24, (L, D, N_EXPERTS), D ** -0.5),
        "moe_b_router": nrm(25, (L, N_EXPERTS), 0.01),
        "moe_w_gate": nrm(26, (L, N_EXPERTS, D, D_EXPERT), D ** -0.5),
        "moe_w_up": nrm(27, (L, N_EXPERTS, D, D_EXPERT), D ** -0.5),
        "moe_w_down": nrm(28, (L, N_EXPERTS, D_EXPERT, D), D_EXPERT ** -0.5),
        "final_g": 1.0 + nrm(29, (D,), 0.05),
    }


def reference(x, c, positions, ada_w, ada_b, norm1_g, norm2_g, w_in, w_out, attn_lambda,
              attn_subln_g, rwkv_shift_mu, rwkv_w0, rwkv_w_up, rwkv_a0, rwkv_a_up, rwkv_g_up,
              rwkv_k_k, rwkv_k_a, rwkv_r_k, rwkv_lnx_g, rwkv_lnx_b, moe_w_group, moe_b_group,
              moe_w_router, moe_b_router, moe_w_gate, moe_w_up, moe_w_down, final_g):
    bsz, seq, _ = x.shape
    AW = ATTN_WIDTH
    inv_freq = ROPE_THETA ** (-jnp.arange(0, ROT_DIM, 2, dtype=jnp.float32) / ROT_DIM)
    ang = positions.astype(jnp.float32)[..., None] * inv_freq
    cos = jnp.cos(ang)[:, :, None, None, :].astype(x.dtype)
    sin = jnp.sin(ang)[:, :, None, None, :].astype(x.dtype)
    c_act = jax.nn.silu(c)
    for l in range(DEPTH):
        mod = (c_act @ ada_w[l] + ada_b[l])[:, None, :]
        sh1, sc1, gt1, sh2, sc2, gt2 = jnp.split(mod, 6, axis=-1)
        h = rms_norm(x, norm1_g[l]) * (1.0 + sc1) + sh1
        proj = h @ w_in[l]
        q = proj[..., :AW].reshape(bsz, seq, ATTN_HEADS, 2, ATTN_HEAD_DIM)
        k = proj[..., AW:2 * AW].reshape(bsz, seq, ATTN_HEADS, 2, ATTN_HEAD_DIM)
        v = proj[..., 2 * AW:3 * AW].reshape(bsz, seq, ATTN_HEADS, 2 * ATTN_HEAD_DIM)
        q = partial_rotary(q, cos, sin)
        k = partial_rotary(k, cos, sin)
        attn_out = diff_attention(q, k, v, attn_lambda[l], attn_subln_g[l], lambda_init_of(l))
        rwkv_out = rwkv7_time_mix(proj[..., 3 * AW:], rwkv_shift_mu[l], rwkv_w0[l], rwkv_w_up[l],
                                  rwkv_a0[l], rwkv_a_up[l], rwkv_g_up[l], rwkv_k_k[l], rwkv_k_a[l],
                                  rwkv_r_k[l], rwkv_lnx_g[l], rwkv_lnx_b[l])
        mixed = jnp.concatenate([attn_out, rwkv_out], axis=-1) @ w_out[l]
        x = x + gt1 * mixed
        h2 = rms_norm(x, norm2_g[l]) * (1.0 + sc2) + sh2
        x = x + gt2 * hier_moe(h2, moe_w_group[l], moe_b_group[l], moe_w_router[l], moe_b_router[l],
                               moe_w_gate[l], moe_w_up[l], moe_w_down[l])
    return rms_norm(x, final_g)
```

```python
import functools
import math

import jax
import jax.numpy as jnp
from jax import lax
from jax.experimental import pallas as pl
from jax.experimental.pallas import tpu as pltpu

F32 = jnp.float32
BF16 = jnp.bfloat16

D_MODEL = 1024
ATTN_WIDTH = 512
ATTN_HEADS = 4
HEAD_DIM = 64
ROT_DIM = 16
ROPE_THETA = 500000.0
SUBLN_EPS = 1e-5
ATTN_CHUNK = 64

RWKV_WIDTH = 512
RWKV_HEAD = 64
W_LORA, A_LORA, G_LORA = 32, 32, 96
LORA = W_LORA + A_LORA + G_LORA
LORA_PAD = 256
GN_EPS = 64e-5
RWKV_COLS = 3 * RWKV_WIDTH + LORA_PAD
QKV_COLS = 3 * ATTN_WIDTH
WKV_CHUNK = 64
SLAB = 256

N_GROUPS = 4
EXPERTS_PER_GROUP = 8
N_EXPERTS = 32
D_EXPERT = 256
TOP_K = 2
ROUTE_LANES = 128
MOE_ROWS = 256

NORM_EPS = 1e-6
NEG_BIG = -1e30
VMEM_LIMIT = 56 * 1024 * 1024


def _cparams(sem):
    return pltpu.CompilerParams(dimension_semantics=sem, vmem_limit_bytes=VMEM_LIMIT)


def _mm(a, b):
    return jnp.dot(a.astype(BF16), b.astype(BF16), preferred_element_type=F32)


def _mm_nt(a, b):
    return lax.dot_general(a.astype(BF16), b.astype(BF16), (((1,), (1,)), ((), ())),
                           preferred_element_type=F32)


def _mm_tn(a, b):
    return lax.dot_general(a.astype(BF16), b.astype(BF16), (((0,), (0,)), ((), ())),
                           preferred_element_type=F32)


def _mm_split(m_exact, x):
    hi = x.astype(BF16)
    lo = (x - hi.astype(F32)).astype(BF16)
    return (jnp.dot(m_exact, hi, preferred_element_type=F32)
            + jnp.dot(m_exact, lo, preferred_element_type=F32))


def _split_mm(x, m_exact):
    hi = x.astype(BF16)
    lo = (x - hi.astype(F32)).astype(BF16)
    return (jnp.dot(hi, m_exact, preferred_element_type=F32)
            + jnp.dot(lo, m_exact, preferred_element_type=F32))


def _sigmoid(x):
    return 1.0 / (1.0 + jnp.exp(-x))


def _mod_kernel(c_ref, w_ref, b_ref, o_ref):
    c = c_ref[...]
    ca = c * _sigmoid(c)
    o_ref[0] = jnp.dot(ca, w_ref[0], preferred_element_type=F32,
                       precision=lax.Precision.HIGHEST) + b_ref[0]


def _modulation(c, ada_w, ada_b):
    depth, d, n = ada_w.shape
    bsz = c.shape[0]
    nb = n // d
    return pl.pallas_call(
        _mod_kernel,
        grid=(depth, nb),
        in_specs=[pl.BlockSpec((bsz, d), lambda l, j: (0, 0)),
                  pl.BlockSpec((1, d, d), lambda l, j: (l, 0, j)),
                  pl.BlockSpec((1, 1, d), lambda l, j: (l, 0, j))],
        out_specs=pl.BlockSpec((1, bsz, d), lambda l, j: (l, 0, j)),
        out_shape=jax.ShapeDtypeStruct((depth, bsz, n), F32),
        compiler_params=_cparams(("arbitrary", "arbitrary")),
        name="adaln_mod",
    )(c, ada_w, ada_b.reshape(depth, 1, n))


def _inproj_kernel(x_ref, g_ref, sc_ref, sh_ref, w_ref, mu_ref, cos_ref, s1_ref, s2_ref,
                   qa_ref, qb_ref, k_ref, v_ref, p_ref, carry_ref):
    i = pl.program_id(1)
    tm = x_ref.shape[0]
    x = x_ref[...]
    ms = jnp.mean(x * x, axis=-1, keepdims=True)
    h = x * lax.rsqrt(ms + NORM_EPS) * g_ref[...]
    h = h * (1.0 + sc_ref[0]) + sh_ref[0]
    hb = h.astype(BF16)

    cosv, s1v, s2v = cos_ref[...], s1_ref[...], s2_ref[...]

    def rot128(t):
        return (t * cosv + pltpu.roll(t, ROT_DIM // 2, 1) * s1v
                + pltpu.roll(t, 128 - ROT_DIM // 2, 1) * s2v)

    lane = lax.broadcasted_iota(jnp.int32, (1, 128), 1)
    first_map = lane < HEAD_DIM
    scale = HEAD_DIM ** -0.5
    qkv = jnp.dot(hb, w_ref[:, :QKV_COLS], preferred_element_type=F32)
    for hd in range(ATTN_HEADS):
        lo, hi = hd * 128, (hd + 1) * 128
        q = rot128(qkv[:, lo:hi]) * scale
        qa_ref[:, lo:hi] = jnp.where(first_map, q, 0.0).astype(BF16)
        qb_ref[:, lo:hi] = jnp.where(first_map, 0.0, q).astype(BF16)
        k_ref[:, lo:hi] = rot128(qkv[:, ATTN_WIDTH + lo:ATTN_WIDTH + hi]).astype(BF16)
    v_ref[...] = qkv[:, 2 * ATTN_WIDTH:].astype(BF16)

    @pl.when(i == 0)
    def _():
        carry_ref[...] = jnp.zeros_like(carry_ref)

    p = jnp.dot(hb, w_ref[:, QKV_COLS:], preferred_element_type=F32)
    row = lax.broadcasted_iota(jnp.int32, p.shape, 0)
    prev = jnp.where(row == 0, carry_ref[...], pltpu.roll(p, 1, 0))
    carry_ref[...] = p[tm - 1:tm, :]
    p_ref[...] = p + (prev - p) * mu_ref[...]


def _in_projection(x2, g1, mod3, layer, bsz, seq, w_pad, mu_pad, cos_t, s1_t, s2_t):
    tm = min(512, seq)
    nt = seq // tm
    tok = x2.shape[0]
    row = lambda b, i: (b * nt + i, 0)
    const = lambda b, i: (0, 0)
    wide = w_pad.shape[1]
    outs = pl.pallas_call(
        _inproj_kernel,
        grid=(bsz, nt),
        in_specs=[pl.BlockSpec((tm, D_MODEL), row),
                  pl.BlockSpec((1, D_MODEL), const),
                  pl.BlockSpec((1, 1, D_MODEL), lambda b, i: (layer * bsz + b, 0, 1)),
                  pl.BlockSpec((1, 1, D_MODEL), lambda b, i: (layer * bsz + b, 0, 0)),
                  pl.BlockSpec((D_MODEL, wide), const),
                  pl.BlockSpec((1, RWKV_COLS), const),
                  pl.BlockSpec((tm, 128), row),
                  pl.BlockSpec((tm, 128), row),
                  pl.BlockSpec((tm, 128), row)],
        out_specs=[pl.BlockSpec((tm, ATTN_WIDTH), row)] * 4 + [pl.BlockSpec((tm, RWKV_COLS), row)],
        out_shape=[jax.ShapeDtypeStruct((tok, ATTN_WIDTH), BF16)] * 4
        + [jax.ShapeDtypeStruct((tok, RWKV_COLS), F32)],
        scratch_shapes=[pltpu.VMEM((1, RWKV_COLS), F32)],
        compiler_params=_cparams(("arbitrary", "arbitrary")),
        name="in_proj",
    )(x2, g1, mod3, mod3, w_pad, mu_pad, cos_t, s1_t, s2_t)
    return outs


def _attn_kernel(qa_ref, qb_ref, k_ref, v_ref, lam_ref, g_ref, o_ref, m_ref, l_ref, acc_ref,
                 *, lambda_init):
    i = pl.program_id(1)
    j = pl.program_id(2)
    tq = qa_ref.shape[0]

    @pl.when(j == 0)
    def _():
        m_ref[...] = jnp.full(m_ref.shape, NEG_BIG, F32)
        l_ref[...] = jnp.zeros_like(l_ref)
        acc_ref[...] = jnp.zeros_like(acc_ref)

    def accumulate(diagonal):
        if diagonal:
            rc = lax.broadcasted_iota(jnp.int32, (tq, tq), 0) // ATTN_CHUNK
            cc = lax.broadcasted_iota(jnp.int32, (tq, tq), 1) // ATTN_CHUNK
            visible = cc <= rc
        for hd in range(ATTN_HEADS):
            lo, hi = hd * 128, (hd + 1) * 128
            kh = k_ref[:, lo:hi]
            vh = v_ref[:, lo:hi]
            for mp, q_ref in enumerate((qa_ref, qb_ref)):
                idx = 2 * hd + mp
                s = lax.dot_general(q_ref[:, lo:hi], kh, (((1,), (1,)), ((), ())),
                                    preferred_element_type=F32)
                if diagonal:
                    s = jnp.where(visible, s, NEG_BIG)
                m_old = m_ref[idx]
                m_new = jnp.maximum(m_old, jnp.max(s, axis=-1, keepdims=True))
                alpha = jnp.exp(m_old - m_new)
                p = jnp.exp(s - m_new)
                l_ref[idx] = alpha * l_ref[idx] + jnp.sum(p, axis=-1, keepdims=True)
                acc_ref[idx] = alpha * acc_ref[idx] + jnp.dot(p.astype(BF16), vh,
                                                              preferred_element_type=F32)
                m_ref[idx] = m_new

    @pl.when(j < i)
    def _():
        accumulate(False)

    @pl.when(j == i)
    def _():
        accumulate(True)
        lv = lam_ref[...]
        lam = (jnp.exp(jnp.sum(lv[0:1] * lv[1:2], axis=-1, keepdims=True))
               - jnp.exp(jnp.sum(lv[2:3] * lv[3:4], axis=-1, keepdims=True)) + lambda_init)
        for hd in range(ATTN_HEADS):
            o = acc_ref[2 * hd] / l_ref[2 * hd] - lam * (acc_ref[2 * hd + 1] / l_ref[2 * hd + 1])
            ms = jnp.mean(o * o, axis=-1, keepdims=True)
            o = o * lax.rsqrt(ms + SUBLN_EPS) * g_ref[...] * (1.0 - lambda_init)
            o_ref[:, hd * 128:(hd + 1) * 128] = o.astype(BF16)


def _diff_attention(qa, qb, k, v, lam_vecs, subln_g, lambda_init, bsz, seq):
    tq = min(512, seq)
    nq = seq // tq
    tok = qa.shape[0]
    qmap = lambda b, i, j: (b * nq + i, 0)
    kmap = lambda b, i, j: (b * nq + jnp.minimum(j, i), 0)
    const = lambda b, i, j: (0, 0)
    return pl.pallas_call(
        functools.partial(_attn_kernel, lambda_init=lambda_init),
        grid=(bsz, nq, nq),
        in_specs=[pl.BlockSpec((tq, ATTN_WIDTH), qmap),
                  pl.BlockSpec((tq, ATTN_WIDTH), qmap),
                  pl.BlockSpec((tq, ATTN_WIDTH), kmap),
                  pl.BlockSpec((tq, ATTN_WIDTH), kmap),
                  pl.BlockSpec((4, HEAD_DIM), const),
                  pl.BlockSpec((1, 2 * HEAD_DIM), const)],
        out_specs=pl.BlockSpec((tq, ATTN_WIDTH), qmap),
        out_shape=jax.ShapeDtypeStruct((tok, ATTN_WIDTH), BF16),
        scratch_shapes=[pltpu.VMEM((2 * ATTN_HEADS, tq, 1), F32),
                        pltpu.VMEM((2 * ATTN_HEADS, tq, 1), F32),
                        pltpu.VMEM((2 * ATTN_HEADS, tq, 2 * HEAD_DIM), F32)],
        compiler_params=_cparams(("arbitrary", "arbitrary", "arbitrary")),
        name="diff_attn",
    )(qa, qb, k, v, lam_vecs, subln_g)


def _softplus(z):
    return jnp.maximum(z, 0.0) + jnp.log(1.0 + jnp.exp(-jnp.abs(z)))


def _rwkv_kernel(p_ref, prm_ref, up_ref, tri_ref, seg_ref, bd_ref, sl_ref, il_ref, o_ref, h_ref):
    t = pl.program_id(1)
    C = p_ref.shape[0]
    W = RWKV_WIDTH

    @pl.when(t == 0)
    def _():
        h_ref[...] = jnp.zeros_like(h_ref)

    p = p_ref[...]
    r, k, v, lo = p[:, :W], p[:, W:2 * W], p[:, 2 * W:3 * W], p[:, 3 * W:]
    lane = lax.broadcasted_iota(jnp.int32, lo.shape, 1)
    act = jnp.where(lane < W_LORA, jnp.tanh(lo), jnp.where(lane < W_LORA + A_LORA, lo, _sigmoid(lo)))
    up = _mm(act, up_ref[...])
    prm = prm_ref[...]
    w0, a0, k_k, k_a, r_k, ln_g, ln_b = (prm[n:n + 1] for n in range(7))
    w_raw = -_softplus(-(w0 + up[:, :W])) - 0.5
    logd = -jnp.exp(w_raw)
    a_sig = _sigmoid(a0 + up[:, W:2 * W])
    gate = up[:, 2 * W:]

    seg = seg_ref[...]
    kkr = k * k_k
    kk = kkr / jnp.maximum(jnp.sqrt(_split_mm(kkr * kkr, seg)), 1e-12)
    kf = k * (1.0 + (a_sig - 1.0) * k_a)
    bonus = _split_mm(r * kf * r_k, seg) * v
    a_vec = -kk
    b_vec = kk * a_sig

    cum = _mm_split(tri_ref[...], logd)
    d_in = jnp.exp(cum)
    d_ex = jnp.exp(cum - logd)
    d_inv = jnp.exp(-cum)
    cum_end = cum[C - 1:C, :]
    d_end = jnp.exp(cum_end)
    d_rest = jnp.exp(cum_end - cum)

    at, rt = a_vec * d_ex, r * d_in
    bt, kt = b_vec * d_inv, kf * d_inv
    be, ke = b_vec * d_rest, kf * d_rest

    bd = bd_ref[...]
    sl = sl_ref[...]
    il = il_ref[...]
    eye = il - sl

    ys = []
    for g in range(W // SLAB):
        cs = slice(g * SLAB, (g + 1) * SLAB)

        def stack(z):
            zz = z[:, cs]
            return jnp.concatenate([zz] * (SLAB // RWKV_HEAD), axis=0) * bd

        atm, rtm, btm, ktm = stack(at), stack(rt), stack(bt), stack(kt)
        vm, bem, kem = stack(v), stack(be), stack(ke)
        l_ab = sl * _mm_nt(atm, btm)
        l_ak = sl * _mm_nt(atm, ktm)
        m_rb = il * _mm_nt(rtm, btm)
        m_rk = il * _mm_nt(rtm, ktm)
        inv = eye + l_ab
        lp = l_ab
        for _ in range(int(math.log2(C)) - 1):
            lp = _mm(lp, lp)
            inv = inv + _mm(inv, lp)
        abar = _mm(inv, atm)
        vbar = _mm(inv, _mm(l_ak, vm))
        rhat = rtm + _mm(m_rb, abar)
        yhat = _mm(m_rb, vbar) + _mm(m_rk, vm)
        gmat = eye * d_end[:, cs] + _mm_tn(bem, abar)
        fmat = _mm_tn(bem, vbar) + _mm_tn(kem, vm)
        h0 = h_ref[g]
        ym = _mm(rhat, h0) + yhat
        h_ref[g] = _mm(gmat, h0) + fmat
        y = ym[0:C]
        for hh in range(1, SLAB // RWKV_HEAD):
            y = y + ym[hh * C:(hh + 1) * C]
        ys.append(y)
    y = jnp.concatenate(ys, axis=1)

    inv_n = 1.0 / RWKV_HEAD
    mu = _split_mm(y, seg) * inv_n
    yc = y - mu
    var = _split_mm(yc * yc, seg) * inv_n
    yn = yc * lax.rsqrt(var + GN_EPS) * ln_g + ln_b
    o_ref[...] = ((yn + bonus) * gate).astype(BF16)


def _block_masks():
    n = SLAB
    r = jnp.arange(n)[:, None]
    c = jnp.arange(n)[None, :]
    same = (r // RWKV_HEAD) == (c // RWKV_HEAD)
    bd = same.astype(F32)
    sl = (same & (c < r)).astype(F32)
    il = (same & (c <= r)).astype(F32)
    w = jnp.arange(RWKV_WIDTH)
    seg = ((w[:, None] // RWKV_HEAD) == (w[None, :] // RWKV_HEAD)).astype(BF16)
    tt = jnp.arange(WKV_CHUNK)
    tri = (tt[None, :] <= tt[:, None]).astype(BF16)
    return tri, seg, bd, sl, il


def _rwkv_mix(p, prm, up_w, masks, bsz, seq):
    C = WKV_CHUNK
    nc = seq // C
    tok = p.shape[0]
    tri, seg, bd, sl, il = masks
    row = lambda b, t: (b * nc + t, 0)
    const = lambda b, t: (0, 0)
    return pl.pallas_call(
        _rwkv_kernel,
        grid=(bsz, nc),
        in_specs=[pl.BlockSpec((C, RWKV_COLS), row),
                  pl.BlockSpec((8, RWKV_WIDTH), const),
                  pl.BlockSpec((LORA_PAD, 3 * RWKV_WIDTH), const),
                  pl.BlockSpec((C, C), const),
                  pl.BlockSpec((RWKV_WIDTH, RWKV_WIDTH), const),
                  pl.BlockSpec((SLAB, SLAB), const),
                  pl.BlockSpec((SLAB, SLAB), const),
                  pl.BlockSpec((SLAB, SLAB), const)],
        out_specs=pl.BlockSpec((C, RWKV_WIDTH), row),
        out_shape=jax.ShapeDtypeStruct((tok, RWKV_WIDTH), BF16),
        scratch_shapes=[pltpu.VMEM((RWKV_WIDTH // SLAB, SLAB, SLAB), F32)],
        compiler_params=_cparams(("arbitrary", "arbitrary")),
        name="rwkv7_mix",
    )(p, prm, up_w, tri, seg, bd, sl, il)


def _out_kernel(a_ref, r_ref, x_ref, wa_ref, wr_ref, gt_ref, g2_ref, sc_ref, sh_ref, wrt_ref,
                brt_ref, x1_ref, h2_ref, rt_ref):
    mixed = (jnp.dot(a_ref[...], wa_ref[...], preferred_element_type=F32)
             + jnp.dot(r_ref[...], wr_ref[...], preferred_element_type=F32))
    x1 = x_ref[...] + gt_ref[0] * mixed
    x1_ref[...] = x1
    ms = jnp.mean(x1 * x1, axis=-1, keepdims=True)
    h2 = x1 * lax.rsqrt(ms + NORM_EPS) * g2_ref[...]
    h2 = h2 * (1.0 + sc_ref[0]) + sh_ref[0]
    h2_ref[...] = h2
    logits = jnp.dot(h2, wrt_ref[...], preferred_element_type=F32,
                     precision=lax.Precision.HIGHEST) + brt_ref[...]

    lane_i = lax.broadcasted_iota(jnp.int32, logits.shape, 1)
    lane = lane_i.astype(F32)
    far = float(ROUTE_LANES)
    is_g = lane_i < N_GROUPS
    gmax = jnp.max(jnp.where(is_g, logits, NEG_BIG), axis=-1, keepdims=True)
    gsum = jnp.sum(jnp.where(is_g, jnp.exp(logits - gmax), 0.0), axis=-1, keepdims=True)
    g_top = 1.0 / gsum
    g_idx = jnp.min(jnp.where(is_g & (logits == gmax), lane, far), axis=-1, keepdims=True)
    base = N_GROUPS + g_idx * EXPERTS_PER_GROUP
    in_g = (lane >= base) & (lane < base + EXPERTS_PER_GROUP)
    v1 = jnp.max(jnp.where(in_g, logits, NEG_BIG), axis=-1, keepdims=True)
    i1 = jnp.min(jnp.where(in_g & (logits == v1), lane, far), axis=-1, keepdims=True)
    rest = in_g & (lane != i1)
    v2 = jnp.max(jnp.where(rest, logits, NEG_BIG), axis=-1, keepdims=True)
    i2 = jnp.min(jnp.where(rest & (logits == v2), lane, far), axis=-1, keepdims=True)
    e21 = jnp.exp(v2 - v1)
    w1 = g_top / (1.0 + e21)
    w2 = g_top * e21 / (1.0 + e21)
    rt_ref[...] = jnp.where(lane_i == 0, i1 - N_GROUPS,
                            jnp.where(lane_i == 1, i2 - N_GROUPS,
                                      jnp.where(lane_i == 2, w1,
                                                jnp.where(lane_i == 3, w2, 0.0))))


def _out_projection(attn_o, rwkv_o, x2, w_a, w_r, mod3, g2, w_rt, b_rt, layer, bsz, seq):
    tm = min(512, seq)
    nt = seq // tm
    tok = x2.shape[0]
    row = lambda b, i: (b * nt + i, 0)
    const = lambda b, i: (0, 0)
    modspec = lambda col: pl.BlockSpec((1, 1, D_MODEL), lambda b, i: (layer * bsz + b, 0, col))
    return pl.pallas_call(
        _out_kernel,
        grid=(bsz, nt),
        in_specs=[pl.BlockSpec((tm, ATTN_WIDTH), row),
                  pl.BlockSpec((tm, RWKV_WIDTH), row),
                  pl.BlockSpec((tm, D_MODEL), row),
                  pl.BlockSpec((ATTN_WIDTH, D_MODEL), const),
                  pl.BlockSpec((RWKV_WIDTH, D_MODEL), const),
                  modspec(2),
                  pl.BlockSpec((1, D_MODEL), const),
                  modspec(4),
                  modspec(3),
                  pl.BlockSpec((D_MODEL, ROUTE_LANES), const),
                  pl.BlockSpec((1, ROUTE_LANES), const)],
        out_specs=[pl.BlockSpec((tm, D_MODEL), row),
                   pl.BlockSpec((tm, D_MODEL), row),
                   pl.BlockSpec((tm, ROUTE_LANES), row)],
        out_shape=[jax.ShapeDtypeStruct((tok, D_MODEL), F32),
                   jax.ShapeDtypeStruct((tok, D_MODEL), F32),
                   jax.ShapeDtypeStruct((tok, ROUTE_LANES), F32)],
        compiler_params=_cparams(("arbitrary", "arbitrary")),
        name="out_proj_router",
    )(attn_o, rwkv_o, x2, w_a, w_r, mod3, g2, mod3, mod3, w_rt, b_rt)


def _dispatch_kernel(dest_ref, h_ref, xin_ref, xs_ref, sem):
    del xin_ref
    tt = h_ref.shape[0]

    def row_copy(t, kk):
        d = dest_ref[0, 0, TOP_K * t + kk]
        return pltpu.make_async_copy(h_ref.at[pl.ds(t, 1)], xs_ref.at[pl.ds(d, 1)], sem)

    def issue(t, c):
        for kk in range(TOP_K):
            row_copy(t, kk).start()
        return c

    def drain(t, c):
        for kk in range(TOP_K):
            row_copy(t, kk).wait()
        return c

    lax.fori_loop(0, tt, issue, 0)
    lax.fori_loop(0, tt, drain, 0)


def _dispatch(dest3, h2, xs_init):
    tok = h2.shape[0]
    nt, _, two_tt = dest3.shape
    tt = two_tt // TOP_K
    return pl.pallas_call(
        _dispatch_kernel,
        grid=(nt,),
        in_specs=[pl.BlockSpec((1, 1, two_tt), lambda i: (i, 0, 0), memory_space=pltpu.SMEM),
                  pl.BlockSpec((tt, D_MODEL), lambda i: (i, 0)),
                  pl.BlockSpec(memory_space=pl.ANY)],
        out_specs=pl.BlockSpec(memory_space=pl.ANY),
        out_shape=jax.ShapeDtypeStruct(xs_init.shape, F32),
        scratch_shapes=[pltpu.SemaphoreType.DMA],
        input_output_aliases={2: 0},
        compiler_params=_cparams(("arbitrary",)),
        name="moe_dispatch",
    )(dest3, h2, xs_init)


def _expert_kernel(be_ref, x_ref, wg_ref, wu_ref, wd_ref, y_ref):
    del be_ref
    xb = x_ref[...].astype(BF16)
    gate = jnp.dot(xb, wg_ref[0], preferred_element_type=F32)
    up = jnp.dot(xb, wu_ref[0], preferred_element_type=F32)
    hid = gate * _sigmoid(gate) * up
    y_ref[...] = jnp.dot(hid.astype(BF16), wd_ref[0], preferred_element_type=F32)


def _experts(block_e, xs, w_gate, w_up, w_down):
    rows = xs.shape[0]
    nb = rows // MOE_ROWS
    grid_spec = pltpu.PrefetchScalarGridSpec(
        num_scalar_prefetch=1,
        grid=(nb,),
        in_specs=[pl.BlockSpec((MOE_ROWS, D_MODEL), lambda i, be: (i, 0)),
                  pl.BlockSpec((1, D_MODEL, D_EXPERT), lambda i, be: (be[i], 0, 0)),
                  pl.BlockSpec((1, D_MODEL, D_EXPERT), lambda i, be: (be[i], 0, 0)),
                  pl.BlockSpec((1, D_EXPERT, D_MODEL), lambda i, be: (be[i], 0, 0))],
        out_specs=pl.BlockSpec((MOE_ROWS, D_MODEL), lambda i, be: (i, 0)),
    )
    return pl.pallas_call(
        _expert_kernel,
        grid_spec=grid_spec,
        out_shape=jax.ShapeDtypeStruct((rows, D_MODEL), F32),
        compiler_params=_cparams(("arbitrary",)),
        name="moe_experts",
    )(block_e, xs, w_gate, w_up, w_down)


def _combine_kernel(dest_ref, yb_ref, rt_ref, x_ref, gt_ref, o_ref, buf_ref, sem):
    tt = x_ref.shape[0]

    def row_copy(t, kk):
        d = dest_ref[0, 0, TOP_K * t + kk]
        return pltpu.make_async_copy(yb_ref.at[pl.ds(d, 1)], buf_ref.at[kk, pl.ds(t, 1)], sem)

    def issue(t, c):
        for kk in range(TOP_K):
            row_copy(t, kk).start()
        return c

    def drain(t, c):
        for kk in range(TOP_K):
            row_copy(t, kk).wait()
        return c

    lax.fori_loop(0, tt, issue, 0)
    lax.fori_loop(0, tt, drain, 0)
    rt = rt_ref[...]
    y = buf_ref[0] * rt[:, 2:3] + buf_ref[1] * rt[:, 3:4]
    o_ref[...] = x_ref[...] + gt_ref[0] * y


def _combine(dest3, yb, rt, x1, mod3, layer, bsz, seq):
    tok = x1.shape[0]
    nt, _, two_tt = dest3.shape
    tt = two_tt // TOP_K
    per_seq = seq // tt
    return pl.pallas_call(
        _combine_kernel,
        grid=(nt,),
        in_specs=[pl.BlockSpec((1, 1, two_tt), lambda i: (i, 0, 0), memory_space=pltpu.SMEM),
                  pl.BlockSpec(memory_space=pl.ANY),
                  pl.BlockSpec((tt, ROUTE_LANES), lambda i: (i, 0)),
                  pl.BlockSpec((tt, D_MODEL), lambda i: (i, 0)),
                  pl.BlockSpec((1, 1, D_MODEL), lambda i: (layer * bsz + i // per_seq, 0, 5))],
        out_specs=pl.BlockSpec((tt, D_MODEL), lambda i: (i, 0)),
        out_shape=jax.ShapeDtypeStruct((tok, D_MODEL), F32),
        scratch_shapes=[pltpu.VMEM((TOP_K, tt, D_MODEL), F32), pltpu.SemaphoreType.DMA],
        compiler_params=_cparams(("arbitrary",)),
        name="moe_combine",
    )(dest3, yb, rt, x1, mod3)


def _routing_tables(rt, tok):
    m = tok * TOP_K
    flat_e = rt[:, :TOP_K].astype(jnp.int32).reshape(m)
    onehot = (flat_e[:, None] == jnp.arange(N_EXPERTS, dtype=jnp.int32)[None, :]).astype(jnp.int32)
    running = jnp.cumsum(onehot, axis=0)
    counts = running[-1]
    padded = (counts + MOE_ROWS - 1) // MOE_ROWS * MOE_ROWS
    pad_end = jnp.cumsum(padded)
    pad_start = pad_end - padded
    dest = jnp.sum(onehot * (running - 1 + pad_start[None, :]), axis=1)
    rows = -(-m // MOE_ROWS) * MOE_ROWS + N_EXPERTS * MOE_ROWS
    nb = rows // MOE_ROWS
    block_e = jnp.minimum(
        jnp.searchsorted(pad_end, jnp.arange(nb, dtype=jnp.int32) * MOE_ROWS, side="right"),
        N_EXPERTS - 1).astype(jnp.int32)
    return dest.astype(jnp.int32), block_e, rows


def _final_kernel(x_ref, g_ref, o_ref):
    x = x_ref[...]
    ms = jnp.mean(x * x, axis=-1, keepdims=True)
    o_ref[...] = x * lax.rsqrt(ms + NORM_EPS) * g_ref[...]


def _final_norm(x2, g):
    tok = x2.shape[0]
    tm = min(1024, tok)
    return pl.pallas_call(
        _final_kernel,
        grid=(tok // tm,),
        in_specs=[pl.BlockSpec((tm, D_MODEL), lambda i: (i, 0)),
                  pl.BlockSpec((1, D_MODEL), lambda i: (0, 0))],
        out_specs=pl.BlockSpec((tm, D_MODEL), lambda i: (i, 0)),
        out_shape=jax.ShapeDtypeStruct((tok, D_MODEL), F32),
        compiler_params=_cparams(("arbitrary",)),
        name="final_norm",
    )(x2, g)


def _rotary_tables(positions):
    half = ROT_DIM // 2
    inv_freq = ROPE_THETA ** (-jnp.arange(0, ROT_DIM, 2, dtype=F32) / ROT_DIM)
    ang = positions.astype(F32).reshape(-1, 1) * inv_freq[None, :]
    cos, sin = jnp.cos(ang), jnp.sin(ang)
    tok = ang.shape[0]
    ones = jnp.ones((tok, HEAD_DIM - ROT_DIM), F32)
    zeros = jnp.zeros((tok, HEAD_DIM - ROT_DIM), F32)
    zh = jnp.zeros((tok, half), F32)
    c64 = jnp.concatenate([cos, cos, ones], axis=1)
    s1_64 = jnp.concatenate([zh, sin, zeros], axis=1)
    s2_64 = jnp.concatenate([-sin, zh, zeros], axis=1)
    dup = lambda t: jnp.concatenate([t, t], axis=1)
    return dup(c64), dup(s1_64), dup(s2_64)


def _lambda_init(layer):
    return 0.8 - 0.6 * math.exp(-0.3 * layer)


def kernel(x, c, positions, ada_w, ada_b, norm1_g, norm2_g, w_in, w_out, attn_lambda, attn_subln_g, rwkv_shift_mu, rwkv_w0, rwkv_w_up, rwkv_a0, rwkv_a_up, rwkv_g_up, rwkv_k_k, rwkv_k_a, rwkv_r_k, rwkv_lnx_g, rwkv_lnx_b, moe_w_group, moe_b_group, moe_w_router, moe_b_router, moe_w_gate, moe_w_up, moe_w_down, final_g):
    bsz, seq, d = x.shape
    depth = ada_w.shape[0]
    tok = bsz * seq
    x2 = x.reshape(tok, d)

    mod = _modulation(c, ada_w, ada_b)
    mod3 = mod.reshape(depth * bsz, 1, 6 * d)
    cos_t, s1_t, s2_t = _rotary_tables(positions)
    masks = _block_masks()
    tt = min(512, seq)

    for layer in range(depth):
        pad = LORA_PAD - LORA
        w_pad = jnp.pad(w_in[layer], ((0, 0), (0, pad))).astype(BF16)
        mu_pad = jnp.pad(rwkv_shift_mu[layer], (0, pad)).reshape(1, RWKV_COLS)
        qa, qb, k, v, p = _in_projection(x2, norm1_g[layer].reshape(1, d), mod3, layer, bsz, seq,
                                         w_pad, mu_pad, cos_t, s1_t, s2_t)
        attn_o = _diff_attention(qa, qb, k, v, attn_lambda[layer],
                                 attn_subln_g[layer].reshape(1, 2 * HEAD_DIM),
                                 _lambda_init(layer), bsz, seq)

        zrow = jnp.zeros((RWKV_WIDTH,), F32)
        prm = jnp.stack([rwkv_w0[layer], rwkv_a0[layer], rwkv_k_k[layer], rwkv_k_a[layer],
                         rwkv_r_k[layer].reshape(RWKV_WIDTH), rwkv_lnx_g[layer],
                         rwkv_lnx_b[layer], zrow])
        up_w = jnp.zeros((LORA_PAD, 3 * RWKV_WIDTH), F32)
        up_w = up_w.at[:W_LORA, :RWKV_WIDTH].set(rwkv_w_up[layer])
        up_w = up_w.at[W_LORA:W_LORA + A_LORA, RWKV_WIDTH:2 * RWKV_WIDTH].set(rwkv_a_up[layer])
        up_w = up_w.at[W_LORA + A_LORA:LORA, 2 * RWKV_WIDTH:].set(rwkv_g_up[layer])
        rwkv_o = _rwkv_mix(p, prm, up_w.astype(BF16), masks, bsz, seq)

        w_o = w_out[layer].astype(BF16)
        w_rt = jnp.zeros((d, ROUTE_LANES), F32)
        w_rt = w_rt.at[:, :N_GROUPS].set(moe_w_group[layer])
        w_rt = w_rt.at[:, N_GROUPS:N_GROUPS + N_EXPERTS].set(moe_w_router[layer])
        b_rt = jnp.zeros((1, ROUTE_LANES), F32)
        b_rt = b_rt.at[0, :N_GROUPS].set(moe_b_group[layer])
        b_rt = b_rt.at[0, N_GROUPS:N_GROUPS + N_EXPERTS].set(moe_b_router[layer])
        x1, h2, rt = _out_projection(attn_o, rwkv_o, x2, w_o[:ATTN_WIDTH], w_o[ATTN_WIDTH:], mod3,
                                     norm2_g[layer].reshape(1, d), w_rt, b_rt, layer, bsz, seq)

        dest, block_e, rows = _routing_tables(rt, tok)
        dest3 = dest.reshape(tok // tt, 1, TOP_K * tt)
        xs = _dispatch(dest3, h2, jnp.zeros((rows, d), F32))
        yb = _experts(block_e, xs, moe_w_gate[layer].astype(BF16), moe_w_up[layer].astype(BF16),
                      moe_w_down[layer].astype(BF16))
        x2 = _combine(dest3, yb, rt, x1, mod3, layer, bsz, seq)

    out = _final_norm(x2, final_g.reshape(1, d))
    return out.reshape(bsz, seq, d)
```

```python
import functools
import math

import jax
import jax.numpy as jnp
from jax import lax
from jax.experimental import pallas as pl
from jax.experimental.pallas import tpu as pltpu

F32 = jnp.float32
BF16 = jnp.bfloat16

D_MODEL = 1024
ATTN_WIDTH = 512
ATTN_HEADS = 4
HEAD_DIM = 64
ROT_DIM = 16
ROPE_THETA = 500000.0
SUBLN_EPS = 1e-5
ATTN_CHUNK = 64

RWKV_WIDTH = 512
RWKV_HEAD = 64
W_LORA, A_LORA, G_LORA = 32, 32, 96
LORA = W_LORA + A_LORA + G_LORA
LORA_PAD = 256
GN_EPS = 64e-5
RWKV_COLS = 3 * RWKV_WIDTH + LORA_PAD
QKV_COLS = 3 * ATTN_WIDTH
WKV_CHUNK = 64
WKV_STEP = 128
SLAB = 256

N_GROUPS = 4
EXPERTS_PER_GROUP = 8
N_EXPERTS = 32
D_EXPERT = 256
TOP_K = 2
ROUTE_LANES = 128
MOE_ROWS = 256
DMA_UNROLL = 8
COUNT_TILE = 256

NORM_EPS = 1e-6
NEG_BIG = -1e30
VMEM_LIMIT = 56 * 1024 * 1024


def _cparams(sem):
    return pltpu.CompilerParams(dimension_semantics=sem, vmem_limit_bytes=VMEM_LIMIT)


def _mm(a, b):
    return jnp.dot(a.astype(BF16), b.astype(BF16), preferred_element_type=F32)


def _mm_nt(a, b):
    return lax.dot_general(a.astype(BF16), b.astype(BF16), (((1,), (1,)), ((), ())),
                           preferred_element_type=F32)


def _mm_tn(a, b):
    return lax.dot_general(a.astype(BF16), b.astype(BF16), (((0,), (0,)), ((), ())),
                           preferred_element_type=F32)


def _mm_split(m_exact, x):
    hi = x.astype(BF16)
    lo = (x - hi.astype(F32)).astype(BF16)
    return (jnp.dot(m_exact, hi, preferred_element_type=F32)
            + jnp.dot(m_exact, lo, preferred_element_type=F32))


def _split_mm(x, m_exact):
    hi = x.astype(BF16)
    lo = (x - hi.astype(F32)).astype(BF16)
    return (jnp.dot(hi, m_exact, preferred_element_type=F32)
            + jnp.dot(lo, m_exact, preferred_element_type=F32))


def _sigmoid(x):
    return 1.0 / (1.0 + jnp.exp(-x))


def _mod_kernel(c_ref, w_ref, b_ref, o_ref):
    c = c_ref[...]
    ca = c * _sigmoid(c)
    o_ref[0] = jnp.dot(ca, w_ref[0], preferred_element_type=F32,
                       precision=lax.Precision.HIGHEST) + b_ref[0]


def _modulation(c, ada_w, ada_b):
    depth, d, n = ada_w.shape
    bsz = c.shape[0]
    nb = n // d
    return pl.pallas_call(
        _mod_kernel,
        grid=(depth, nb),
        in_specs=[pl.BlockSpec((bsz, d), lambda l, j: (0, 0)),
                  pl.BlockSpec((1, d, d), lambda l, j: (l, 0, j)),
                  pl.BlockSpec((1, 1, d), lambda l, j: (l, 0, j))],
        out_specs=pl.BlockSpec((1, bsz, d), lambda l, j: (l, 0, j)),
        out_shape=jax.ShapeDtypeStruct((depth, bsz, n), F32),
        compiler_params=_cparams(("arbitrary", "arbitrary")),
        name="adaln_mod",
    )(c, ada_w, ada_b.reshape(depth, 1, n))


def _inproj_kernel(x_ref, g_ref, sc_ref, sh_ref, w_ref, mu_ref, cos_ref, s1_ref, s2_ref,
                   qa_ref, qb_ref, k_ref, vt_ref, p_ref, carry_ref):
    i = pl.program_id(1)
    tm = x_ref.shape[0]
    x = x_ref[...]
    ms = jnp.mean(x * x, axis=-1, keepdims=True)
    h = x * lax.rsqrt(ms + NORM_EPS) * g_ref[...]
    h = h * (1.0 + sc_ref[0]) + sh_ref[0]
    hb = h.astype(BF16)

    cosv, s1v, s2v = cos_ref[...], s1_ref[...], s2_ref[...]

    def rot128(t):
        return (t * cosv + pltpu.roll(t, ROT_DIM // 2, 1) * s1v
                + pltpu.roll(t, 128 - ROT_DIM // 2, 1) * s2v)

    lane = lax.broadcasted_iota(jnp.int32, (1, 128), 1)
    first_map = lane < HEAD_DIM
    scale = HEAD_DIM ** -0.5 * LOG2E
    qkv = jnp.dot(hb, w_ref[:, :QKV_COLS], preferred_element_type=F32)
    for hd in range(ATTN_HEADS):
        lo, hi = hd * 128, (hd + 1) * 128
        q = rot128(qkv[:, lo:hi]) * scale
        qa_ref[:, lo:hi] = jnp.where(first_map, q, 0.0).astype(BF16)
        qb_ref[:, lo:hi] = jnp.where(first_map, 0.0, q).astype(BF16)
        k_ref[:, lo:hi] = rot128(qkv[:, ATTN_WIDTH + lo:ATTN_WIDTH + hi]).astype(BF16)
    vt_ref[...] = qkv[:, 2 * ATTN_WIDTH:].T.astype(BF16)

    @pl.when(i == 0)
    def _():
        carry_ref[...] = jnp.zeros_like(carry_ref)

    p = jnp.dot(hb, w_ref[:, QKV_COLS:], preferred_element_type=F32)
    row = lax.broadcasted_iota(jnp.int32, p.shape, 0)
    prev = jnp.where(row == 0, carry_ref[...], pltpu.roll(p, 1, 0))
    carry_ref[...] = p[tm - 1:tm, :]
    p_ref[...] = p + (prev - p) * mu_ref[...]


def _in_projection(x2, g1, mod3, layer, bsz, seq, w_pad, mu_pad, cos_t, s1_t, s2_t):
    tm = min(512, seq)
    nt = seq // tm
    tok = x2.shape[0]
    row = lambda b, i: (b * nt + i, 0)
    const = lambda b, i: (0, 0)
    wide = w_pad.shape[1]
    outs = pl.pallas_call(
        _inproj_kernel,
        grid=(bsz, nt),
        in_specs=[pl.BlockSpec((tm, D_MODEL), row),
                  pl.BlockSpec((1, D_MODEL), const),
                  pl.BlockSpec((1, 1, D_MODEL), lambda b, i: (layer * bsz + b, 0, 1)),
                  pl.BlockSpec((1, 1, D_MODEL), lambda b, i: (layer * bsz + b, 0, 0)),
                  pl.BlockSpec((D_MODEL, wide), const),
                  pl.BlockSpec((1, RWKV_COLS), const),
                  pl.BlockSpec((tm, 128), row),
                  pl.BlockSpec((tm, 128), row),
                  pl.BlockSpec((tm, 128), row)],
        out_specs=[pl.BlockSpec((tm, ATTN_WIDTH), row)] * 3
        + [pl.BlockSpec((ATTN_WIDTH, tm), lambda b, i: (b, i)), pl.BlockSpec((tm, RWKV_COLS), row)],
        out_shape=[jax.ShapeDtypeStruct((tok, ATTN_WIDTH), BF16)] * 3
        + [jax.ShapeDtypeStruct((bsz * ATTN_WIDTH, seq), BF16),
           jax.ShapeDtypeStruct((tok, RWKV_COLS), F32)],
        scratch_shapes=[pltpu.VMEM((1, RWKV_COLS), F32)],
        compiler_params=_cparams(("arbitrary", "arbitrary")),
        name="in_proj",
    )(x2, g1, mod3, mod3, w_pad, mu_pad, cos_t, s1_t, s2_t)
    return outs


ATTN_TILE = 512
ATTN_QLANES = 256
ATTN_KEYS = 512
LOG2E = 1.4426950408889634


def _attn_kernel(qi_ref, kj_ref, qa_ref, qb_ref, k_ref, vt_ref, lam_ref, g_ref, o_ref,
                 m_ref, l_ref, acc_ref, *, lambda_init):
    step = pl.program_id(1)
    i = qi_ref[step]
    j = kj_ref[step]
    tq = qa_ref.shape[0]
    tk = k_ref.shape[0]
    hq = min(ATTN_QLANES, tq)
    n_half = tq // hq

    @pl.when(j == 0)
    def _():
        m_ref[...] = jnp.full(m_ref.shape, NEG_BIG, F32)
        l_ref[...] = jnp.zeros_like(l_ref)
        acc_ref[...] = jnp.zeros_like(acc_ref)

    kb = min(ATTN_KEYS, tk)

    def accumulate(diagonal):
        def visible(half, k0):
            kc = (lax.broadcasted_iota(jnp.int32, (kb, hq), 0) + k0) // ATTN_CHUNK
            qc = (lax.broadcasted_iota(jnp.int32, (kb, hq), 1) + half * hq) // ATTN_CHUNK
            return kc <= qc

        chains = []
        for half in range(n_half):
            for sub in range(tk // kb):
                k0 = sub * kb
                if diagonal and k0 >= (half + 1) * hq:
                    continue
                masked = diagonal and k0 + kb > half * hq
                for hd in range(ATTN_HEADS):
                    for mp in range(2):
                        chains.append((half, k0, masked, hd, mp))

        def scores(chain):
            half, k0, _, hd, mp = chain
            q_ref = (qa_ref, qb_ref)[mp]
            return lax.dot_general(k_ref[k0:k0 + kb, hd * 128:(hd + 1) * 128],
                                   q_ref[half * hq:(half + 1) * hq, hd * 128:(hd + 1) * 128],
                                   (((1,), (1,)), ((), ())), preferred_element_type=F32)

        s_next = scores(chains[0])
        for n, (half, k0, masked, hd, mp) in enumerate(chains):
            s = s_next
            if n + 1 < len(chains):
                s_next = scores(chains[n + 1])
            idx = (2 * hd + mp) * n_half + half
            if masked:
                s = jnp.where(visible(half, k0), s, NEG_BIG)
            m_old = m_ref[idx]
            m_new = jnp.maximum(m_old, jnp.max(s, axis=0, keepdims=True))
            alpha = jnp.exp2(m_old - m_new)
            p = jnp.exp2(s - m_new)
            l_ref[idx] = alpha * l_ref[idx] + jnp.sum(p, axis=0, keepdims=True)
            acc_ref[idx] = alpha * acc_ref[idx] + jnp.dot(
                vt_ref[hd * 128:(hd + 1) * 128, k0:k0 + kb], p.astype(BF16),
                preferred_element_type=F32)
            m_ref[idx] = m_new

    @pl.when(j < i)
    def _():
        accumulate(False)

    @pl.when(j == i)
    def _():
        accumulate(True)
        lv = lam_ref[...]
        lam = (jnp.exp(jnp.sum(lv[0:1] * lv[1:2], axis=-1, keepdims=True))
               - jnp.exp(jnp.sum(lv[2:3] * lv[3:4], axis=-1, keepdims=True)) + lambda_init)
        gcol = jnp.concatenate([g_ref[...]] * (hq // 128), axis=1)
        for hd in range(ATTN_HEADS):
            for half in range(n_half):
                a = (2 * hd) * n_half + half
                b = (2 * hd + 1) * n_half + half
                o = acc_ref[a] / l_ref[a] - lam * (acc_ref[b] / l_ref[b])
                ms = jnp.mean(o * o, axis=0, keepdims=True)
                o = o * lax.rsqrt(ms + SUBLN_EPS) * gcol * (1.0 - lambda_init)
                o_ref[half * hq:(half + 1) * hq, hd * 128:(hd + 1) * 128] = o.T.astype(BF16)


def _diff_attention(qa, qb, k, vt, lam_vecs, subln_g, lambda_init, bsz, seq):
    tq = min(ATTN_TILE, seq)
    nq = seq // tq
    tok = qa.shape[0]
    hq = min(ATTN_QLANES, tq)
    n_stat = 2 * ATTN_HEADS * (tq // hq)
    pairs = [(i, j) for i in range(nq) for j in range(i + 1)]
    qi = jnp.asarray([p[0] for p in pairs], jnp.int32)
    kj = jnp.asarray([p[1] for p in pairs], jnp.int32)
    qmap = lambda b, s, qi, kj: (b * nq + qi[s], 0)
    kmap = lambda b, s, qi, kj: (b * nq + kj[s], 0)
    vmap = lambda b, s, qi, kj: (b, kj[s])
    const = lambda b, s, qi, kj: (0, 0)
    g_col = jnp.broadcast_to(subln_g.reshape(2 * HEAD_DIM, 1), (2 * HEAD_DIM, 128))
    grid_spec = pltpu.PrefetchScalarGridSpec(
        num_scalar_prefetch=2,
        grid=(bsz, len(pairs)),
        in_specs=[pl.BlockSpec((tq, ATTN_WIDTH), qmap),
                  pl.BlockSpec((tq, ATTN_WIDTH), qmap),
                  pl.BlockSpec((tq, ATTN_WIDTH), kmap),
                  pl.BlockSpec((ATTN_WIDTH, tq), vmap),
                  pl.BlockSpec((4, HEAD_DIM), const),
                  pl.BlockSpec((2 * HEAD_DIM, 128), const)],
        out_specs=pl.BlockSpec((tq, ATTN_WIDTH), qmap),
        scratch_shapes=[pltpu.VMEM((n_stat, 1, hq), F32),
                        pltpu.VMEM((n_stat, 1, hq), F32),
                        pltpu.VMEM((n_stat, 2 * HEAD_DIM, hq), F32)],
    )
    return pl.pallas_call(
        functools.partial(_attn_kernel, lambda_init=lambda_init),
        grid_spec=grid_spec,
        out_shape=jax.ShapeDtypeStruct((tok, ATTN_WIDTH), BF16),
        compiler_params=_cparams(("arbitrary", "arbitrary")),
        name="diff_attn",
    )(qi, kj, qa, qb, k, vt, lam_vecs, g_col)


def _softplus(z):
    return jnp.maximum(z, 0.0) + jnp.log(1.0 + jnp.exp(-jnp.abs(z)))


def _rwkv_kernel(p_ref, prm_ref, up_ref, tri_ref, seg_ref, bd_ref, sl_ref, il_ref, o_ref, h_ref):
    t = pl.program_id(1)
    C = WKV_CHUNK
    n_chunks = p_ref.shape[0] // C
    W = RWKV_WIDTH

    @pl.when(t == 0)
    def _():
        h_ref[...] = jnp.zeros_like(h_ref)

    p = p_ref[...]
    r, k, v, lo = p[:, :W], p[:, W:2 * W], p[:, 2 * W:3 * W], p[:, 3 * W:]
    lane = lax.broadcasted_iota(jnp.int32, lo.shape, 1)
    act = jnp.where(lane < W_LORA, jnp.tanh(lo), jnp.where(lane < W_LORA + A_LORA, lo, _sigmoid(lo)))
    up = _mm(act, up_ref[...])
    prm = prm_ref[...]
    w0, a0, k_k, k_a, r_k, ln_g, ln_b = (prm[n:n + 1] for n in range(7))
    w_raw = -_softplus(-(w0 + up[:, :W])) - 0.5
    logd = -jnp.exp(w_raw)
    a_sig = _sigmoid(a0 + up[:, W:2 * W])
    gate = up[:, 2 * W:]

    seg = seg_ref[...]
    kkr = k * k_k
    kk = kkr / jnp.maximum(jnp.sqrt(_split_mm(kkr * kkr, seg)), 1e-12)
    kf = k * (1.0 + (a_sig - 1.0) * k_a)
    bonus = _split_mm(r * kf * r_k, seg) * v
    a_vec = -kk
    b_vec = kk * a_sig

    cum = _mm_split(tri_ref[...], logd)
    cum_end = jnp.concatenate(
        [jnp.broadcast_to(cum[(ch + 1) * C - 1:(ch + 1) * C, :], (C, W)) for ch in range(n_chunks)],
        axis=0)
    d_in = jnp.exp(cum)
    d_ex = jnp.exp(cum - logd)
    d_inv = jnp.exp(-cum)
    d_end = jnp.exp(cum_end)
    d_rest = jnp.exp(cum_end - cum)

    at, rt = a_vec * d_ex, r * d_in
    bt, kt = b_vec * d_inv, kf * d_inv
    be, ke = b_vec * d_rest, kf * d_rest

    bd = bd_ref[...]
    sl = sl_ref[...]
    il = il_ref[...]
    eye = il - sl
    heads = SLAB // RWKV_HEAD
    n_slabs = W // SLAB
    probs = [(ch, g) for ch in range(n_chunks) for g in range(n_slabs)]

    def stack(z, ch, g):
        zz = z[ch * C:(ch + 1) * C, g * SLAB:(g + 1) * SLAB]
        return jnp.concatenate([zz] * heads, axis=0) * bd

    atm = [stack(at, *pr) for pr in probs]
    btm = [stack(bt, *pr) for pr in probs]
    ktm = [stack(kt, *pr) for pr in probs]
    rtm = [stack(rt, *pr) for pr in probs]
    vm = [stack(v, *pr) for pr in probs]
    l_ab = [sl * _mm_nt(a_, b_) for a_, b_ in zip(atm, btm)]
    l_ak = [sl * _mm_nt(a_, k_) for a_, k_ in zip(atm, ktm)]
    lp = [_mm(l_, l_) for l_ in l_ab]
    inv = [eye + l_ for l_ in l_ab]
    wv = [_mm(l_, v_) for l_, v_ in zip(l_ak, vm)]
    m_rb = [il * _mm_nt(r_, b_) for r_, b_ in zip(rtm, btm)]
    m_rk = [il * _mm_nt(r_, k_) for r_, k_ in zip(rtm, ktm)]
    levels = int(math.log2(C)) - 1
    for lev in range(levels):
        inv_new = [i_ + _mm(i_, l_) for i_, l_ in zip(inv, lp)]
        if lev + 1 < levels:
            lp = [_mm(l_, l_) for l_ in lp]
        inv = inv_new
    abar = [_mm(i_, a_) for i_, a_ in zip(inv, atm)]
    vbar = [_mm(i_, w_) for i_, w_ in zip(inv, wv)]
    bem = [stack(be, *pr) for pr in probs]
    kem = [stack(ke, *pr) for pr in probs]
    rhat = [r_ + _mm(m_, a_) for r_, m_, a_ in zip(rtm, m_rb, abar)]
    yhat = [_mm(mb_, vb_) + _mm(mk_, v_) for mb_, vb_, mk_, v_ in zip(m_rb, vbar, m_rk, vm)]
    gmat = [eye * d_end[ch * C:ch * C + 1, g * SLAB:(g + 1) * SLAB] + _mm_tn(b_, a_)
            for (ch, g), b_, a_ in zip(probs, bem, abar)]
    fmat = [_mm_tn(b_, vb_) + _mm_tn(k_, v_) for b_, vb_, k_, v_ in zip(bem, vbar, kem, vm)]

    hs = [h_ref[g] for g in range(n_slabs)]
    rows = []
    for ch in range(n_chunks):
        ys = []
        for g in range(n_slabs):
            n = ch * n_slabs + g
            ym = _mm(rhat[n], hs[g]) + yhat[n]
            hs[g] = _mm(gmat[n], hs[g]) + fmat[n]
            y = ym[0:C]
            for hh in range(1, heads):
                y = y + ym[hh * C:(hh + 1) * C]
            ys.append(y)
        rows.append(jnp.concatenate(ys, axis=1))
    for g in range(n_slabs):
        h_ref[g] = hs[g]
    y = jnp.concatenate(rows, axis=0) if n_chunks > 1 else rows[0]

    inv_n = 1.0 / RWKV_HEAD
    mu = _split_mm(y, seg) * inv_n
    yc = y - mu
    var = _split_mm(yc * yc, seg) * inv_n
    yn = yc * lax.rsqrt(var + GN_EPS) * ln_g + ln_b
    o_ref[...] = ((yn + bonus) * gate).astype(BF16)


def _block_masks():
    n = SLAB
    r = jnp.arange(n)[:, None]
    c = jnp.arange(n)[None, :]
    same = (r // RWKV_HEAD) == (c // RWKV_HEAD)
    bd = same.astype(F32)
    sl = (same & (c < r)).astype(F32)
    il = (same & (c <= r)).astype(F32)
    w = jnp.arange(RWKV_WIDTH)
    seg = ((w[:, None] // RWKV_HEAD) == (w[None, :] // RWKV_HEAD)).astype(BF16)
    tt = jnp.arange(WKV_STEP)
    tri = ((tt[None, :] <= tt[:, None])
           & (tt[None, :] // WKV_CHUNK == tt[:, None] // WKV_CHUNK)).astype(BF16)
    return tri, seg, bd, sl, il


def _rwkv_mix(p, prm, up_w, masks, bsz, seq):
    C = WKV_STEP
    nc = seq // C
    tok = p.shape[0]
    tri, seg, bd, sl, il = masks
    row = lambda b, t: (b * nc + t, 0)
    const = lambda b, t: (0, 0)
    return pl.pallas_call(
        _rwkv_kernel,
        grid=(bsz, nc),
        in_specs=[pl.BlockSpec((C, RWKV_COLS), row),
                  pl.BlockSpec((8, RWKV_WIDTH), const),
                  pl.BlockSpec((LORA_PAD, 3 * RWKV_WIDTH), const),
                  pl.BlockSpec((C, C), const),
                  pl.BlockSpec((RWKV_WIDTH, RWKV_WIDTH), const),
                  pl.BlockSpec((SLAB, SLAB), const),
                  pl.BlockSpec((SLAB, SLAB), const),
                  pl.BlockSpec((SLAB, SLAB), const)],
        out_specs=pl.BlockSpec((C, RWKV_WIDTH), row),
        out_shape=jax.ShapeDtypeStruct((tok, RWKV_WIDTH), BF16),
        scratch_shapes=[pltpu.VMEM((RWKV_WIDTH // SLAB, SLAB, SLAB), F32)],
        compiler_params=_cparams(("arbitrary", "arbitrary")),
        name="rwkv7_mix",
    )(p, prm, up_w, tri, seg, bd, sl, il)


def _out_kernel(a_ref, r_ref, x_ref, wa_ref, wr_ref, gt_ref, g2_ref, sc_ref, sh_ref, wrt_ref,
                brt_ref, x1_ref, h2_ref, rt_ref):
    mixed = (jnp.dot(a_ref[...], wa_ref[...], preferred_element_type=F32)
             + jnp.dot(r_ref[...], wr_ref[...], preferred_element_type=F32))
    x1 = x_ref[...] + gt_ref[0] * mixed
    x1_ref[...] = x1
    ms = jnp.mean(x1 * x1, axis=-1, keepdims=True)
    h2 = x1 * lax.rsqrt(ms + NORM_EPS) * g2_ref[...]
    h2 = h2 * (1.0 + sc_ref[0]) + sh_ref[0]
    h2_ref[...] = h2
    logits = jnp.dot(h2, wrt_ref[...], preferred_element_type=F32,
                     precision=lax.Precision.HIGHEST) + brt_ref[...]

    lane_i = lax.broadcasted_iota(jnp.int32, logits.shape, 1)
    lane = lane_i.astype(F32)
    far = float(ROUTE_LANES)
    is_g = lane_i < N_GROUPS
    gmax = jnp.max(jnp.where(is_g, logits, NEG_BIG), axis=-1, keepdims=True)
    gsum = jnp.sum(jnp.where(is_g, jnp.exp(logits - gmax), 0.0), axis=-1, keepdims=True)
    g_top = 1.0 / gsum
    g_idx = jnp.min(jnp.where(is_g & (logits == gmax), lane, far), axis=-1, keepdims=True)
    base = N_GROUPS + g_idx * EXPERTS_PER_GROUP
    in_g = (lane >= base) & (lane < base + EXPERTS_PER_GROUP)
    v1 = jnp.max(jnp.where(in_g, logits, NEG_BIG), axis=-1, keepdims=True)
    i1 = jnp.min(jnp.where(in_g & (logits == v1), lane, far), axis=-1, keepdims=True)
    rest = in_g & (lane != i1)
    v2 = jnp.max(jnp.where(rest, logits, NEG_BIG), axis=-1, keepdims=True)
    i2 = jnp.min(jnp.where(rest & (logits == v2), lane, far), axis=-1, keepdims=True)
    e21 = jnp.exp(v2 - v1)
    w1 = g_top / (1.0 + e21)
    w2 = g_top * e21 / (1.0 + e21)
    rt_ref[...] = jnp.where(lane_i == 0, i1 - N_GROUPS,
                            jnp.where(lane_i == 1, i2 - N_GROUPS,
                                      jnp.where(lane_i == 2, w1,
                                                jnp.where(lane_i == 3, w2, 0.0))))


def _out_projection(attn_o, rwkv_o, x2, w_a, w_r, mod3, g2, w_rt, b_rt, layer, bsz, seq):
    tm = min(512, seq)
    nt = seq // tm
    tok = x2.shape[0]
    row = lambda b, i: (b * nt + i, 0)
    const = lambda b, i: (0, 0)
    modspec = lambda col: pl.BlockSpec((1, 1, D_MODEL), lambda b, i: (layer * bsz + b, 0, col))
    return pl.pallas_call(
        _out_kernel,
        grid=(bsz, nt),
        in_specs=[pl.BlockSpec((tm, ATTN_WIDTH), row),
                  pl.BlockSpec((tm, RWKV_WIDTH), row),
                  pl.BlockSpec((tm, D_MODEL), row),
                  pl.BlockSpec((ATTN_WIDTH, D_MODEL), const),
                  pl.BlockSpec((RWKV_WIDTH, D_MODEL), const),
                  modspec(2),
                  pl.BlockSpec((1, D_MODEL), const),
                  modspec(4),
                  modspec(3),
                  pl.BlockSpec((D_MODEL, ROUTE_LANES), const),
                  pl.BlockSpec((1, ROUTE_LANES), const)],
        out_specs=[pl.BlockSpec((tm, D_MODEL), row),
                   pl.BlockSpec((tm, D_MODEL), row),
                   pl.BlockSpec((tm, ROUTE_LANES), row)],
        out_shape=[jax.ShapeDtypeStruct((tok, D_MODEL), F32),
                   jax.ShapeDtypeStruct((tok, D_MODEL), F32),
                   jax.ShapeDtypeStruct((tok, ROUTE_LANES), F32)],
        compiler_params=_cparams(("arbitrary", "arbitrary")),
        name="out_proj_router",
    )(attn_o, rwkv_o, x2, w_a, w_r, mod3, g2, mod3, mod3, w_rt, b_rt)


def _dispatch_kernel(dest_ref, h_ref, xin_ref, xs_ref, sem):
    del xin_ref
    tt = h_ref.shape[0]

    def row_copy(t, kk):
        d = dest_ref[0, 0, TOP_K * t + kk]
        return pltpu.make_async_copy(h_ref.at[pl.ds(t, 1)], xs_ref.at[pl.ds(d, 1)], sem)

    def issue(t, c):
        for kk in range(TOP_K):
            row_copy(t, kk).start()
        return c

    lax.fori_loop(0, tt, issue, 0, unroll=DMA_UNROLL)
    for kk in range(TOP_K):
        pltpu.make_async_copy(h_ref, xs_ref.at[pl.ds(0, tt)], sem).wait()


def _dispatch(dest3, h2, xs_init):
    tok = h2.shape[0]
    nt, _, two_tt = dest3.shape
    tt = two_tt // TOP_K
    return pl.pallas_call(
        _dispatch_kernel,
        grid=(nt,),
        in_specs=[pl.BlockSpec((1, 1, two_tt), lambda i: (i, 0, 0), memory_space=pltpu.SMEM),
                  pl.BlockSpec((tt, D_MODEL), lambda i: (i, 0)),
                  pl.BlockSpec(memory_space=pl.ANY)],
        out_specs=pl.BlockSpec(memory_space=pl.ANY),
        out_shape=jax.ShapeDtypeStruct(xs_init.shape, F32),
        scratch_shapes=[pltpu.SemaphoreType.DMA],
        input_output_aliases={2: 0},
        compiler_params=_cparams(("arbitrary",)),
        name="moe_dispatch",
    )(dest3, h2, xs_init)


def _expert_kernel(be_ref, x_ref, wg_ref, wu_ref, wd_ref, y_ref):
    del be_ref
    xb = x_ref[...].astype(BF16)
    gate = jnp.dot(xb, wg_ref[0], preferred_element_type=F32)
    up = jnp.dot(xb, wu_ref[0], preferred_element_type=F32)
    hid = gate * _sigmoid(gate) * up
    y_ref[...] = jnp.dot(hid.astype(BF16), wd_ref[0], preferred_element_type=F32)


def _experts(block_e, xs, w_gate, w_up, w_down):
    rows = xs.shape[0]
    nb = rows // MOE_ROWS
    grid_spec = pltpu.PrefetchScalarGridSpec(
        num_scalar_prefetch=1,
        grid=(nb,),
        in_specs=[pl.BlockSpec((MOE_ROWS, D_MODEL), lambda i, be: (i, 0)),
                  pl.BlockSpec((1, D_MODEL, D_EXPERT), lambda i, be: (be[i], 0, 0)),
                  pl.BlockSpec((1, D_MODEL, D_EXPERT), lambda i, be: (be[i], 0, 0)),
                  pl.BlockSpec((1, D_EXPERT, D_MODEL), lambda i, be: (be[i], 0, 0))],
        out_specs=pl.BlockSpec((MOE_ROWS, D_MODEL), lambda i, be: (i, 0)),
    )
    return pl.pallas_call(
        _expert_kernel,
        grid_spec=grid_spec,
        out_shape=jax.ShapeDtypeStruct((rows, D_MODEL), F32),
        compiler_params=_cparams(("arbitrary",)),
        name="moe_experts",
    )(block_e, xs, w_gate, w_up, w_down)


def _combine_kernel(dest_ref, yb_ref, rt_ref, x_ref, gt_ref, o_ref, buf_ref, sem):
    tt = x_ref.shape[0]

    def row_copy(t, kk):
        d = dest_ref[0, 0, TOP_K * t + kk]
        return pltpu.make_async_copy(yb_ref.at[pl.ds(d, 1)], buf_ref.at[kk, pl.ds(t, 1)], sem)

    def issue(t, c):
        for kk in range(TOP_K):
            row_copy(t, kk).start()
        return c

    lax.fori_loop(0, tt, issue, 0, unroll=DMA_UNROLL)
    for kk in range(TOP_K):
        pltpu.make_async_copy(yb_ref.at[pl.ds(0, tt)], buf_ref.at[kk], sem).wait()
    rt = rt_ref[...]
    y = buf_ref[0] * rt[:, 2:3] + buf_ref[1] * rt[:, 3:4]
    o_ref[...] = x_ref[...] + gt_ref[0] * y


def _combine(dest3, yb, rt, x1, mod3, layer, bsz, seq):
    tok = x1.shape[0]
    nt, _, two_tt = dest3.shape
    tt = two_tt // TOP_K
    per_seq = seq // tt
    return pl.pallas_call(
        _combine_kernel,
        grid=(nt,),
        in_specs=[pl.BlockSpec((1, 1, two_tt), lambda i: (i, 0, 0), memory_space=pltpu.SMEM),
                  pl.BlockSpec(memory_space=pl.ANY),
                  pl.BlockSpec((tt, ROUTE_LANES), lambda i: (i, 0)),
                  pl.BlockSpec((tt, D_MODEL), lambda i: (i, 0)),
                  pl.BlockSpec((1, 1, D_MODEL), lambda i: (layer * bsz + i // per_seq, 0, 5))],
        out_specs=pl.BlockSpec((tt, D_MODEL), lambda i: (i, 0)),
        out_shape=jax.ShapeDtypeStruct((tok, D_MODEL), F32),
        scratch_shapes=[pltpu.VMEM((TOP_K, tt, D_MODEL), F32), pltpu.SemaphoreType.DMA],
        compiler_params=_cparams(("arbitrary",)),
        name="moe_combine",
    )(dest3, yb, rt, x1, mod3)


def _routing_tables(rt, tok):
    m = tok * TOP_K
    flat_e = rt[:, :TOP_K].astype(jnp.int32).reshape(m)
    onehot = (flat_e[:, None] == jnp.arange(N_EXPERTS, dtype=jnp.int32)[None, :]).astype(jnp.int32)
    ct = min(COUNT_TILE, m)
    tri = (jnp.arange(ct)[None, :] <= jnp.arange(ct)[:, None]).astype(BF16)
    within = jnp.einsum("ts,nse->nte", tri, onehot.reshape(m // ct, ct, N_EXPERTS).astype(BF16),
                        preferred_element_type=F32)
    tile_tot = within[:, -1, :]
    tile_base = jnp.cumsum(tile_tot, axis=0) - tile_tot
    running = (within + tile_base[:, None, :]).reshape(m, N_EXPERTS).astype(jnp.int32)
    counts = running[-1]
    padded = (counts + MOE_ROWS - 1) // MOE_ROWS * MOE_ROWS
    pad_end = jnp.cumsum(padded)
    pad_start = pad_end - padded
    dest = jnp.sum(onehot * (running - 1 + pad_start[None, :]), axis=1)
    rows = -(-m // MOE_ROWS) * MOE_ROWS + N_EXPERTS * MOE_ROWS
    nb = rows // MOE_ROWS
    block_e = jnp.minimum(
        jnp.searchsorted(pad_end, jnp.arange(nb, dtype=jnp.int32) * MOE_ROWS, side="right"),
        N_EXPERTS - 1).astype(jnp.int32)
    return dest.astype(jnp.int32), block_e, rows


def _final_kernel(x_ref, g_ref, o_ref):
    x = x_ref[...]
    ms = jnp.mean(x * x, axis=-1, keepdims=True)
    o_ref[...] = x * lax.rsqrt(ms + NORM_EPS) * g_ref[...]


def _final_norm(x2, g):
    tok = x2.shape[0]
    tm = min(1024, tok)
    return pl.pallas_call(
        _final_kernel,
        grid=(tok // tm,),
        in_specs=[pl.BlockSpec((tm, D_MODEL), lambda i: (i, 0)),
                  pl.BlockSpec((1, D_MODEL), lambda i: (0, 0))],
        out_specs=pl.BlockSpec((tm, D_MODEL), lambda i: (i, 0)),
        out_shape=jax.ShapeDtypeStruct((tok, D_MODEL), F32),
        compiler_params=_cparams(("arbitrary",)),
        name="final_norm",
    )(x2, g)


def _rotary_tables(positions):
    half = ROT_DIM // 2
    inv_freq = ROPE_THETA ** (-jnp.arange(0, ROT_DIM, 2, dtype=F32) / ROT_DIM)
    ang = positions.astype(F32).reshape(-1, 1) * inv_freq[None, :]
    cos, sin = jnp.cos(ang), jnp.sin(ang)
    tok = ang.shape[0]
    ones = jnp.ones((tok, HEAD_DIM - ROT_DIM), F32)
    zeros = jnp.zeros((tok, HEAD_DIM - ROT_DIM), F32)
    zh = jnp.zeros((tok, half), F32)
    c64 = jnp.concatenate([cos, cos, ones], axis=1)
    s1_64 = jnp.concatenate([zh, sin, zeros], axis=1)
    s2_64 = jnp.concatenate([-sin, zh, zeros], axis=1)
    dup = lambda t: jnp.concatenate([t, t], axis=1)
    return dup(c64), dup(s1_64), dup(s2_64)


def _lambda_init(layer):
    return 0.8 - 0.6 * math.exp(-0.3 * layer)


def kernel(x, c, positions, ada_w, ada_b, norm1_g, norm2_g, w_in, w_out, attn_lambda, attn_subln_g, rwkv_shift_mu, rwkv_w0, rwkv_w_up, rwkv_a0, rwkv_a_up, rwkv_g_up, rwkv_k_k, rwkv_k_a, rwkv_r_k, rwkv_lnx_g, rwkv_lnx_b, moe_w_group, moe_b_group, moe_w_router, moe_b_router, moe_w_gate, moe_w_up, moe_w_down, final_g):
    bsz, seq, d = x.shape
    depth = ada_w.shape[0]
    tok = bsz * seq
    x2 = x.reshape(tok, d)

    mod = _modulation(c, ada_w, ada_b)
    mod3 = mod.reshape(depth * bsz, 1, 6 * d)
    cos_t, s1_t, s2_t = _rotary_tables(positions)
    masks = _block_masks()
    tt = min(512, seq)

    for layer in range(depth):
        pad = LORA_PAD - LORA
        w_pad = jnp.pad(w_in[layer], ((0, 0), (0, pad))).astype(BF16)
        mu_pad = jnp.pad(rwkv_shift_mu[layer], (0, pad)).reshape(1, RWKV_COLS)
        qa, qb, k, v, p = _in_projection(x2, norm1_g[layer].reshape(1, d), mod3, layer, bsz, seq,
                                         w_pad, mu_pad, cos_t, s1_t, s2_t)
        attn_o = _diff_attention(qa, qb, k, v, attn_lambda[layer],
                                 attn_subln_g[layer].reshape(1, 2 * HEAD_DIM),
                                 _lambda_init(layer), bsz, seq)

        zrow = jnp.zeros((RWKV_WIDTH,), F32)
        prm = jnp.stack([rwkv_w0[layer], rwkv_a0[layer], rwkv_k_k[layer], rwkv_k_a[layer],
                         rwkv_r_k[layer].reshape(RWKV_WIDTH), rwkv_lnx_g[layer],
                         rwkv_lnx_b[layer], zrow])
        up_w = jnp.zeros((LORA_PAD, 3 * RWKV_WIDTH), F32)
        up_w = up_w.at[:W_LORA, :RWKV_WIDTH].set(rwkv_w_up[layer])
        up_w = up_w.at[W_LORA:W_LORA + A_LORA, RWKV_WIDTH:2 * RWKV_WIDTH].set(rwkv_a_up[layer])
        up_w = up_w.at[W_LORA + A_LORA:LORA, 2 * RWKV_WIDTH:].set(rwkv_g_up[layer])
        rwkv_o = _rwkv_mix(p, prm, up_w.astype(BF16), masks, bsz, seq)

        w_o = w_out[layer].astype(BF16)
        w_rt = jnp.zeros((d, ROUTE_LANES), F32)
        w_rt = w_rt.at[:, :N_GROUPS].set(moe_w_group[layer])
        w_rt = w_rt.at[:, N_GROUPS:N_GROUPS + N_EXPERTS].set(moe_w_router[layer])
        b_rt = jnp.zeros((1, ROUTE_LANES), F32)
        b_rt = b_rt.at[0, :N_GROUPS].set(moe_b_group[layer])
        b_rt = b_rt.at[0, N_GROUPS:N_GROUPS + N_EXPERTS].set(moe_b_router[layer])
        x1, h2, rt = _out_projection(attn_o, rwkv_o, x2, w_o[:ATTN_WIDTH], w_o[ATTN_WIDTH:], mod3,
                                     norm2_g[layer].reshape(1, d), w_rt, b_rt, layer, bsz, seq)

        dest, block_e, rows = _routing_tables(rt, tok)
        dest3 = dest.reshape(tok // tt, 1, TOP_K * tt)
        xs = _dispatch(dest3, h2, jnp.zeros((rows, d), F32))
        yb = _experts(block_e, xs, moe_w_gate[layer].astype(BF16), moe_w_up[layer].astype(BF16),
                      moe_w_down[layer].astype(BF16))
        x2 = _combine(dest3, yb, rt, x1, mod3, layer, bsz, seq)

    out = _final_norm(x2, final_g.reshape(1, d))
    return out.reshape(bsz, seq, d)
```

```python
import functools
import math

import jax
import jax.numpy as jnp
from jax import lax
from jax.experimental import pallas as pl
from jax.experimental.pallas import tpu as pltpu

F32 = jnp.float32
BF16 = jnp.bfloat16

D_MODEL = 1024
ATTN_WIDTH = 512
ATTN_HEADS = 4
HEAD_DIM = 64
ROT_DIM = 16
ROPE_THETA = 500000.0
SUBLN_EPS = 1e-5
ATTN_CHUNK = 64

RWKV_WIDTH = 512
RWKV_HEAD = 64
W_LORA, A_LORA, G_LORA = 32, 32, 96
LORA = W_LORA + A_LORA + G_LORA
LORA_PAD = 256
GN_EPS = 64e-5
RWKV_COLS = 3 * RWKV_WIDTH + LORA_PAD
QKV_COLS = 3 * ATTN_WIDTH
WKV_CHUNK = 64
WKV_STEP = 128
SLAB = 256

N_GROUPS = 4
EXPERTS_PER_GROUP = 8
N_EXPERTS = 32
D_EXPERT = 256
TOP_K = 2
ROUTE_LANES = 128
MOE_ROWS = 256
DMA_UNROLL = 8
COUNT_TILE = 256

NORM_EPS = 1e-6
NEG_BIG = -1e30
VMEM_LIMIT = 56 * 1024 * 1024


def _cparams(sem):
    return pltpu.CompilerParams(dimension_semantics=sem, vmem_limit_bytes=VMEM_LIMIT)


def _mm(a, b):
    return jnp.dot(a.astype(BF16), b.astype(BF16), preferred_element_type=F32)


def _mm_nt(a, b):
    return lax.dot_general(a.astype(BF16), b.astype(BF16), (((1,), (1,)), ((), ())),
                           preferred_element_type=F32)


def _mm_tn(a, b):
    return lax.dot_general(a.astype(BF16), b.astype(BF16), (((0,), (0,)), ((), ())),
                           preferred_element_type=F32)


def _mm_split(m_exact, x):
    hi = x.astype(BF16)
    lo = (x - hi.astype(F32)).astype(BF16)
    return (jnp.dot(m_exact, hi, preferred_element_type=F32)
            + jnp.dot(m_exact, lo, preferred_element_type=F32))


def _split_mm(x, m_exact):
    hi = x.astype(BF16)
    lo = (x - hi.astype(F32)).astype(BF16)
    return (jnp.dot(hi, m_exact, preferred_element_type=F32)
            + jnp.dot(lo, m_exact, preferred_element_type=F32))


def _sigmoid(x):
    return 1.0 / (1.0 + jnp.exp(-x))


def _mod_kernel(c_ref, w_ref, b_ref, o_ref):
    c = c_ref[...]
    ca = c * _sigmoid(c)
    o_ref[0] = jnp.dot(ca, w_ref[0], preferred_element_type=F32,
                       precision=lax.Precision.HIGHEST) + b_ref[0]


def _modulation(c, ada_w, ada_b):
    depth, d, n = ada_w.shape
    bsz = c.shape[0]
    nb = n // d
    return pl.pallas_call(
        _mod_kernel,
        grid=(depth, nb),
        in_specs=[pl.BlockSpec((bsz, d), lambda l, j: (0, 0)),
                  pl.BlockSpec((1, d, d), lambda l, j: (l, 0, j)),
                  pl.BlockSpec((1, 1, d), lambda l, j: (l, 0, j))],
        out_specs=pl.BlockSpec((1, bsz, d), lambda l, j: (l, 0, j)),
        out_shape=jax.ShapeDtypeStruct((depth, bsz, n), F32),
        compiler_params=_cparams(("arbitrary", "arbitrary")),
        name="adaln_mod",
    )(c, ada_w, ada_b.reshape(depth, 1, n))


def _inproj_kernel(x_ref, g_ref, sc_ref, sh_ref, w_ref, mu_ref, cos_ref, s1_ref, s2_ref,
                   qa_ref, qb_ref, k_ref, vt_ref, p_ref, carry_ref):
    i = pl.program_id(1)
    tm = x_ref.shape[0]
    x = x_ref[...]
    ms = jnp.mean(x * x, axis=-1, keepdims=True)
    h = x * lax.rsqrt(ms + NORM_EPS) * g_ref[...]
    h = h * (1.0 + sc_ref[0]) + sh_ref[0]
    hb = h.astype(BF16)

    cosv, s1v, s2v = cos_ref[...], s1_ref[...], s2_ref[...]

    def rot128(t):
        return (t * cosv + pltpu.roll(t, ROT_DIM // 2, 1) * s1v
                + pltpu.roll(t, 128 - ROT_DIM // 2, 1) * s2v)

    lane = lax.broadcasted_iota(jnp.int32, (1, 128), 1)
    first_map = lane < HEAD_DIM
    scale = HEAD_DIM ** -0.5 * LOG2E
    qkv = jnp.dot(hb, w_ref[:, :QKV_COLS], preferred_element_type=F32)
    for hd in range(ATTN_HEADS):
        lo, hi = hd * 128, (hd + 1) * 128
        q = rot128(qkv[:, lo:hi]) * scale
        qa_ref[:, lo:hi] = jnp.where(first_map, q, 0.0).astype(BF16)
        qb_ref[:, lo:hi] = jnp.where(first_map, 0.0, q).astype(BF16)
        k_ref[:, lo:hi] = rot128(qkv[:, ATTN_WIDTH + lo:ATTN_WIDTH + hi]).astype(BF16)
    vt_ref[...] = qkv[:, 2 * ATTN_WIDTH:].T.astype(BF16)

    @pl.when(i == 0)
    def _():
        carry_ref[...] = jnp.zeros_like(carry_ref)

    p = jnp.dot(hb, w_ref[:, QKV_COLS:], preferred_element_type=F32)
    row = lax.broadcasted_iota(jnp.int32, p.shape, 0)
    prev = jnp.where(row == 0, carry_ref[...], pltpu.roll(p, 1, 0))
    carry_ref[...] = p[tm - 1:tm, :]
    p_ref[...] = p + (prev - p) * mu_ref[...]


def _in_projection(x2, g1, mod3, layer, bsz, seq, w_pad, mu_pad, cos_t, s1_t, s2_t):
    tm = min(512, seq)
    nt = seq // tm
    tok = x2.shape[0]
    row = lambda b, i: (b * nt + i, 0)
    const = lambda b, i: (0, 0)
    wide = w_pad.shape[1]
    outs = pl.pallas_call(
        _inproj_kernel,
        grid=(bsz, nt),
        in_specs=[pl.BlockSpec((tm, D_MODEL), row),
                  pl.BlockSpec((1, D_MODEL), const),
                  pl.BlockSpec((1, 1, D_MODEL), lambda b, i: (layer * bsz + b, 0, 1)),
                  pl.BlockSpec((1, 1, D_MODEL), lambda b, i: (layer * bsz + b, 0, 0)),
                  pl.BlockSpec((D_MODEL, wide), const),
                  pl.BlockSpec((1, RWKV_COLS), const),
                  pl.BlockSpec((tm, 128), row),
                  pl.BlockSpec((tm, 128), row),
                  pl.BlockSpec((tm, 128), row)],
        out_specs=[pl.BlockSpec((tm, ATTN_WIDTH), row)] * 3
        + [pl.BlockSpec((ATTN_WIDTH, tm), lambda b, i: (b, i)), pl.BlockSpec((tm, RWKV_COLS), row)],
        out_shape=[jax.ShapeDtypeStruct((tok, ATTN_WIDTH), BF16)] * 3
        + [jax.ShapeDtypeStruct((bsz * ATTN_WIDTH, seq), BF16),
           jax.ShapeDtypeStruct((tok, RWKV_COLS), F32)],
        scratch_shapes=[pltpu.VMEM((1, RWKV_COLS), F32)],
        compiler_params=_cparams(("arbitrary", "arbitrary")),
        name="in_proj",
    )(x2, g1, mod3, mod3, w_pad, mu_pad, cos_t, s1_t, s2_t)
    return outs


ATTN_TILE = 512
ATTN_QLANES = 256
ATTN_KEYS = 512
ATTN_AHEAD = 4
LOG2E = 1.4426950408889634


def _attn_kernel(qi_ref, kj_ref, qa_ref, qb_ref, k_ref, vt_ref, lam_ref, g_ref, o_ref,
                 m_ref, l_ref, acc_ref, *, lambda_init):
    step = pl.program_id(1)
    i = qi_ref[step]
    j = kj_ref[step]
    tq = qa_ref.shape[0]
    tk = k_ref.shape[0]
    hq = min(ATTN_QLANES, tq)
    n_half = tq // hq

    @pl.when(j == 0)
    def _():
        m_ref[...] = jnp.full(m_ref.shape, NEG_BIG, F32)
        l_ref[...] = jnp.zeros_like(l_ref)
        acc_ref[...] = jnp.zeros_like(acc_ref)

    kb = min(ATTN_KEYS, tk)

    def accumulate(diagonal):
        def visible(half, k0, kn):
            kc = (lax.broadcasted_iota(jnp.int32, (kn, hq), 0) + k0) // ATTN_CHUNK
            qc = (lax.broadcasted_iota(jnp.int32, (kn, hq), 1) + half * hq) // ATTN_CHUNK
            return kc <= qc

        spans = []
        for half in range(n_half):
            if not diagonal:
                spans += [(half, k0, kb, False) for k0 in range(0, tk, kb)]
            else:
                if half > 0:
                    spans.append((half, 0, half * hq, False))
                spans.append((half, half * hq, hq, True))
        chains = [(half, k0, kn, masked, hd, mp) for (half, k0, kn, masked) in spans
                  for hd in range(ATTN_HEADS) for mp in range(2)]

        def scores(chain):
            half, k0, kn, _, hd, mp = chain
            q_ref = (qa_ref, qb_ref)[mp]
            return lax.dot_general(k_ref[k0:k0 + kn, hd * 128:(hd + 1) * 128],
                                   q_ref[half * hq:(half + 1) * hq, hd * 128:(hd + 1) * 128],
                                   (((1,), (1,)), ((), ())), preferred_element_type=F32)

        ahead = [scores(c) for c in chains[:ATTN_AHEAD]]
        for n, (half, k0, kn, masked, hd, mp) in enumerate(chains):
            s = ahead.pop(0)
            if n + ATTN_AHEAD < len(chains):
                ahead.append(scores(chains[n + ATTN_AHEAD]))
            idx = (2 * hd + mp) * n_half + half
            if masked:
                s = jnp.where(visible(half, k0, kn), s, NEG_BIG)
            m_old = m_ref[idx]
            m_new = jnp.maximum(m_old, jnp.max(s, axis=0, keepdims=True))
            alpha = jnp.exp2(m_old - m_new)
            p = jnp.exp2(s - m_new)
            l_ref[idx] = alpha * l_ref[idx] + jnp.sum(p, axis=0, keepdims=True)
            acc_ref[idx] = alpha * acc_ref[idx] + jnp.dot(
                vt_ref[hd * 128:(hd + 1) * 128, k0:k0 + kn], p.astype(BF16),
                preferred_element_type=F32)
            m_ref[idx] = m_new

    @pl.when(j < i)
    def _():
        accumulate(False)

    @pl.when(j == i)
    def _():
        accumulate(True)
        lv = lam_ref[...]
        lam = (jnp.exp(jnp.sum(lv[0:1] * lv[1:2], axis=-1, keepdims=True))
               - jnp.exp(jnp.sum(lv[2:3] * lv[3:4], axis=-1, keepdims=True)) + lambda_init)
        gcol = jnp.concatenate([g_ref[...]] * (hq // 128), axis=1)
        for hd in range(ATTN_HEADS):
            for half in range(n_half):
                a = (2 * hd) * n_half + half
                b = (2 * hd + 1) * n_half + half
                o = acc_ref[a] / l_ref[a] - lam * (acc_ref[b] / l_ref[b])
                ms = jnp.mean(o * o, axis=0, keepdims=True)
                o = o * lax.rsqrt(ms + SUBLN_EPS) * gcol * (1.0 - lambda_init)
                o_ref[half * hq:(half + 1) * hq, hd * 128:(hd + 1) * 128] = o.T.astype(BF16)


def _diff_attention(qa, qb, k, vt, lam_vecs, subln_g, lambda_init, bsz, seq):
    tq = min(ATTN_TILE, seq)
    nq = seq // tq
    tok = qa.shape[0]
    hq = min(ATTN_QLANES, tq)
    n_stat = 2 * ATTN_HEADS * (tq // hq)
    pairs = [(i, j) for i in range(nq) for j in range(i + 1)]
    qi = jnp.asarray([p[0] for p in pairs], jnp.int32)
    kj = jnp.asarray([p[1] for p in pairs], jnp.int32)
    qmap = lambda b, s, qi, kj: (b * nq + qi[s], 0)
    kmap = lambda b, s, qi, kj: (b * nq + kj[s], 0)
    vmap = lambda b, s, qi, kj: (b, kj[s])
    const = lambda b, s, qi, kj: (0, 0)
    g_col = jnp.broadcast_to(subln_g.reshape(2 * HEAD_DIM, 1), (2 * HEAD_DIM, 128))
    grid_spec = pltpu.PrefetchScalarGridSpec(
        num_scalar_prefetch=2,
        grid=(bsz, len(pairs)),
        in_specs=[pl.BlockSpec((tq, ATTN_WIDTH), qmap),
                  pl.BlockSpec((tq, ATTN_WIDTH), qmap),
                  pl.BlockSpec((tq, ATTN_WIDTH), kmap),
                  pl.BlockSpec((ATTN_WIDTH, tq), vmap),
                  pl.BlockSpec((4, HEAD_DIM), const),
                  pl.BlockSpec((2 * HEAD_DIM, 128), const)],
        out_specs=pl.BlockSpec((tq, ATTN_WIDTH), qmap),
        scratch_shapes=[pltpu.VMEM((n_stat, 1, hq), F32),
                        pltpu.VMEM((n_stat, 1, hq), F32),
                        pltpu.VMEM((n_stat, 2 * HEAD_DIM, hq), F32)],
    )
    return pl.pallas_call(
        functools.partial(_attn_kernel, lambda_init=lambda_init),
        grid_spec=grid_spec,
        out_shape=jax.ShapeDtypeStruct((tok, ATTN_WIDTH), BF16),
        compiler_params=_cparams(("arbitrary", "arbitrary")),
        name="diff_attn",
    )(qi, kj, qa, qb, k, vt, lam_vecs, g_col)


def _softplus(z):
    return jnp.maximum(z, 0.0) + jnp.log(1.0 + jnp.exp(-jnp.abs(z)))


def _rwkv_kernel(p_ref, prm_ref, up_ref, tri_ref, seg_ref, bd_ref, sl_ref, il_ref, o_ref, h_ref):
    t = pl.program_id(1)
    C = WKV_CHUNK
    n_chunks = p_ref.shape[0] // C
    W = RWKV_WIDTH

    @pl.when(t == 0)
    def _():
        h_ref[...] = jnp.zeros_like(h_ref)

    p = p_ref[...]
    r, k, v, lo = p[:, :W], p[:, W:2 * W], p[:, 2 * W:3 * W], p[:, 3 * W:]
    lane = lax.broadcasted_iota(jnp.int32, lo.shape, 1)
    act = jnp.where(lane < W_LORA, jnp.tanh(lo), jnp.where(lane < W_LORA + A_LORA, lo, _sigmoid(lo)))
    up = _mm(act, up_ref[...])
    prm = prm_ref[...]
    w0, a0, k_k, k_a, r_k, ln_g, ln_b = (prm[n:n + 1] for n in range(7))
    w_raw = -_softplus(-(w0 + up[:, :W])) - 0.5
    logd = -jnp.exp(w_raw)
    a_sig = _sigmoid(a0 + up[:, W:2 * W])
    gate = up[:, 2 * W:]

    seg = seg_ref[...]
    kkr = k * k_k
    kk = kkr / jnp.maximum(jnp.sqrt(_split_mm(kkr * kkr, seg)), 1e-12)
    kf = k * (1.0 + (a_sig - 1.0) * k_a)
    bonus = _split_mm(r * kf * r_k, seg) * v
    a_vec = -kk
    b_vec = kk * a_sig

    cum = _mm_split(tri_ref[...], logd)
    cum_end = jnp.concatenate(
        [jnp.broadcast_to(cum[(ch + 1) * C - 1:(ch + 1) * C, :], (C, W)) for ch in range(n_chunks)],
        axis=0)
    d_in = jnp.exp(cum)
    d_ex = jnp.exp(cum - logd)
    d_inv = jnp.exp(-cum)
    d_end = jnp.exp(cum_end)
    d_rest = jnp.exp(cum_end - cum)

    at, rt = a_vec * d_ex, r * d_in
    bt, kt = b_vec * d_inv, kf * d_inv
    be, ke = b_vec * d_rest, kf * d_rest

    bd = bd_ref[...]
    sl = sl_ref[...]
    il = il_ref[...]
    eye = il - sl
    heads = SLAB // RWKV_HEAD
    n_slabs = W // SLAB
    probs = [(ch, g) for ch in range(n_chunks) for g in range(n_slabs)]

    bd_lo = bd.astype(BF16)

    def stack(z, ch, g):
        zz = z[ch * C:(ch + 1) * C, g * SLAB:(g + 1) * SLAB].astype(BF16)
        return jnp.concatenate([zz] * heads, axis=0) * bd_lo

    atm = [stack(at, *pr) for pr in probs]
    btm = [stack(bt, *pr) for pr in probs]
    ktm = [stack(kt, *pr) for pr in probs]
    rtm = [stack(rt, *pr) for pr in probs]
    vm = [stack(v, *pr) for pr in probs]
    l_ab = [sl * _mm_nt(a_, b_) for a_, b_ in zip(atm, btm)]
    l_ak = [sl * _mm_nt(a_, k_) for a_, k_ in zip(atm, ktm)]
    lp = [_mm(l_, l_) for l_ in l_ab]
    inv = [eye + l_ for l_ in l_ab]
    wv = [_mm(l_, v_) for l_, v_ in zip(l_ak, vm)]
    m_rb = [il * _mm_nt(r_, b_) for r_, b_ in zip(rtm, btm)]
    m_rk = [il * _mm_nt(r_, k_) for r_, k_ in zip(rtm, ktm)]
    levels = int(math.log2(C)) - 1
    for lev in range(levels):
        inv_new = [i_ + _mm(i_, l_) for i_, l_ in zip(inv, lp)]
        if lev + 1 < levels:
            lp = [_mm(l_, l_) for l_ in lp]
        inv = inv_new
    abar = [_mm(i_, a_) for i_, a_ in zip(inv, atm)]
    vbar = [_mm(i_, w_) for i_, w_ in zip(inv, wv)]
    bem = [stack(be, *pr) for pr in probs]
    kem = [stack(ke, *pr) for pr in probs]
    rhat = [r_ + _mm(m_, a_) for r_, m_, a_ in zip(rtm, m_rb, abar)]
    yhat = [_mm(mb_, vb_) + _mm(mk_, v_) for mb_, vb_, mk_, v_ in zip(m_rb, vbar, m_rk, vm)]
    gmat = [eye * d_end[ch * C:ch * C + 1, g * SLAB:(g + 1) * SLAB] + _mm_tn(b_, a_)
            for (ch, g), b_, a_ in zip(probs, bem, abar)]
    fmat = [_mm_tn(b_, vb_) + _mm_tn(k_, v_) for b_, vb_, k_, v_ in zip(bem, vbar, kem, vm)]

    hs = [h_ref[g] for g in range(n_slabs)]
    rows = []
    for ch in range(n_chunks):
        ys = []
        for g in range(n_slabs):
            n = ch * n_slabs + g
            ym = _mm(rhat[n], hs[g]) + yhat[n]
            hs[g] = _mm(gmat[n], hs[g]) + fmat[n]
            y = ym[0:C]
            for hh in range(1, heads):
                y = y + ym[hh * C:(hh + 1) * C]
            ys.append(y)
        rows.append(jnp.concatenate(ys, axis=1))
    for g in range(n_slabs):
        h_ref[g] = hs[g]
    y = jnp.concatenate(rows, axis=0) if n_chunks > 1 else rows[0]

    inv_n = 1.0 / RWKV_HEAD
    mu = _split_mm(y, seg) * inv_n
    yc = y - mu
    var = _split_mm(yc * yc, seg) * inv_n
    yn = yc * lax.rsqrt(var + GN_EPS) * ln_g + ln_b
    o_ref[...] = ((yn + bonus) * gate).astype(BF16)


def _block_masks():
    n = SLAB
    r = jnp.arange(n)[:, None]
    c = jnp.arange(n)[None, :]
    same = (r // RWKV_HEAD) == (c // RWKV_HEAD)
    bd = same.astype(F32)
    sl = (same & (c < r)).astype(F32)
    il = (same & (c <= r)).astype(F32)
    w = jnp.arange(RWKV_WIDTH)
    seg = ((w[:, None] // RWKV_HEAD) == (w[None, :] // RWKV_HEAD)).astype(BF16)
    tt = jnp.arange(WKV_STEP)
    tri = ((tt[None, :] <= tt[:, None])
           & (tt[None, :] // WKV_CHUNK == tt[:, None] // WKV_CHUNK)).astype(BF16)
    return tri, seg, bd, sl, il


def _rwkv_mix(p, prm, up_w, masks, bsz, seq):
    C = WKV_STEP
    nc = seq // C
    tok = p.shape[0]
    tri, seg, bd, sl, il = masks
    row = lambda b, t: (b * nc + t, 0)
    const = lambda b, t: (0, 0)
    return pl.pallas_call(
        _rwkv_kernel,
        grid=(bsz, nc),
        in_specs=[pl.BlockSpec((C, RWKV_COLS), row),
                  pl.BlockSpec((8, RWKV_WIDTH), const),
                  pl.BlockSpec((LORA_PAD, 3 * RWKV_WIDTH), const),
                  pl.BlockSpec((C, C), const),
                  pl.BlockSpec((RWKV_WIDTH, RWKV_WIDTH), const),
                  pl.BlockSpec((SLAB, SLAB), const),
                  pl.BlockSpec((SLAB, SLAB), const),
                  pl.BlockSpec((SLAB, SLAB), const)],
        out_specs=pl.BlockSpec((C, RWKV_WIDTH), row),
        out_shape=jax.ShapeDtypeStruct((tok, RWKV_WIDTH), BF16),
        scratch_shapes=[pltpu.VMEM((RWKV_WIDTH // SLAB, SLAB, SLAB), F32)],
        compiler_params=_cparams(("arbitrary", "arbitrary")),
        name="rwkv7_mix",
    )(p, prm, up_w, tri, seg, bd, sl, il)


def _out_kernel(a_ref, r_ref, x_ref, wa_ref, wr_ref, gt_ref, g2_ref, sc_ref, sh_ref, wrt_ref,
                brt_ref, x1_ref, h2_ref, rt_ref):
    mixed = (jnp.dot(a_ref[...], wa_ref[...], preferred_element_type=F32)
             + jnp.dot(r_ref[...], wr_ref[...], preferred_element_type=F32))
    x1 = x_ref[...] + gt_ref[0] * mixed
    x1_ref[...] = x1
    ms = jnp.mean(x1 * x1, axis=-1, keepdims=True)
    h2 = x1 * lax.rsqrt(ms + NORM_EPS) * g2_ref[...]
    h2 = h2 * (1.0 + sc_ref[0]) + sh_ref[0]
    h2_ref[...] = h2
    logits = jnp.dot(h2, wrt_ref[...], preferred_element_type=F32,
                     precision=lax.Precision.HIGHEST) + brt_ref[...]

    lane_i = lax.broadcasted_iota(jnp.int32, logits.shape, 1)
    lane = lane_i.astype(F32)
    far = float(ROUTE_LANES)
    is_g = lane_i < N_GROUPS
    gmax = jnp.max(jnp.where(is_g, logits, NEG_BIG), axis=-1, keepdims=True)
    gsum = jnp.sum(jnp.where(is_g, jnp.exp(logits - gmax), 0.0), axis=-1, keepdims=True)
    g_top = 1.0 / gsum
    g_idx = jnp.min(jnp.where(is_g & (logits == gmax), lane, far), axis=-1, keepdims=True)
    base = N_GROUPS + g_idx * EXPERTS_PER_GROUP
    in_g = (lane >= base) & (lane < base + EXPERTS_PER_GROUP)
    v1 = jnp.max(jnp.where(in_g, logits, NEG_BIG), axis=-1, keepdims=True)
    i1 = jnp.min(jnp.where(in_g & (logits == v1), lane, far), axis=-1, keepdims=True)
    rest = in_g & (lane != i1)
    v2 = jnp.max(jnp.where(rest, logits, NEG_BIG), axis=-1, keepdims=True)
    i2 = jnp.min(jnp.where(rest & (logits == v2), lane, far), axis=-1, keepdims=True)
    e21 = jnp.exp(v2 - v1)
    w1 = g_top / (1.0 + e21)
    w2 = g_top * e21 / (1.0 + e21)
    rt_ref[...] = jnp.where(lane_i == 0, i1 - N_GROUPS,
                            jnp.where(lane_i == 1, i2 - N_GROUPS,
                                      jnp.where(lane_i == 2, w1,
                                                jnp.where(lane_i == 3, w2, 0.0))))


def _out_projection(attn_o, rwkv_o, x2, w_a, w_r, mod3, g2, w_rt, b_rt, layer, bsz, seq):
    tm = min(512, seq)
    nt = seq // tm
    tok = x2.shape[0]
    row = lambda b, i: (b * nt + i, 0)
    const = lambda b, i: (0, 0)
    modspec = lambda col: pl.BlockSpec((1, 1, D_MODEL), lambda b, i: (layer * bsz + b, 0, col))
    return pl.pallas_call(
        _out_kernel,
        grid=(bsz, nt),
        in_specs=[pl.BlockSpec((tm, ATTN_WIDTH), row),
                  pl.BlockSpec((tm, RWKV_WIDTH), row),
                  pl.BlockSpec((tm, D_MODEL), row),
                  pl.BlockSpec((ATTN_WIDTH, D_MODEL), const),
                  pl.BlockSpec((RWKV_WIDTH, D_MODEL), const),
                  modspec(2),
                  pl.BlockSpec((1, D_MODEL), const),
                  modspec(4),
                  modspec(3),
                  pl.BlockSpec((D_MODEL, ROUTE_LANES), const),
                  pl.BlockSpec((1, ROUTE_LANES), const)],
        out_specs=[pl.BlockSpec((tm, D_MODEL), row),
                   pl.BlockSpec((tm, D_MODEL), row),
                   pl.BlockSpec((tm, ROUTE_LANES), row)],
        out_shape=[jax.ShapeDtypeStruct((tok, D_MODEL), F32),
                   jax.ShapeDtypeStruct((tok, D_MODEL), F32),
                   jax.ShapeDtypeStruct((tok, ROUTE_LANES), F32)],
        compiler_params=_cparams(("arbitrary", "arbitrary")),
        name="out_proj_router",
    )(attn_o, rwkv_o, x2, w_a, w_r, mod3, g2, mod3, mod3, w_rt, b_rt)


def _dispatch_kernel(dest_ref, h_ref, xin_ref, xs_ref, sem):
    del xin_ref
    tt = h_ref.shape[0]

    def row_copy(t, kk):
        d = dest_ref[0, 0, TOP_K * t + kk]
        return pltpu.make_async_copy(h_ref.at[pl.ds(t, 1)], xs_ref.at[pl.ds(d, 1)], sem)

    def issue(t, c):
        for kk in range(TOP_K):
            row_copy(t, kk).start()
        return c

    lax.fori_loop(0, tt, issue, 0, unroll=DMA_UNROLL)
    for kk in range(TOP_K):
        pltpu.make_async_copy(h_ref, xs_ref.at[pl.ds(0, tt)], sem).wait()


def _dispatch(dest3, h2, xs_init):
    tok = h2.shape[0]
    nt, _, two_tt = dest3.shape
    tt = two_tt // TOP_K
    return pl.pallas_call(
        _dispatch_kernel,
        grid=(nt,),
        in_specs=[pl.BlockSpec((1, 1, two_tt), lambda i: (i, 0, 0), memory_space=pltpu.SMEM),
                  pl.BlockSpec((tt, D_MODEL), lambda i: (i, 0)),
                  pl.BlockSpec(memory_space=pl.ANY)],
        out_specs=pl.BlockSpec(memory_space=pl.ANY),
        out_shape=jax.ShapeDtypeStruct(xs_init.shape, F32),
        scratch_shapes=[pltpu.SemaphoreType.DMA],
        input_output_aliases={2: 0},
        compiler_params=_cparams(("arbitrary",)),
        name="moe_dispatch",
    )(dest3, h2, xs_init)


def _expert_kernel(be_ref, x_ref, wg_ref, wu_ref, wd_ref, y_ref):
    del be_ref
    xb = x_ref[...].astype(BF16)
    gate = jnp.dot(xb, wg_ref[0].astype(BF16), preferred_element_type=F32)
    up = jnp.dot(xb, wu_ref[0].astype(BF16), preferred_element_type=F32)
    hid = gate * _sigmoid(gate) * up
    y_ref[...] = jnp.dot(hid.astype(BF16), wd_ref[0].astype(BF16), preferred_element_type=F32)


def _experts(block_e, xs, w_gate, w_up, w_down, layer):
    rows = xs.shape[0]
    nb = rows // MOE_ROWS
    grid_spec = pltpu.PrefetchScalarGridSpec(
        num_scalar_prefetch=1,
        grid=(nb,),
        in_specs=[pl.BlockSpec((MOE_ROWS, D_MODEL), lambda i, be: (i, 0)),
                  pl.BlockSpec((1, D_MODEL, D_EXPERT), lambda i, be: (layer * N_EXPERTS + be[i], 0, 0)),
                  pl.BlockSpec((1, D_MODEL, D_EXPERT), lambda i, be: (layer * N_EXPERTS + be[i], 0, 0)),
                  pl.BlockSpec((1, D_EXPERT, D_MODEL), lambda i, be: (layer * N_EXPERTS + be[i], 0, 0))],
        out_specs=pl.BlockSpec((MOE_ROWS, D_MODEL), lambda i, be: (i, 0)),
    )
    return pl.pallas_call(
        _expert_kernel,
        grid_spec=grid_spec,
        out_shape=jax.ShapeDtypeStruct((rows, D_MODEL), F32),
        compiler_params=_cparams(("arbitrary",)),
        name="moe_experts",
    )(block_e, xs, w_gate, w_up, w_down)


def _combine_kernel(dest_ref, yb_ref, rt_ref, x_ref, gt_ref, o_ref, buf_ref, sem):
    tt = x_ref.shape[0]

    def row_copy(t, kk):
        d = dest_ref[0, 0, TOP_K * t + kk]
        return pltpu.make_async_copy(yb_ref.at[pl.ds(d, 1)], buf_ref.at[kk, pl.ds(t, 1)], sem)

    def issue(t, c):
        for kk in range(TOP_K):
            row_copy(t, kk).start()
        return c

    lax.fori_loop(0, tt, issue, 0, unroll=DMA_UNROLL)
    for kk in range(TOP_K):
        pltpu.make_async_copy(yb_ref.at[pl.ds(0, tt)], buf_ref.at[kk], sem).wait()
    rt = rt_ref[...]
    y = buf_ref[0] * rt[:, 2:3] + buf_ref[1] * rt[:, 3:4]
    o_ref[...] = x_ref[...] + gt_ref[0] * y


def _combine(dest3, yb, rt, x1, mod3, layer, bsz, seq):
    tok = x1.shape[0]
    nt, _, two_tt = dest3.shape
    tt = two_tt // TOP_K
    per_seq = seq // tt
    return pl.pallas_call(
        _combine_kernel,
        grid=(nt,),
        in_specs=[pl.BlockSpec((1, 1, two_tt), lambda i: (i, 0, 0), memory_space=pltpu.SMEM),
                  pl.BlockSpec(memory_space=pl.ANY),
                  pl.BlockSpec((tt, ROUTE_LANES), lambda i: (i, 0)),
                  pl.BlockSpec((tt, D_MODEL), lambda i: (i, 0)),
                  pl.BlockSpec((1, 1, D_MODEL), lambda i: (layer * bsz + i // per_seq, 0, 5))],
        out_specs=pl.BlockSpec((tt, D_MODEL), lambda i: (i, 0)),
        out_shape=jax.ShapeDtypeStruct((tok, D_MODEL), F32),
        scratch_shapes=[pltpu.VMEM((TOP_K, tt, D_MODEL), F32), pltpu.SemaphoreType.DMA],
        compiler_params=_cparams(("arbitrary",)),
        name="moe_combine",
    )(dest3, yb, rt, x1, mod3)


def _routing_tables(rt, tok):
    m = tok * TOP_K
    flat_e = rt[:, :TOP_K].astype(jnp.int32).reshape(m)
    onehot = (flat_e[:, None] == jnp.arange(N_EXPERTS, dtype=jnp.int32)[None, :]).astype(jnp.int32)
    ct = min(COUNT_TILE, m)
    tri = (jnp.arange(ct)[None, :] <= jnp.arange(ct)[:, None]).astype(BF16)
    within = jnp.einsum("ts,nse->nte", tri, onehot.reshape(m // ct, ct, N_EXPERTS).astype(BF16),
                        preferred_element_type=F32)
    tile_tot = within[:, -1, :]
    tile_base = jnp.cumsum(tile_tot, axis=0) - tile_tot
    running = (within + tile_base[:, None, :]).reshape(m, N_EXPERTS).astype(jnp.int32)
    counts = running[-1]
    padded = (counts + MOE_ROWS - 1) // MOE_ROWS * MOE_ROWS
    pad_end = jnp.cumsum(padded)
    pad_start = pad_end - padded
    dest = jnp.sum(onehot * (running - 1 + pad_start[None, :]), axis=1)
    rows = -(-m // MOE_ROWS) * MOE_ROWS + N_EXPERTS * MOE_ROWS
    nb = rows // MOE_ROWS
    block_start = jnp.arange(nb, dtype=jnp.int32) * MOE_ROWS
    block_e = jnp.minimum(jnp.sum((pad_end[None, :] <= block_start[:, None]).astype(jnp.int32), axis=1),
                          N_EXPERTS - 1).astype(jnp.int32)
    return dest.astype(jnp.int32), block_e, rows


def _final_kernel(x_ref, g_ref, o_ref):
    x = x_ref[...]
    ms = jnp.mean(x * x, axis=-1, keepdims=True)
    o_ref[...] = x * lax.rsqrt(ms + NORM_EPS) * g_ref[...]


def _final_norm(x2, g):
    tok = x2.shape[0]
    tm = min(1024, tok)
    return pl.pallas_call(
        _final_kernel,
        grid=(tok // tm,),
        in_specs=[pl.BlockSpec((tm, D_MODEL), lambda i: (i, 0)),
                  pl.BlockSpec((1, D_MODEL), lambda i: (0, 0))],
        out_specs=pl.BlockSpec((tm, D_MODEL), lambda i: (i, 0)),
        out_shape=jax.ShapeDtypeStruct((tok, D_MODEL), F32),
        compiler_params=_cparams(("arbitrary",)),
        name="final_norm",
    )(x2, g)


def _rotary_tables(positions):
    half = ROT_DIM // 2
    inv_freq = ROPE_THETA ** (-jnp.arange(0, ROT_DIM, 2, dtype=F32) / ROT_DIM)
    ang = positions.astype(F32).reshape(-1, 1) * inv_freq[None, :]
    cos, sin = jnp.cos(ang), jnp.sin(ang)
    tok = ang.shape[0]
    ones = jnp.ones((tok, HEAD_DIM - ROT_DIM), F32)
    zeros = jnp.zeros((tok, HEAD_DIM - ROT_DIM), F32)
    zh = jnp.zeros((tok, half), F32)
    c64 = jnp.concatenate([cos, cos, ones], axis=1)
    s1_64 = jnp.concatenate([zh, sin, zeros], axis=1)
    s2_64 = jnp.concatenate([-sin, zh, zeros], axis=1)
    dup = lambda t: jnp.concatenate([t, t], axis=1)
    return dup(c64), dup(s1_64), dup(s2_64)


def _lambda_init(layer):
    return 0.8 - 0.6 * math.exp(-0.3 * layer)


def kernel(x, c, positions, ada_w, ada_b, norm1_g, norm2_g, w_in, w_out, attn_lambda, attn_subln_g, rwkv_shift_mu, rwkv_w0, rwkv_w_up, rwkv_a0, rwkv_a_up, rwkv_g_up, rwkv_k_k, rwkv_k_a, rwkv_r_k, rwkv_lnx_g, rwkv_lnx_b, moe_w_group, moe_b_group, moe_w_router, moe_b_router, moe_w_gate, moe_w_up, moe_w_down, final_g):
    bsz, seq, d = x.shape
    depth = ada_w.shape[0]
    tok = bsz * seq
    x2 = x.reshape(tok, d)

    mod = _modulation(c, ada_w, ada_b)
    mod3 = mod.reshape(depth * bsz, 1, 6 * d)
    cos_t, s1_t, s2_t = _rotary_tables(positions)
    masks = _block_masks()
    tt = min(512, seq)

    for layer in range(depth):
        pad = LORA_PAD - LORA
        w_pad = jnp.pad(w_in[layer], ((0, 0), (0, pad))).astype(BF16)
        mu_pad = jnp.pad(rwkv_shift_mu[layer], (0, pad)).reshape(1, RWKV_COLS)
        qa, qb, k, v, p = _in_projection(x2, norm1_g[layer].reshape(1, d), mod3, layer, bsz, seq,
                                         w_pad, mu_pad, cos_t, s1_t, s2_t)
        attn_o = _diff_attention(qa, qb, k, v, attn_lambda[layer],
                                 attn_subln_g[layer].reshape(1, 2 * HEAD_DIM),
                                 _lambda_init(layer), bsz, seq)

        zrow = jnp.zeros((RWKV_WIDTH,), F32)
        prm = jnp.stack([rwkv_w0[layer], rwkv_a0[layer], rwkv_k_k[layer], rwkv_k_a[layer],
                         rwkv_r_k[layer].reshape(RWKV_WIDTH), rwkv_lnx_g[layer],
                         rwkv_lnx_b[layer], zrow])
        up_w = jnp.zeros((LORA_PAD, 3 * RWKV_WIDTH), F32)
        up_w = up_w.at[:W_LORA, :RWKV_WIDTH].set(rwkv_w_up[layer])
        up_w = up_w.at[W_LORA:W_LORA + A_LORA, RWKV_WIDTH:2 * RWKV_WIDTH].set(rwkv_a_up[layer])
        up_w = up_w.at[W_LORA + A_LORA:LORA, 2 * RWKV_WIDTH:].set(rwkv_g_up[layer])
        rwkv_o = _rwkv_mix(p, prm, up_w.astype(BF16), masks, bsz, seq)

        w_o = w_out[layer].astype(BF16)
        w_rt = jnp.zeros((d, ROUTE_LANES), F32)
        w_rt = w_rt.at[:, :N_GROUPS].set(moe_w_group[layer])
        w_rt = w_rt.at[:, N_GROUPS:N_GROUPS + N_EXPERTS].set(moe_w_router[layer])
        b_rt = jnp.zeros((1, ROUTE_LANES), F32)
        b_rt = b_rt.at[0, :N_GROUPS].set(moe_b_group[layer])
        b_rt = b_rt.at[0, N_GROUPS:N_GROUPS + N_EXPERTS].set(moe_b_router[layer])
        x1, h2, rt = _out_projection(attn_o, rwkv_o, x2, w_o[:ATTN_WIDTH], w_o[ATTN_WIDTH:], mod3,
                                     norm2_g[layer].reshape(1, d), w_rt, b_rt, layer, bsz, seq)

        dest, block_e, rows = _routing_tables(rt, tok)
        dest3 = dest.reshape(tok // tt, 1, TOP_K * tt)
        xs = _dispatch(dest3, h2, jnp.zeros((rows, d), F32))
        yb = _experts(block_e, xs, moe_w_gate.reshape(-1, d, D_EXPERT),
                      moe_w_up.reshape(-1, d, D_EXPERT), moe_w_down.reshape(-1, D_EXPERT, d), layer)
        x2 = _combine(dest3, yb, rt, x1, mod3, layer, bsz, seq)

    out = _final_norm(x2, final_g.reshape(1, d))
    return out.reshape(bsz, seq, d)
```

```python
import functools
import math

import jax
import jax.numpy as jnp
from jax import lax
from jax.experimental import pallas as pl
from jax.experimental.pallas import tpu as pltpu

F32 = jnp.float32
BF16 = jnp.bfloat16

D_MODEL = 1024
ATTN_WIDTH = 512
ATTN_HEADS = 4
HEAD_DIM = 64
ROT_DIM = 16
ROPE_THETA = 500000.0
SUBLN_EPS = 1e-5
ATTN_CHUNK = 64

RWKV_WIDTH = 512
RWKV_HEAD = 64
W_LORA, A_LORA, G_LORA = 32, 32, 96
LORA = W_LORA + A_LORA + G_LORA
LORA_PAD = 256
GN_EPS = 64e-5
RWKV_COLS = 3 * RWKV_WIDTH + LORA_PAD
QKV_COLS = 3 * ATTN_WIDTH
WKV_CHUNK = 64
WKV_STEP = 256
SLAB = 256

N_GROUPS = 4
EXPERTS_PER_GROUP = 8
N_EXPERTS = 32
D_EXPERT = 256
TOP_K = 2
ROUTE_LANES = 128
MOE_ROWS = 256
DMA_UNROLL = 8
COUNT_TILE = 256

NORM_EPS = 1e-6
NEG_BIG = -1e30
VMEM_LIMIT = 56 * 1024 * 1024


def _cparams(sem):
    return pltpu.CompilerParams(dimension_semantics=sem, vmem_limit_bytes=VMEM_LIMIT)


def _mm(a, b):
    return jnp.dot(a.astype(BF16), b.astype(BF16), preferred_element_type=F32)


def _mm_nt(a, b):
    return lax.dot_general(a.astype(BF16), b.astype(BF16), (((1,), (1,)), ((), ())),
                           preferred_element_type=F32)


def _mm_tn(a, b):
    return lax.dot_general(a.astype(BF16), b.astype(BF16), (((0,), (0,)), ((), ())),
                           preferred_element_type=F32)


def _mm_split(m_exact, x):
    hi = x.astype(BF16)
    lo = (x - hi.astype(F32)).astype(BF16)
    return (jnp.dot(m_exact, hi, preferred_element_type=F32)
            + jnp.dot(m_exact, lo, preferred_element_type=F32))


def _split_mm(x, m_exact):
    hi = x.astype(BF16)
    lo = (x - hi.astype(F32)).astype(BF16)
    return (jnp.dot(hi, m_exact, preferred_element_type=F32)
            + jnp.dot(lo, m_exact, preferred_element_type=F32))


def _sigmoid(x):
    return 1.0 / (1.0 + jnp.exp(-x))


def _mod_kernel(c_ref, w_ref, b_ref, o_ref):
    c = c_ref[...]
    ca = c * _sigmoid(c)
    o_ref[0] = jnp.dot(ca, w_ref[0], preferred_element_type=F32,
                       precision=lax.Precision.HIGHEST) + b_ref[0]


def _modulation(c, ada_w, ada_b):
    depth, d, n = ada_w.shape
    bsz = c.shape[0]
    nb = n // d
    return pl.pallas_call(
        _mod_kernel,
        grid=(depth, nb),
        in_specs=[pl.BlockSpec((bsz, d), lambda l, j: (0, 0)),
                  pl.BlockSpec((1, d, d), lambda l, j: (l, 0, j)),
                  pl.BlockSpec((1, 1, d), lambda l, j: (l, 0, j))],
        out_specs=pl.BlockSpec((1, bsz, d), lambda l, j: (l, 0, j)),
        out_shape=jax.ShapeDtypeStruct((depth, bsz, n), F32),
        compiler_params=_cparams(("arbitrary", "arbitrary")),
        name="adaln_mod",
    )(c, ada_w, ada_b.reshape(depth, 1, n))


def _inproj_kernel(x_ref, g_ref, sc_ref, sh_ref, w_ref, mu_ref, cos_ref, s1_ref, s2_ref,
                   qa_ref, qb_ref, k_ref, vt_ref, p_ref, carry_ref):
    i = pl.program_id(1)
    tm = x_ref.shape[0]
    x = x_ref[...]
    ms = jnp.mean(x * x, axis=-1, keepdims=True)
    h = x * lax.rsqrt(ms + NORM_EPS) * g_ref[...]
    h = h * (1.0 + sc_ref[0]) + sh_ref[0]
    hb = h.astype(BF16)

    cosv, s1v, s2v = cos_ref[...], s1_ref[...], s2_ref[...]

    def rot128(t):
        return (t * cosv + pltpu.roll(t, ROT_DIM // 2, 1) * s1v
                + pltpu.roll(t, 128 - ROT_DIM // 2, 1) * s2v)

    lane = lax.broadcasted_iota(jnp.int32, (1, 128), 1)
    first_map = lane < HEAD_DIM
    scale = HEAD_DIM ** -0.5 * LOG2E
    qkv = jnp.dot(hb, w_ref[:, :QKV_COLS], preferred_element_type=F32)
    for hd in range(ATTN_HEADS):
        lo, hi = hd * 128, (hd + 1) * 128
        q = rot128(qkv[:, lo:hi]) * scale
        qa_ref[:, lo:hi] = jnp.where(first_map, q, 0.0).astype(BF16)
        qb_ref[:, lo:hi] = jnp.where(first_map, 0.0, q).astype(BF16)
        k_ref[:, lo:hi] = rot128(qkv[:, ATTN_WIDTH + lo:ATTN_WIDTH + hi]).astype(BF16)
    vt_ref[...] = qkv[:, 2 * ATTN_WIDTH:].T.astype(BF16)

    @pl.when(i == 0)
    def _():
        carry_ref[...] = jnp.zeros_like(carry_ref)

    p = jnp.dot(hb, w_ref[:, QKV_COLS:], preferred_element_type=F32)
    row = lax.broadcasted_iota(jnp.int32, p.shape, 0)
    prev = jnp.where(row == 0, carry_ref[...], pltpu.roll(p, 1, 0))
    carry_ref[...] = p[tm - 1:tm, :]
    p_ref[...] = p + (prev - p) * mu_ref[...]


def _in_projection(x2, g1, mod3, layer, bsz, seq, w_pad, mu_pad, cos_t, s1_t, s2_t):
    tm = min(512, seq)
    nt = seq // tm
    tok = x2.shape[0]
    row = lambda b, i: (b * nt + i, 0)
    const = lambda b, i: (0, 0)
    wide = w_pad.shape[1]
    outs = pl.pallas_call(
        _inproj_kernel,
        grid=(bsz, nt),
        in_specs=[pl.BlockSpec((tm, D_MODEL), row),
                  pl.BlockSpec((1, D_MODEL), const),
                  pl.BlockSpec((1, 1, D_MODEL), lambda b, i: (layer * bsz + b, 0, 1)),
                  pl.BlockSpec((1, 1, D_MODEL), lambda b, i: (layer * bsz + b, 0, 0)),
                  pl.BlockSpec((D_MODEL, wide), const),
                  pl.BlockSpec((1, RWKV_COLS), const),
                  pl.BlockSpec((tm, 128), row),
                  pl.BlockSpec((tm, 128), row),
                  pl.BlockSpec((tm, 128), row)],
        out_specs=[pl.BlockSpec((tm, ATTN_WIDTH), row)] * 3
        + [pl.BlockSpec((ATTN_WIDTH, tm), lambda b, i: (b, i)), pl.BlockSpec((tm, RWKV_COLS), row)],
        out_shape=[jax.ShapeDtypeStruct((tok, ATTN_WIDTH), BF16)] * 3
        + [jax.ShapeDtypeStruct((bsz * ATTN_WIDTH, seq), BF16),
           jax.ShapeDtypeStruct((tok, RWKV_COLS), F32)],
        scratch_shapes=[pltpu.VMEM((1, RWKV_COLS), F32)],
        compiler_params=_cparams(("arbitrary", "arbitrary")),
        name="in_proj",
    )(x2, g1, mod3, mod3, w_pad, mu_pad, cos_t, s1_t, s2_t)
    return outs


ATTN_TILE = 512
ATTN_QLANES = 256
ATTN_KEYS = 512
ATTN_AHEAD = 4
LOG2E = 1.4426950408889634


def _attn_kernel(qi_ref, kj_ref, qa_ref, qb_ref, k_ref, vt_ref, lam_ref, g_ref, o_ref,
                 m_ref, l_ref, acc_ref, *, lambda_init):
    step = pl.program_id(1)
    i = qi_ref[step]
    j = kj_ref[step]
    tq = qa_ref.shape[0]
    tk = k_ref.shape[0]
    hq = min(ATTN_QLANES, tq)
    n_half = tq // hq

    @pl.when(j == 0)
    def _():
        m_ref[...] = jnp.full(m_ref.shape, NEG_BIG, F32)
        l_ref[...] = jnp.zeros_like(l_ref)
        acc_ref[...] = jnp.zeros_like(acc_ref)

    kb = min(ATTN_KEYS, tk)

    def accumulate(diagonal):
        def visible(half, k0, kn):
            kc = (lax.broadcasted_iota(jnp.int32, (kn, hq), 0) + k0) // ATTN_CHUNK
            qc = (lax.broadcasted_iota(jnp.int32, (kn, hq), 1) + half * hq) // ATTN_CHUNK
            return kc <= qc

        spans = []
        for half in range(n_half):
            if not diagonal:
                spans += [(half, k0, kb, False) for k0 in range(0, tk, kb)]
            else:
                if half > 0:
                    spans.append((half, 0, half * hq, False))
                spans.append((half, half * hq, hq, True))
        chains = [(half, k0, kn, masked, hd, mp) for (half, k0, kn, masked) in spans
                  for hd in range(ATTN_HEADS) for mp in range(2)]

        def scores(chain):
            half, k0, kn, _, hd, mp = chain
            q_ref = (qa_ref, qb_ref)[mp]
            return lax.dot_general(k_ref[k0:k0 + kn, hd * 128:(hd + 1) * 128],
                                   q_ref[half * hq:(half + 1) * hq, hd * 128:(hd + 1) * 128],
                                   (((1,), (1,)), ((), ())), preferred_element_type=F32)

        ahead = [scores(c) for c in chains[:ATTN_AHEAD]]
        for n, (half, k0, kn, masked, hd, mp) in enumerate(chains):
            s = ahead.pop(0)
            if n + ATTN_AHEAD < len(chains):
                ahead.append(scores(chains[n + ATTN_AHEAD]))
            idx = (2 * hd + mp) * n_half + half
            if masked:
                s = jnp.where(visible(half, k0, kn), s, NEG_BIG)
            m_old = m_ref[idx]
            m_new = jnp.maximum(m_old, jnp.max(s, axis=0, keepdims=True))
            alpha = jnp.exp2(m_old - m_new)
            p = jnp.exp2(s - m_new)
            l_ref[idx] = alpha * l_ref[idx] + jnp.sum(p, axis=0, keepdims=True)
            acc_ref[idx] = alpha * acc_ref[idx] + jnp.dot(
                vt_ref[hd * 128:(hd + 1) * 128, k0:k0 + kn], p.astype(BF16),
                preferred_element_type=F32)
            m_ref[idx] = m_new

    @pl.when(j < i)
    def _():
        accumulate(False)

    @pl.when(j == i)
    def _():
        accumulate(True)
        lv = lam_ref[...]
        lam = (jnp.exp(jnp.sum(lv[0:1] * lv[1:2], axis=-1, keepdims=True))
               - jnp.exp(jnp.sum(lv[2:3] * lv[3:4], axis=-1, keepdims=True)) + lambda_init)
        gcol = jnp.concatenate([g_ref[...]] * (hq // 128), axis=1)
        for hd in range(ATTN_HEADS):
            for half in range(n_half):
                a = (2 * hd) * n_half + half
                b = (2 * hd + 1) * n_half + half
                o = acc_ref[a] / l_ref[a] - lam * (acc_ref[b] / l_ref[b])
                ms = jnp.mean(o * o, axis=0, keepdims=True)
                o = o * lax.rsqrt(ms + SUBLN_EPS) * gcol * (1.0 - lambda_init)
                o_ref[half * hq:(half + 1) * hq, hd * 128:(hd + 1) * 128] = o.T.astype(BF16)


def _diff_attention(qa, qb, k, vt, lam_vecs, subln_g, lambda_init, bsz, seq):
    tq = min(ATTN_TILE, seq)
    nq = seq // tq
    tok = qa.shape[0]
    hq = min(ATTN_QLANES, tq)
    n_stat = 2 * ATTN_HEADS * (tq // hq)
    pairs = [(i, j) for i in range(nq) for j in range(i + 1)]
    qi = jnp.asarray([p[0] for p in pairs], jnp.int32)
    kj = jnp.asarray([p[1] for p in pairs], jnp.int32)
    qmap = lambda b, s, qi, kj: (b * nq + qi[s], 0)
    kmap = lambda b, s, qi, kj: (b * nq + kj[s], 0)
    vmap = lambda b, s, qi, kj: (b, kj[s])
    const = lambda b, s, qi, kj: (0, 0)
    g_col = jnp.broadcast_to(subln_g.reshape(2 * HEAD_DIM, 1), (2 * HEAD_DIM, 128))
    grid_spec = pltpu.PrefetchScalarGridSpec(
        num_scalar_prefetch=2,
        grid=(bsz, len(pairs)),
        in_specs=[pl.BlockSpec((tq, ATTN_WIDTH), qmap),
                  pl.BlockSpec((tq, ATTN_WIDTH), qmap),
                  pl.BlockSpec((tq, ATTN_WIDTH), kmap),
                  pl.BlockSpec((ATTN_WIDTH, tq), vmap),
                  pl.BlockSpec((4, HEAD_DIM), const),
                  pl.BlockSpec((2 * HEAD_DIM, 128), const)],
        out_specs=pl.BlockSpec((tq, ATTN_WIDTH), qmap),
        scratch_shapes=[pltpu.VMEM((n_stat, 1, hq), F32),
                        pltpu.VMEM((n_stat, 1, hq), F32),
                        pltpu.VMEM((n_stat, 2 * HEAD_DIM, hq), F32)],
    )
    return pl.pallas_call(
        functools.partial(_attn_kernel, lambda_init=lambda_init),
        grid_spec=grid_spec,
        out_shape=jax.ShapeDtypeStruct((tok, ATTN_WIDTH), BF16),
        compiler_params=_cparams(("arbitrary", "arbitrary")),
        name="diff_attn",
    )(qi, kj, qa, qb, k, vt, lam_vecs, g_col)


def _softplus(z):
    return jnp.maximum(z, 0.0) + jnp.log(1.0 + jnp.exp(-jnp.abs(z)))


def _rwkv_kernel(p_ref, prm_ref, up_ref, tri_ref, bd_ref, sl_ref, il_ref, eye_ref,
                 o_ref, h_ref):
    t = pl.program_id(1)
    C = WKV_CHUNK
    n_chunks = p_ref.shape[0] // C
    W = RWKV_WIDTH

    @pl.when(t == 0)
    def _():
        h_ref[...] = jnp.zeros_like(h_ref)

    p = p_ref[...]
    r, k, v, lo = p[:, :W], p[:, W:2 * W], p[:, 2 * W:3 * W], p[:, 3 * W:]
    lane = lax.broadcasted_iota(jnp.int32, lo.shape, 1)
    act = jnp.where(lane < W_LORA, jnp.tanh(lo), jnp.where(lane < W_LORA + A_LORA, lo, _sigmoid(lo)))
    up = _mm(act, up_ref[...])
    prm = prm_ref[...]
    w0, a0, k_k, k_a, r_k, ln_g, ln_b = (prm[n:n + 1] for n in range(7))
    w_raw = -_softplus(-(w0 + up[:, :W])) - 0.5
    logd = -jnp.exp(w_raw)
    a_sig = _sigmoid(a0 + up[:, W:2 * W])
    gate = up[:, 2 * W:]

    bd = bd_ref[...]
    bd_lo = bd.astype(BF16)

    def head_sum(z):
        return jnp.concatenate([_mm(z[:, g * SLAB:(g + 1) * SLAB], bd_lo)
                                for g in range(W // SLAB)], axis=1)

    kkr = k * k_k
    kk = kkr / jnp.maximum(jnp.sqrt(head_sum(kkr * kkr)), 1e-12)
    kf = k * (1.0 + (a_sig - 1.0) * k_a)
    bonus = head_sum(r * kf * r_k) * v
    a_vec = -kk
    b_vec = kk * a_sig

    cum = _mm_split(tri_ref[...], logd)
    cum_end = jnp.concatenate(
        [jnp.broadcast_to(cum[(ch + 1) * C - 1:(ch + 1) * C, :], (C, W)) for ch in range(n_chunks)],
        axis=0)
    d_in = jnp.exp(cum)
    d_ex = jnp.exp(cum - logd)
    d_inv = jnp.exp(-cum)
    d_end = jnp.exp(cum_end)
    d_rest = jnp.exp(cum_end - cum)

    at, rt = a_vec * d_ex, r * d_in
    bt, kt = b_vec * d_inv, kf * d_inv
    be, ke = b_vec * d_rest, kf * d_rest

    sl = sl_ref[...]
    il = il_ref[...]
    eye_c = il - sl
    eye_s = eye_ref[...]
    heads = SLAB // RWKV_HEAD
    n_slabs = W // SLAB
    probs = [(ch, g) for ch in range(n_chunks) for g in range(n_slabs)]

    def cut(z, ch, g):
        return z[ch * C:(ch + 1) * C, g * SLAB:(g + 1) * SLAB]

    def stack(zz):
        return jnp.concatenate([zz.astype(BF16)] * heads, axis=0) * bd_lo

    ats = [cut(at, *pr) for pr in probs]
    rts = [cut(rt, *pr) for pr in probs]
    vs = [cut(v, *pr) for pr in probs]
    btm = [stack(cut(bt, *pr)) for pr in probs]
    ktm = [stack(cut(kt, *pr)) for pr in probs]
    vm = [stack(v_) for v_ in vs]
    l_ab = [sl * _mm_nt(a_, b_) for a_, b_ in zip(ats, btm)]
    l_ak = [sl * _mm_nt(a_, k_) for a_, k_ in zip(ats, ktm)]
    lpm = [stack(l_) for l_ in l_ab]
    lp = [_mm(l_, m_) for l_, m_ in zip(l_ab, lpm)]
    inv = [eye_c + l_ for l_ in l_ab]
    wv = [_mm(l_, v_) for l_, v_ in zip(l_ak, vm)]
    m_rb = [il * _mm_nt(r_, b_) for r_, b_ in zip(rts, btm)]
    m_rk = [il * _mm_nt(r_, k_) for r_, k_ in zip(rts, ktm)]
    levels = int(math.log2(C)) - 1
    for lev in range(levels):
        lpm = [stack(l_) for l_ in lp]
        inv_new = [i_ + _mm(i_, m_) for i_, m_ in zip(inv, lpm)]
        if lev + 1 < levels:
            lp = [_mm(l_, m_) for l_, m_ in zip(lp, lpm)]
        inv = inv_new
    abar = [_mm(i_, stack(a_)) for i_, a_ in zip(inv, ats)]
    vbar = [_mm(i_, stack(w_)) for i_, w_ in zip(inv, wv)]
    abm = [stack(a_) for a_ in abar]
    vbm = [stack(v_) for v_ in vbar]
    rhat = [r_ + _mm(m_, a_) for r_, m_, a_ in zip(rts, m_rb, abm)]
    yhat = [_mm(mb_, vb_) + _mm(mk_, v_) for mb_, vb_, mk_, v_ in zip(m_rb, vbm, m_rk, vm)]
    bes = [cut(be, *pr) for pr in probs]
    kes = [cut(ke, *pr) for pr in probs]
    gmat = [eye_s * cut(d_end, *pr)[0:1] + bd * _mm_tn(b_, a_) for pr, b_, a_ in zip(probs, bes, abar)]
    fmat = [bd * _mm_tn(jnp.concatenate([b_, k_], axis=0), jnp.concatenate([vb_, v_], axis=0))
            for b_, k_, vb_, v_ in zip(bes, kes, vbar, vs)]

    hs = [h_ref[g] for g in range(n_slabs)]
    rows = []
    for ch in range(n_chunks):
        ys = []
        for g in range(n_slabs):
            n = ch * n_slabs + g
            ys.append(_mm(rhat[n], hs[g]) + yhat[n])
            hs[g] = _mm(gmat[n], hs[g]) + fmat[n]
        rows.append(jnp.concatenate(ys, axis=1))
    for g in range(n_slabs):
        h_ref[g] = hs[g]
    y = jnp.concatenate(rows, axis=0) if n_chunks > 1 else rows[0]

    inv_n = 1.0 / RWKV_HEAD
    mu = head_sum(y) * inv_n
    yc = y - mu
    var = head_sum(yc * yc) * inv_n
    yn = yc * lax.rsqrt(var + GN_EPS) * ln_g + ln_b
    o_ref[...] = ((yn + bonus) * gate).astype(BF16)


def _block_masks():
    n = SLAB
    r = jnp.arange(n)[:, None]
    c = jnp.arange(n)[None, :]
    same = (r // RWKV_HEAD) == (c // RWKV_HEAD)
    bd = same.astype(F32)
    eye = (r == c).astype(F32)
    t = jnp.arange(WKV_CHUNK)[:, None]
    s_in_head = c % RWKV_HEAD
    sl = (s_in_head < t).astype(F32)
    il = (s_in_head <= t).astype(F32)
    tt = jnp.arange(WKV_STEP)
    tri = ((tt[None, :] <= tt[:, None])
           & (tt[None, :] // WKV_CHUNK == tt[:, None] // WKV_CHUNK)).astype(BF16)
    return tri, bd, sl, il, eye


def _rwkv_mix(p, prm, up_w, masks, bsz, seq):
    C = WKV_STEP
    nc = seq // C
    tok = p.shape[0]
    tri, bd, sl, il, eye = masks
    row = lambda b, t: (b * nc + t, 0)
    const = lambda b, t: (0, 0)
    return pl.pallas_call(
        _rwkv_kernel,
        grid=(bsz, nc),
        in_specs=[pl.BlockSpec((C, RWKV_COLS), row),
                  pl.BlockSpec((8, RWKV_WIDTH), const),
                  pl.BlockSpec((LORA_PAD, 3 * RWKV_WIDTH), const),
                  pl.BlockSpec((C, C), const),
                  pl.BlockSpec((SLAB, SLAB), const),
                  pl.BlockSpec((WKV_CHUNK, SLAB), const),
                  pl.BlockSpec((WKV_CHUNK, SLAB), const),
                  pl.BlockSpec((SLAB, SLAB), const)],
        out_specs=pl.BlockSpec((C, RWKV_WIDTH), row),
        out_shape=jax.ShapeDtypeStruct((tok, RWKV_WIDTH), BF16),
        scratch_shapes=[pltpu.VMEM((RWKV_WIDTH // SLAB, SLAB, SLAB), F32)],
        compiler_params=_cparams(("arbitrary", "arbitrary")),
        name="rwkv7_mix",
    )(p, prm, up_w, tri, bd, sl, il, eye)


def _out_kernel(a_ref, r_ref, x_ref, wa_ref, wr_ref, gt_ref, g2_ref, sc_ref, sh_ref, wrt_ref,
                brt_ref, x1_ref, h2_ref, rt_ref):
    mixed = (jnp.dot(a_ref[...], wa_ref[...], preferred_element_type=F32)
             + jnp.dot(r_ref[...], wr_ref[...], preferred_element_type=F32))
    x1 = x_ref[...] + gt_ref[0] * mixed
    x1_ref[...] = x1
    ms = jnp.mean(x1 * x1, axis=-1, keepdims=True)
    h2 = x1 * lax.rsqrt(ms + NORM_EPS) * g2_ref[...]
    h2 = h2 * (1.0 + sc_ref[0]) + sh_ref[0]
    h2_ref[...] = h2
    h_hi = h2.astype(BF16)
    h_lo = (h2 - h_hi.astype(F32)).astype(BF16)
    logits = (jnp.dot(h_hi, wrt_ref[0], preferred_element_type=F32)
              + jnp.dot(h_lo, wrt_ref[0], preferred_element_type=F32)
              + jnp.dot(h_hi, wrt_ref[1], preferred_element_type=F32)) + brt_ref[...]

    lane_i = lax.broadcasted_iota(jnp.int32, logits.shape, 1)
    lane = lane_i.astype(F32)
    far = float(ROUTE_LANES)
    is_g = lane_i < N_GROUPS
    gmax = jnp.max(jnp.where(is_g, logits, NEG_BIG), axis=-1, keepdims=True)
    gsum = jnp.sum(jnp.where(is_g, jnp.exp(logits - gmax), 0.0), axis=-1, keepdims=True)
    g_top = 1.0 / gsum
    g_idx = jnp.min(jnp.where(is_g & (logits == gmax), lane, far), axis=-1, keepdims=True)
    base = N_GROUPS + g_idx * EXPERTS_PER_GROUP
    in_g = (lane >= base) & (lane < base + EXPERTS_PER_GROUP)
    v1 = jnp.max(jnp.where(in_g, logits, NEG_BIG), axis=-1, keepdims=True)
    i1 = jnp.min(jnp.where(in_g & (logits == v1), lane, far), axis=-1, keepdims=True)
    rest = in_g & (lane != i1)
    v2 = jnp.max(jnp.where(rest, logits, NEG_BIG), axis=-1, keepdims=True)
    i2 = jnp.min(jnp.where(rest & (logits == v2), lane, far), axis=-1, keepdims=True)
    e21 = jnp.exp(v2 - v1)
    w1 = g_top / (1.0 + e21)
    w2 = g_top * e21 / (1.0 + e21)
    rt_ref[...] = jnp.where(lane_i == 0, i1 - N_GROUPS,
                            jnp.where(lane_i == 1, i2 - N_GROUPS,
                                      jnp.where(lane_i == 2, w1,
                                                jnp.where(lane_i == 3, w2, 0.0))))


def _out_projection(attn_o, rwkv_o, x2, w_a, w_r, mod3, g2, w_rt, b_rt, layer, bsz, seq):
    tm = min(512, seq)
    nt = seq // tm
    tok = x2.shape[0]
    row = lambda b, i: (b * nt + i, 0)
    const = lambda b, i: (0, 0)
    modspec = lambda col: pl.BlockSpec((1, 1, D_MODEL), lambda b, i: (layer * bsz + b, 0, col))
    return pl.pallas_call(
        _out_kernel,
        grid=(bsz, nt),
        in_specs=[pl.BlockSpec((tm, ATTN_WIDTH), row),
                  pl.BlockSpec((tm, RWKV_WIDTH), row),
                  pl.BlockSpec((tm, D_MODEL), row),
                  pl.BlockSpec((ATTN_WIDTH, D_MODEL), const),
                  pl.BlockSpec((RWKV_WIDTH, D_MODEL), const),
                  modspec(2),
                  pl.BlockSpec((1, D_MODEL), const),
                  modspec(4),
                  modspec(3),
                  pl.BlockSpec((2, D_MODEL, ROUTE_LANES), lambda b, i: (0, 0, 0)),
                  pl.BlockSpec((1, ROUTE_LANES), const)],
        out_specs=[pl.BlockSpec((tm, D_MODEL), row),
                   pl.BlockSpec((tm, D_MODEL), row),
                   pl.BlockSpec((tm, ROUTE_LANES), row)],
        out_shape=[jax.ShapeDtypeStruct((tok, D_MODEL), F32),
                   jax.ShapeDtypeStruct((tok, D_MODEL), F32),
                   jax.ShapeDtypeStruct((tok, ROUTE_LANES), F32)],
        compiler_params=_cparams(("arbitrary", "arbitrary")),
        name="out_proj_router",
    )(attn_o, rwkv_o, x2, w_a, w_r, mod3, g2, mod3, mod3, w_rt, b_rt)


def _dispatch_kernel(dest_ref, h_ref, xin_ref, xs_ref, sem):
    del xin_ref
    tt = h_ref.shape[0]

    def row_copy(t, kk):
        d = dest_ref[0, 0, TOP_K * t + kk]
        return pltpu.make_async_copy(h_ref.at[pl.ds(t, 1)], xs_ref.at[pl.ds(d, 1)], sem)

    def issue(t, c):
        for kk in range(TOP_K):
            row_copy(t, kk).start()
        return c

    lax.fori_loop(0, tt, issue, 0, unroll=DMA_UNROLL)
    for kk in range(TOP_K):
        pltpu.make_async_copy(h_ref, xs_ref.at[pl.ds(0, tt)], sem).wait()


def _dispatch(dest3, h2, xs_init):
    tok = h2.shape[0]
    nt, _, two_tt = dest3.shape
    tt = two_tt // TOP_K
    return pl.pallas_call(
        _dispatch_kernel,
        grid=(nt,),
        in_specs=[pl.BlockSpec((1, 1, two_tt), lambda i: (i, 0, 0), memory_space=pltpu.SMEM),
                  pl.BlockSpec((tt, D_MODEL), lambda i: (i, 0)),
                  pl.BlockSpec(memory_space=pl.ANY)],
        out_specs=pl.BlockSpec(memory_space=pl.ANY),
        out_shape=jax.ShapeDtypeStruct(xs_init.shape, F32),
        scratch_shapes=[pltpu.SemaphoreType.DMA],
        input_output_aliases={2: 0},
        compiler_params=_cparams(("arbitrary",)),
        name="moe_dispatch",
    )(dest3, h2, xs_init)


def _expert_kernel(be_ref, x_ref, wg_ref, wu_ref, wd_ref, y_ref):
    del be_ref
    xb = x_ref[...].astype(BF16)
    gate = jnp.dot(xb, wg_ref[0].astype(BF16), preferred_element_type=F32)
    up = jnp.dot(xb, wu_ref[0].astype(BF16), preferred_element_type=F32)
    hid = gate * _sigmoid(gate) * up
    y_ref[...] = jnp.dot(hid.astype(BF16), wd_ref[0].astype(BF16), preferred_element_type=F32)


def _experts(block_e, xs, w_gate, w_up, w_down, layer):
    rows = xs.shape[0]
    nb = rows // MOE_ROWS
    grid_spec = pltpu.PrefetchScalarGridSpec(
        num_scalar_prefetch=1,
        grid=(nb,),
        in_specs=[pl.BlockSpec((MOE_ROWS, D_MODEL), lambda i, be: (i, 0)),
                  pl.BlockSpec((1, D_MODEL, D_EXPERT), lambda i, be: (layer * N_EXPERTS + be[i], 0, 0)),
                  pl.BlockSpec((1, D_MODEL, D_EXPERT), lambda i, be: (layer * N_EXPERTS + be[i], 0, 0)),
                  pl.BlockSpec((1, D_EXPERT, D_MODEL), lambda i, be: (layer * N_EXPERTS + be[i], 0, 0))],
        out_specs=pl.BlockSpec((MOE_ROWS, D_MODEL), lambda i, be: (i, 0)),
    )
    return pl.pallas_call(
        _expert_kernel,
        grid_spec=grid_spec,
        out_shape=jax.ShapeDtypeStruct((rows, D_MODEL), F32),
        compiler_params=_cparams(("arbitrary",)),
        name="moe_experts",
    )(block_e, xs, w_gate, w_up, w_down)


def _combine_kernel(dest_ref, yb_ref, rt_ref, x_ref, gt_ref, o_ref, buf_ref, sem):
    tt = x_ref.shape[0]

    def row_copy(t, kk):
        d = dest_ref[0, 0, TOP_K * t + kk]
        return pltpu.make_async_copy(yb_ref.at[pl.ds(d, 1)], buf_ref.at[kk, pl.ds(t, 1)], sem)

    def issue(t, c):
        for kk in range(TOP_K):
            row_copy(t, kk).start()
        return c

    lax.fori_loop(0, tt, issue, 0, unroll=DMA_UNROLL)
    for kk in range(TOP_K):
        pltpu.make_async_copy(yb_ref.at[pl.ds(0, tt)], buf_ref.at[kk], sem).wait()
    rt = rt_ref[...]
    y = buf_ref[0] * rt[:, 2:3] + buf_ref[1] * rt[:, 3:4]
    o_ref[...] = x_ref[...] + gt_ref[0] * y


def _combine(dest3, yb, rt, x1, mod3, layer, bsz, seq):
    tok = x1.shape[0]
    nt, _, two_tt = dest3.shape
    tt = two_tt // TOP_K
    per_seq = seq // tt
    return pl.pallas_call(
        _combine_kernel,
        grid=(nt,),
        in_specs=[pl.BlockSpec((1, 1, two_tt), lambda i: (i, 0, 0), memory_space=pltpu.SMEM),
                  pl.BlockSpec(memory_space=pl.ANY),
                  pl.BlockSpec((tt, ROUTE_LANES), lambda i: (i, 0)),
                  pl.BlockSpec((tt, D_MODEL), lambda i: (i, 0)),
                  pl.BlockSpec((1, 1, D_MODEL), lambda i: (layer * bsz + i // per_seq, 0, 5))],
        out_specs=pl.BlockSpec((tt, D_MODEL), lambda i: (i, 0)),
        out_shape=jax.ShapeDtypeStruct((tok, D_MODEL), F32),
        scratch_shapes=[pltpu.VMEM((TOP_K, tt, D_MODEL), F32), pltpu.SemaphoreType.DMA],
        compiler_params=_cparams(("arbitrary",)),
        name="moe_combine",
    )(dest3, yb, rt, x1, mod3)


def _routing_tables(rt, tok):
    m = tok * TOP_K
    flat_e = rt[:, :TOP_K].astype(jnp.int32).reshape(m)
    onehot = (flat_e[:, None] == jnp.arange(N_EXPERTS, dtype=jnp.int32)[None, :]).astype(jnp.int32)
    ct = min(COUNT_TILE, m)
    tri = (jnp.arange(ct)[None, :] <= jnp.arange(ct)[:, None]).astype(BF16)
    within = jnp.einsum("ts,nse->nte", tri, onehot.reshape(m // ct, ct, N_EXPERTS).astype(BF16),
                        preferred_element_type=F32)
    tile_tot = within[:, -1, :]
    tile_base = jnp.cumsum(tile_tot, axis=0) - tile_tot
    running = (within + tile_base[:, None, :]).reshape(m, N_EXPERTS).astype(jnp.int32)
    counts = running[-1]
    padded = (counts + MOE_ROWS - 1) // MOE_ROWS * MOE_ROWS
    pad_end = jnp.cumsum(padded)
    pad_start = pad_end - padded
    dest = jnp.sum(onehot * (running - 1 + pad_start[None, :]), axis=1)
    rows = -(-m // MOE_ROWS) * MOE_ROWS + N_EXPERTS * MOE_ROWS
    nb = rows // MOE_ROWS
    block_start = jnp.arange(nb, dtype=jnp.int32) * MOE_ROWS
    block_e = jnp.minimum(jnp.sum((pad_end[None, :] <= block_start[:, None]).astype(jnp.int32), axis=1),
                          N_EXPERTS - 1).astype(jnp.int32)
    return dest.astype(jnp.int32), block_e, rows


def _final_kernel(x_ref, g_ref, o_ref):
    x = x_ref[...]
    ms = jnp.mean(x * x, axis=-1, keepdims=True)
    o_ref[...] = x * lax.rsqrt(ms + NORM_EPS) * g_ref[...]


def _final_norm(x2, g):
    tok = x2.shape[0]
    tm = min(1024, tok)
    return pl.pallas_call(
        _final_kernel,
        grid=(tok // tm,),
        in_specs=[pl.BlockSpec((tm, D_MODEL), lambda i: (i, 0)),
                  pl.BlockSpec((1, D_MODEL), lambda i: (0, 0))],
        out_specs=pl.BlockSpec((tm, D_MODEL), lambda i: (i, 0)),
        out_shape=jax.ShapeDtypeStruct((tok, D_MODEL), F32),
        compiler_params=_cparams(("arbitrary",)),
        name="final_norm",
    )(x2, g)


def _rotary_tables(positions):
    half = ROT_DIM // 2
    inv_freq = ROPE_THETA ** (-jnp.arange(0, ROT_DIM, 2, dtype=F32) / ROT_DIM)
    ang = positions.astype(F32).reshape(-1, 1) * inv_freq[None, :]
    cos, sin = jnp.cos(ang), jnp.sin(ang)
    tok = ang.shape[0]
    ones = jnp.ones((tok, HEAD_DIM - ROT_DIM), F32)
    zeros = jnp.zeros((tok, HEAD_DIM - ROT_DIM), F32)
    zh = jnp.zeros((tok, half), F32)
    c64 = jnp.concatenate([cos, cos, ones], axis=1)
    s1_64 = jnp.concatenate([zh, sin, zeros], axis=1)
    s2_64 = jnp.concatenate([-sin, zh, zeros], axis=1)
    dup = lambda t: jnp.concatenate([t, t], axis=1)
    return dup(c64), dup(s1_64), dup(s2_64)


def _lambda_init(layer):
    return 0.8 - 0.6 * math.exp(-0.3 * layer)


def kernel(x, c, positions, ada_w, ada_b, norm1_g, norm2_g, w_in, w_out, attn_lambda, attn_subln_g, rwkv_shift_mu, rwkv_w0, rwkv_w_up, rwkv_a0, rwkv_a_up, rwkv_g_up, rwkv_k_k, rwkv_k_a, rwkv_r_k, rwkv_lnx_g, rwkv_lnx_b, moe_w_group, moe_b_group, moe_w_router, moe_b_router, moe_w_gate, moe_w_up, moe_w_down, final_g):
    bsz, seq, d = x.shape
    depth = ada_w.shape[0]
    tok = bsz * seq
    x2 = x.reshape(tok, d)

    mod = _modulation(c, ada_w, ada_b)
    mod3 = mod.reshape(depth * bsz, 1, 6 * d)
    cos_t, s1_t, s2_t = _rotary_tables(positions)
    masks = _block_masks()
    tt = min(512, seq)

    for layer in range(depth):
        pad = LORA_PAD - LORA
        w_pad = jnp.pad(w_in[layer], ((0, 0), (0, pad))).astype(BF16)
        mu_pad = jnp.pad(rwkv_shift_mu[layer], (0, pad)).reshape(1, RWKV_COLS)
        qa, qb, k, v, p = _in_projection(x2, norm1_g[layer].reshape(1, d), mod3, layer, bsz, seq,
                                         w_pad, mu_pad, cos_t, s1_t, s2_t)
        attn_o = _diff_attention(qa, qb, k, v, attn_lambda[layer],
                                 attn_subln_g[layer].reshape(1, 2 * HEAD_DIM),
                                 _lambda_init(layer), bsz, seq)

        zrow = jnp.zeros((RWKV_WIDTH,), F32)
        prm = jnp.stack([rwkv_w0[layer], rwkv_a0[layer], rwkv_k_k[layer], rwkv_k_a[layer],
                         rwkv_r_k[layer].reshape(RWKV_WIDTH), rwkv_lnx_g[layer],
                         rwkv_lnx_b[layer], zrow])
        up_w = jnp.zeros((LORA_PAD, 3 * RWKV_WIDTH), F32)
        up_w = up_w.at[:W_LORA, :RWKV_WIDTH].set(rwkv_w_up[layer])
        up_w = up_w.at[W_LORA:W_LORA + A_LORA, RWKV_WIDTH:2 * RWKV_WIDTH].set(rwkv_a_up[layer])
        up_w = up_w.at[W_LORA + A_LORA:LORA, 2 * RWKV_WIDTH:].set(rwkv_g_up[layer])
        rwkv_o = _rwkv_mix(p, prm, up_w.astype(BF16), masks, bsz, seq)

        w_o = w_out[layer].astype(BF16)
        w_rt = jnp.zeros((d, ROUTE_LANES), F32)
        w_rt = w_rt.at[:, :N_GROUPS].set(moe_w_group[layer])
        w_rt = w_rt.at[:, N_GROUPS:N_GROUPS + N_EXPERTS].set(moe_w_router[layer])
        w_rt_hi = w_rt.astype(BF16)
        w_rt = jnp.stack([w_rt_hi, (w_rt - w_rt_hi.astype(F32)).astype(BF16)])
        b_rt = jnp.zeros((1, ROUTE_LANES), F32)
        b_rt = b_rt.at[0, :N_GROUPS].set(moe_b_group[layer])
        b_rt = b_rt.at[0, N_GROUPS:N_GROUPS + N_EXPERTS].set(moe_b_router[layer])
        x1, h2, rt = _out_projection(attn_o, rwkv_o, x2, w_o[:ATTN_WIDTH], w_o[ATTN_WIDTH:], mod3,
                                     norm2_g[layer].reshape(1, d), w_rt, b_rt, layer, bsz, seq)

        dest, block_e, rows = _routing_tables(rt, tok)
        dest3 = dest.reshape(tok // tt, 1, TOP_K * tt)
        xs = _dispatch(dest3, h2, jnp.zeros((rows, d), F32))
        yb = _experts(block_e, xs, moe_w_gate.reshape(-1, d, D_EXPERT),
                      moe_w_up.reshape(-1, d, D_EXPERT), moe_w_down.reshape(-1, D_EXPERT, d), layer)
        x2 = _combine(dest3, yb, rt, x1, mod3, layer, bsz, seq)

    out = _final_norm(x2, final_g.reshape(1, d))
    return out.reshape(bsz, seq, d)
```

```python
import functools
import math

import jax
import jax.numpy as jnp
from jax import lax
from jax.experimental import pallas as pl
from jax.experimental.pallas import tpu as pltpu

F32 = jnp.float32
BF16 = jnp.bfloat16

D_MODEL = 1024
ATTN_WIDTH = 512
ATTN_HEADS = 4
HEAD_DIM = 64
V_ROWS = 2 * HEAD_DIM + 16
ROT_DIM = 16
ROPE_THETA = 500000.0
SUBLN_EPS = 1e-5
ATTN_CHUNK = 64

RWKV_WIDTH = 512
RWKV_HEAD = 64
W_LORA, A_LORA, G_LORA = 32, 32, 96
LORA = W_LORA + A_LORA + G_LORA
LORA_PAD = 256
GN_EPS = 64e-5
RWKV_COLS = 3 * RWKV_WIDTH + LORA_PAD
QKV_COLS = 3 * ATTN_WIDTH
WKV_CHUNK = 64
WKV_STEP = 256
SLAB = 256

N_GROUPS = 4
EXPERTS_PER_GROUP = 8
N_EXPERTS = 32
D_EXPERT = 256
TOP_K = 2
ROUTE_LANES = 128
MOE_ROWS = 256
DMA_UNROLL = 8
COUNT_TILE = 256

NORM_EPS = 1e-6
NEG_BIG = -1e30
VMEM_LIMIT = 56 * 1024 * 1024


def _cparams(sem):
    return pltpu.CompilerParams(dimension_semantics=sem, vmem_limit_bytes=VMEM_LIMIT)


def _mm(a, b):
    return jnp.dot(a.astype(BF16), b.astype(BF16), preferred_element_type=F32)


def _mm_nt(a, b):
    return lax.dot_general(a.astype(BF16), b.astype(BF16), (((1,), (1,)), ((), ())),
                           preferred_element_type=F32)


def _mm_tn(a, b):
    return lax.dot_general(a.astype(BF16), b.astype(BF16), (((0,), (0,)), ((), ())),
                           preferred_element_type=F32)


def _mm_split(m_exact, x):
    hi = x.astype(BF16)
    lo = (x - hi.astype(F32)).astype(BF16)
    return (jnp.dot(m_exact, hi, preferred_element_type=F32)
            + jnp.dot(m_exact, lo, preferred_element_type=F32))


def _split_mm(x, m_exact):
    hi = x.astype(BF16)
    lo = (x - hi.astype(F32)).astype(BF16)
    return (jnp.dot(hi, m_exact, preferred_element_type=F32)
            + jnp.dot(lo, m_exact, preferred_element_type=F32))


def _sigmoid(x):
    return 1.0 / (1.0 + jnp.exp(-x))


def _mod_kernel(c_ref, w_ref, b_ref, o_ref):
    c = c_ref[...]
    ca = c * _sigmoid(c)
    o_ref[0] = jnp.dot(ca, w_ref[0], preferred_element_type=F32,
                       precision=lax.Precision.HIGHEST) + b_ref[0]


def _modulation(c, ada_w, ada_b):
    depth, d, n = ada_w.shape
    bsz = c.shape[0]
    nb = n // d
    return pl.pallas_call(
        _mod_kernel,
        grid=(depth, nb),
        in_specs=[pl.BlockSpec((bsz, d), lambda l, j: (0, 0)),
                  pl.BlockSpec((1, d, d), lambda l, j: (l, 0, j)),
                  pl.BlockSpec((1, 1, d), lambda l, j: (l, 0, j))],
        out_specs=pl.BlockSpec((1, bsz, d), lambda l, j: (l, 0, j)),
        out_shape=jax.ShapeDtypeStruct((depth, bsz, n), F32),
        compiler_params=_cparams(("arbitrary", "arbitrary")),
        name="adaln_mod",
    )(c, ada_w, ada_b.reshape(depth, 1, n))


def _inproj_kernel(x_ref, g_ref, sc_ref, sh_ref, w_ref, mu_ref, cos_ref, s1_ref, s2_ref,
                   qa_ref, qb_ref, k_ref, vt_ref, p_ref, carry_ref):
    i = pl.program_id(1)
    tm = x_ref.shape[0]
    x = x_ref[...]
    ms = jnp.mean(x * x, axis=-1, keepdims=True)
    h = x * lax.rsqrt(ms + NORM_EPS) * g_ref[...]
    h = h * (1.0 + sc_ref[0]) + sh_ref[0]
    hb = h.astype(BF16)

    cosv, s1v, s2v = cos_ref[...], s1_ref[...], s2_ref[...]

    def rot128(t):
        return (t * cosv + pltpu.roll(t, ROT_DIM // 2, 1) * s1v
                + pltpu.roll(t, 128 - ROT_DIM // 2, 1) * s2v)

    lane = lax.broadcasted_iota(jnp.int32, (1, 128), 1)
    first_map = lane < HEAD_DIM
    scale = HEAD_DIM ** -0.5 * LOG2E
    qkv = jnp.dot(hb, w_ref[:, :QKV_COLS], preferred_element_type=F32)
    for hd in range(ATTN_HEADS):
        lo, hi = hd * 128, (hd + 1) * 128
        q = rot128(qkv[:, lo:hi]) * scale
        qa_ref[:, lo:hi] = jnp.where(first_map, q, 0.0).astype(BF16)
        qb_ref[:, lo:hi] = jnp.where(first_map, 0.0, q).astype(BF16)
        k_ref[:, lo:hi] = rot128(qkv[:, ATTN_WIDTH + lo:ATTN_WIDTH + hi]).astype(BF16)
    vt = qkv[:, 2 * ATTN_WIDTH:].T
    for hd in range(ATTN_HEADS):
        r0 = hd * V_ROWS
        vt_ref[r0:r0 + 128, :] = vt[hd * 128:(hd + 1) * 128].astype(BF16)
        vt_ref[r0 + 128:r0 + V_ROWS, :] = jnp.ones((V_ROWS - 128, tm), BF16)

    @pl.when(i == 0)
    def _():
        carry_ref[...] = jnp.zeros_like(carry_ref)

    p = jnp.dot(hb, w_ref[:, QKV_COLS:], preferred_element_type=F32)
    row = lax.broadcasted_iota(jnp.int32, p.shape, 0)
    prev = jnp.where(row == 0, carry_ref[...], pltpu.roll(p, 1, 0))
    carry_ref[...] = p[tm - 1:tm, :]
    p_ref[...] = p + (prev - p) * mu_ref[...]


def _in_projection(x2, g1, mod3, layer, bsz, seq, w_pad, mu_pad, cos_t, s1_t, s2_t):
    tm = min(512, seq)
    nt = seq // tm
    tok = x2.shape[0]
    row = lambda b, i: (b * nt + i, 0)
    const = lambda b, i: (0, 0)
    wide = w_pad.shape[1]
    outs = pl.pallas_call(
        _inproj_kernel,
        grid=(bsz, nt),
        in_specs=[pl.BlockSpec((tm, D_MODEL), row),
                  pl.BlockSpec((1, D_MODEL), const),
                  pl.BlockSpec((1, 1, D_MODEL), lambda b, i: (layer * bsz + b, 0, 1)),
                  pl.BlockSpec((1, 1, D_MODEL), lambda b, i: (layer * bsz + b, 0, 0)),
                  pl.BlockSpec((D_MODEL, wide), const),
                  pl.BlockSpec((1, RWKV_COLS), const),
                  pl.BlockSpec((tm, 128), row),
                  pl.BlockSpec((tm, 128), row),
                  pl.BlockSpec((tm, 128), row)],
        out_specs=[pl.BlockSpec((tm, ATTN_WIDTH), row)] * 3
        + [pl.BlockSpec((ATTN_HEADS * V_ROWS, tm), lambda b, i: (b, i)),
           pl.BlockSpec((tm, RWKV_COLS), row)],
        out_shape=[jax.ShapeDtypeStruct((tok, ATTN_WIDTH), BF16)] * 3
        + [jax.ShapeDtypeStruct((bsz * ATTN_HEADS * V_ROWS, seq), BF16),
           jax.ShapeDtypeStruct((tok, RWKV_COLS), F32)],
        scratch_shapes=[pltpu.VMEM((1, RWKV_COLS), F32)],
        compiler_params=_cparams(("arbitrary", "arbitrary")),
        name="in_proj",
    )(x2, g1, mod3, mod3, w_pad, mu_pad, cos_t, s1_t, s2_t)
    return outs


ATTN_TILE = 512
ATTN_QLANES = 256
ATTN_KEYS = 512
ATTN_AHEAD = 4
LOG2E = 1.4426950408889634


def _attn_kernel(qi_ref, kj_ref, qa_ref, qb_ref, k_ref, vt_ref, lam_ref, g_ref, o_ref,
                 m_ref, acc_ref, *, lambda_init):
    step = pl.program_id(1)
    i = qi_ref[step]
    j = kj_ref[step]
    tq = qa_ref.shape[0]
    tk = k_ref.shape[0]
    hq = min(ATTN_QLANES, tq)
    n_half = tq // hq

    @pl.when(j == 0)
    def _():
        m_ref[...] = jnp.full(m_ref.shape, NEG_BIG, F32)
        acc_ref[...] = jnp.zeros_like(acc_ref)

    kb = min(ATTN_KEYS, tk)

    def accumulate(diagonal):
        def visible(half, k0, kn):
            kc = (lax.broadcasted_iota(jnp.int32, (kn, hq), 0) + k0) // ATTN_CHUNK
            qc = (lax.broadcasted_iota(jnp.int32, (kn, hq), 1) + half * hq) // ATTN_CHUNK
            return kc <= qc

        spans = []
        for half in range(n_half):
            if not diagonal:
                spans += [(half, k0, kb, False) for k0 in range(0, tk, kb)]
            else:
                if half > 0:
                    spans.append((half, 0, half * hq, False))
                spans.append((half, half * hq, hq, True))
        chains = [(half, k0, kn, masked, hd, mp) for (half, k0, kn, masked) in spans
                  for hd in range(ATTN_HEADS) for mp in range(2)]

        def scores(chain):
            half, k0, kn, _, hd, mp = chain
            q_ref = (qa_ref, qb_ref)[mp]
            return lax.dot_general(k_ref[k0:k0 + kn, hd * 128:(hd + 1) * 128],
                                   q_ref[half * hq:(half + 1) * hq, hd * 128:(hd + 1) * 128],
                                   (((1,), (1,)), ((), ())), preferred_element_type=F32)

        ahead = [scores(c) for c in chains[:ATTN_AHEAD]]
        for n, (half, k0, kn, masked, hd, mp) in enumerate(chains):
            s = ahead.pop(0)
            if n + ATTN_AHEAD < len(chains):
                ahead.append(scores(chains[n + ATTN_AHEAD]))
            idx = (2 * hd + mp) * n_half + half
            if masked:
                s = jnp.where(visible(half, k0, kn), s, NEG_BIG)
            m_old = m_ref[idx]
            m_new = jnp.maximum(m_old, jnp.max(s, axis=0, keepdims=True))
            alpha = jnp.exp2(m_old - m_new)
            p = jnp.exp2((s - m_new).astype(BF16))
            acc_ref[idx] = alpha * acc_ref[idx] + jnp.dot(
                vt_ref[hd * V_ROWS:(hd + 1) * V_ROWS, k0:k0 + kn], p,
                preferred_element_type=F32)
            m_ref[idx] = m_new

    @pl.when(j < i)
    def _():
        accumulate(False)

    @pl.when(j == i)
    def _():
        accumulate(True)
        lv = lam_ref[...]
        lam = (jnp.exp(jnp.sum(lv[0:1] * lv[1:2], axis=-1, keepdims=True))
               - jnp.exp(jnp.sum(lv[2:3] * lv[3:4], axis=-1, keepdims=True)) + lambda_init)
        gcol = jnp.concatenate([g_ref[...]] * (hq // 128), axis=1)
        for hd in range(ATTN_HEADS):
            for half in range(n_half):
                a = (2 * hd) * n_half + half
                b = (2 * hd + 1) * n_half + half
                dv = 2 * HEAD_DIM
                o = (acc_ref[a, :dv] / acc_ref[a, dv:dv + 1]
                     - lam * (acc_ref[b, :dv] / acc_ref[b, dv:dv + 1]))
                ms = jnp.mean(o * o, axis=0, keepdims=True)
                o = o * lax.rsqrt(ms + SUBLN_EPS) * gcol * (1.0 - lambda_init)
                o_ref[half * hq:(half + 1) * hq, hd * 128:(hd + 1) * 128] = o.T.astype(BF16)


def _diff_attention(qa, qb, k, vt, lam_vecs, subln_g, lambda_init, bsz, seq):
    tq = min(ATTN_TILE, seq)
    nq = seq // tq
    tok = qa.shape[0]
    hq = min(ATTN_QLANES, tq)
    n_stat = 2 * ATTN_HEADS * (tq // hq)
    pairs = [(i, j) for i in range(nq) for j in range(i + 1)]
    qi = jnp.asarray([p[0] for p in pairs], jnp.int32)
    kj = jnp.asarray([p[1] for p in pairs], jnp.int32)
    qmap = lambda b, s, qi, kj: (b * nq + qi[s], 0)
    kmap = lambda b, s, qi, kj: (b * nq + kj[s], 0)
    vmap = lambda b, s, qi, kj: (b, kj[s])
    const = lambda b, s, qi, kj: (0, 0)
    g_col = jnp.broadcast_to(subln_g.reshape(2 * HEAD_DIM, 1), (2 * HEAD_DIM, 128))
    grid_spec = pltpu.PrefetchScalarGridSpec(
        num_scalar_prefetch=2,
        grid=(bsz, len(pairs)),
        in_specs=[pl.BlockSpec((tq, ATTN_WIDTH), qmap),
                  pl.BlockSpec((tq, ATTN_WIDTH), qmap),
                  pl.BlockSpec((tq, ATTN_WIDTH), kmap),
                  pl.BlockSpec((ATTN_HEADS * V_ROWS, tq), vmap),
                  pl.BlockSpec((4, HEAD_DIM), const),
                  pl.BlockSpec((2 * HEAD_DIM, 128), const)],
        out_specs=pl.BlockSpec((tq, ATTN_WIDTH), qmap),
        scratch_shapes=[pltpu.VMEM((n_stat, 1, hq), F32),
                        pltpu.VMEM((n_stat, V_ROWS, hq), F32)],
    )
    return pl.pallas_call(
        functools.partial(_attn_kernel, lambda_init=lambda_init),
        grid_spec=grid_spec,
        out_shape=jax.ShapeDtypeStruct((tok, ATTN_WIDTH), BF16),
        compiler_params=_cparams(("arbitrary", "arbitrary")),
        name="diff_attn",
    )(qi, kj, qa, qb, k, vt, lam_vecs, g_col)


def _softplus(z):
    return jnp.maximum(z, 0.0) + jnp.log(1.0 + jnp.exp(-jnp.abs(z)))


def _rwkv_kernel(p_ref, prm_ref, up_ref, tri_ref, bd_ref, sl_ref, il_ref, eye_ref,
                 o_ref, h_ref):
    t = pl.program_id(1)
    C = WKV_CHUNK
    n_chunks = p_ref.shape[0] // C
    W = RWKV_WIDTH

    @pl.when(t == 0)
    def _():
        h_ref[...] = jnp.zeros_like(h_ref)

    p = p_ref[...]
    r, k, v, lo = p[:, :W], p[:, W:2 * W], p[:, 2 * W:3 * W], p[:, 3 * W:]
    lane = lax.broadcasted_iota(jnp.int32, lo.shape, 1)
    act = jnp.where(lane < W_LORA, jnp.tanh(lo), jnp.where(lane < W_LORA + A_LORA, lo, _sigmoid(lo)))
    up = _mm(act, up_ref[...])
    prm = prm_ref[...]
    w0, a0, k_k, k_a, r_k, ln_g, ln_b = (prm[n:n + 1] for n in range(7))
    w_raw = -_softplus(-(w0 + up[:, :W])) - 0.5
    logd = -jnp.exp(w_raw)
    a_sig = _sigmoid(a0 + up[:, W:2 * W])
    gate = up[:, 2 * W:]

    bd = bd_ref[...]
    bd_lo = bd.astype(BF16)

    def head_sum(z):
        return jnp.concatenate([_mm(z[:, g * SLAB:(g + 1) * SLAB], bd_lo)
                                for g in range(W // SLAB)], axis=1)

    kkr = k * k_k
    kk = kkr / jnp.maximum(jnp.sqrt(head_sum(kkr * kkr)), 1e-12)
    kf = k * (1.0 + (a_sig - 1.0) * k_a)
    bonus = head_sum(r * kf * r_k) * v
    a_vec = -kk
    b_vec = kk * a_sig

    cum = _mm_split(tri_ref[...], logd)
    cum_end = jnp.concatenate(
        [jnp.broadcast_to(cum[(ch + 1) * C - 1:(ch + 1) * C, :], (C, W)) for ch in range(n_chunks)],
        axis=0)
    d_in = jnp.exp(cum)
    d_ex = jnp.exp(cum - logd)
    d_inv = jnp.exp(-cum)
    d_end = jnp.exp(cum_end)
    d_rest = jnp.exp(cum_end - cum)

    at, rt = a_vec * d_ex, r * d_in
    bt, kt = b_vec * d_inv, kf * d_inv
    be, ke = b_vec * d_rest, kf * d_rest

    sl = sl_ref[...]
    il = il_ref[...]
    eye_c = il - sl
    eye_s = eye_ref[...]
    heads = SLAB // RWKV_HEAD
    n_slabs = W // SLAB
    probs = [(ch, g) for ch in range(n_chunks) for g in range(n_slabs)]

    def cut(z, ch, g):
        return z[ch * C:(ch + 1) * C, g * SLAB:(g + 1) * SLAB]

    def stack(zz):
        return jnp.concatenate([zz.astype(BF16)] * heads, axis=0) * bd_lo

    ats = [cut(at, *pr) for pr in probs]
    rts = [cut(rt, *pr) for pr in probs]
    vs = [cut(v, *pr) for pr in probs]
    btm = [stack(cut(bt, *pr)) for pr in probs]
    ktm = [stack(cut(kt, *pr)) for pr in probs]
    vm = [stack(v_) for v_ in vs]
    l_ab = [sl * _mm_nt(a_, b_) for a_, b_ in zip(ats, btm)]
    l_ak = [sl * _mm_nt(a_, k_) for a_, k_ in zip(ats, ktm)]
    lpm = [stack(l_) for l_ in l_ab]
    lp = [_mm(l_, m_) for l_, m_ in zip(l_ab, lpm)]
    inv = [eye_c + l_ for l_ in l_ab]
    wv = [_mm(l_, v_) for l_, v_ in zip(l_ak, vm)]
    m_rb = [il * _mm_nt(r_, b_) for r_, b_ in zip(rts, btm)]
    m_rk = [il * _mm_nt(r_, k_) for r_, k_ in zip(rts, ktm)]
    levels = int(math.log2(C)) - 1
    for lev in range(levels):
        lpm = [stack(l_) for l_ in lp]
        inv_new = [i_ + _mm(i_, m_) for i_, m_ in zip(inv, lpm)]
        if lev + 1 < levels:
            lp = [_mm(l_, m_) for l_, m_ in zip(lp, lpm)]
        inv = inv_new
    abar = [_mm(i_, stack(a_)) for i_, a_ in zip(inv, ats)]
    vbar = [_mm(i_, stack(w_)) for i_, w_ in zip(inv, wv)]
    abm = [stack(a_) for a_ in abar]
    vbm = [stack(v_) for v_ in vbar]
    rhat = [r_ + _mm(m_, a_) for r_, m_, a_ in zip(rts, m_rb, abm)]
    yhat = [_mm(mb_, vb_) + _mm(mk_, v_) for mb_, vb_, mk_, v_ in zip(m_rb, vbm, m_rk, vm)]
    bes = [cut(be, *pr) for pr in probs]
    kes = [cut(ke, *pr) for pr in probs]
    gmat = [eye_s * cut(d_end, *pr)[0:1] + bd * _mm_tn(b_, a_) for pr, b_, a_ in zip(probs, bes, abar)]
    fmat = [bd * _mm_tn(jnp.concatenate([b_, k_], axis=0), jnp.concatenate([vb_, v_], axis=0))
            for b_, k_, vb_, v_ in zip(bes, kes, vbar, vs)]

    hs = [h_ref[g] for g in range(n_slabs)]
    rows = []
    for ch in range(n_chunks):
        ys = []
        for g in range(n_slabs):
            n = ch * n_slabs + g
            ys.append(_mm(rhat[n], hs[g]) + yhat[n])
            hs[g] = _mm(gmat[n], hs[g]) + fmat[n]
        rows.append(jnp.concatenate(ys, axis=1))
    for g in range(n_slabs):
        h_ref[g] = hs[g]
    y = jnp.concatenate(rows, axis=0) if n_chunks > 1 else rows[0]

    inv_n = 1.0 / RWKV_HEAD
    mu = head_sum(y) * inv_n
    yc = y - mu
    var = head_sum(yc * yc) * inv_n
    yn = yc * lax.rsqrt(var + GN_EPS) * ln_g + ln_b
    o_ref[...] = ((yn + bonus) * gate).astype(BF16)


def _block_masks():
    n = SLAB
    r = jnp.arange(n)[:, None]
    c = jnp.arange(n)[None, :]
    same = (r // RWKV_HEAD) == (c // RWKV_HEAD)
    bd = same.astype(F32)
    eye = (r == c).astype(F32)
    t = jnp.arange(WKV_CHUNK)[:, None]
    s_in_head = c % RWKV_HEAD
    sl = (s_in_head < t).astype(F32)
    il = (s_in_head <= t).astype(F32)
    tt = jnp.arange(WKV_STEP)
    tri = ((tt[None, :] <= tt[:, None])
           & (tt[None, :] // WKV_CHUNK == tt[:, None] // WKV_CHUNK)).astype(BF16)
    return tri, bd, sl, il, eye


def _rwkv_mix(p, prm, up_w, masks, bsz, seq):
    C = WKV_STEP
    nc = seq // C
    tok = p.shape[0]
    tri, bd, sl, il, eye = masks
    row = lambda b, t: (b * nc + t, 0)
    const = lambda b, t: (0, 0)
    return pl.pallas_call(
        _rwkv_kernel,
        grid=(bsz, nc),
        in_specs=[pl.BlockSpec((C, RWKV_COLS), row),
                  pl.BlockSpec((8, RWKV_WIDTH), const),
                  pl.BlockSpec((LORA_PAD, 3 * RWKV_WIDTH), const),
                  pl.BlockSpec((C, C), const),
                  pl.BlockSpec((SLAB, SLAB), const),
                  pl.BlockSpec((WKV_CHUNK, SLAB), const),
                  pl.BlockSpec((WKV_CHUNK, SLAB), const),
                  pl.BlockSpec((SLAB, SLAB), const)],
        out_specs=pl.BlockSpec((C, RWKV_WIDTH), row),
        out_shape=jax.ShapeDtypeStruct((tok, RWKV_WIDTH), BF16),
        scratch_shapes=[pltpu.VMEM((RWKV_WIDTH // SLAB, SLAB, SLAB), F32)],
        compiler_params=_cparams(("arbitrary", "arbitrary")),
        name="rwkv7_mix",
    )(p, prm, up_w, tri, bd, sl, il, eye)


def _out_kernel(a_ref, r_ref, x_ref, wa_ref, wr_ref, gt_ref, g2_ref, sc_ref, sh_ref, wrt_ref,
                brt_ref, x1_ref, h2_ref, rt_ref):
    mixed = (jnp.dot(a_ref[...], wa_ref[...], preferred_element_type=F32)
             + jnp.dot(r_ref[...], wr_ref[...], preferred_element_type=F32))
    x1 = x_ref[...] + gt_ref[0] * mixed
    x1_ref[...] = x1
    ms = jnp.mean(x1 * x1, axis=-1, keepdims=True)
    h2 = x1 * lax.rsqrt(ms + NORM_EPS) * g2_ref[...]
    h2 = h2 * (1.0 + sc_ref[0]) + sh_ref[0]
    h2_ref[...] = h2
    h_hi = h2.astype(BF16)
    h_lo = (h2 - h_hi.astype(F32)).astype(BF16)
    logits = (jnp.dot(h_hi, wrt_ref[0], preferred_element_type=F32)
              + jnp.dot(h_lo, wrt_ref[0], preferred_element_type=F32)
              + jnp.dot(h_hi, wrt_ref[1], preferred_element_type=F32)) + brt_ref[...]

    lane_i = lax.broadcasted_iota(jnp.int32, logits.shape, 1)
    lane = lane_i.astype(F32)
    far = float(ROUTE_LANES)
    is_g = lane_i < N_GROUPS
    gmax = jnp.max(jnp.where(is_g, logits, NEG_BIG), axis=-1, keepdims=True)
    gsum = jnp.sum(jnp.where(is_g, jnp.exp(logits - gmax), 0.0), axis=-1, keepdims=True)
    g_top = 1.0 / gsum
    g_idx = jnp.min(jnp.where(is_g & (logits == gmax), lane, far), axis=-1, keepdims=True)
    base = N_GROUPS + g_idx * EXPERTS_PER_GROUP
    in_g = (lane >= base) & (lane < base + EXPERTS_PER_GROUP)
    v1 = jnp.max(jnp.where(in_g, logits, NEG_BIG), axis=-1, keepdims=True)
    i1 = jnp.min(jnp.where(in_g & (logits == v1), lane, far), axis=-1, keepdims=True)
    rest = in_g & (lane != i1)
    v2 = jnp.max(jnp.where(rest, logits, NEG_BIG), axis=-1, keepdims=True)
    i2 = jnp.min(jnp.where(rest & (logits == v2), lane, far), axis=-1, keepdims=True)
    e21 = jnp.exp(v2 - v1)
    w1 = g_top / (1.0 + e21)
    w2 = g_top * e21 / (1.0 + e21)
    rt_ref[...] = jnp.where(lane_i == 0, i1 - N_GROUPS,
                            jnp.where(lane_i == 1, i2 - N_GROUPS,
                                      jnp.where(lane_i == 2, w1,
                                                jnp.where(lane_i == 3, w2, 0.0))))


def _out_projection(attn_o, rwkv_o, x2, w_a, w_r, mod3, g2, w_rt, b_rt, layer, bsz, seq):
    tm = min(512, seq)
    nt = seq // tm
    tok = x2.shape[0]
    row = lambda b, i: (b * nt + i, 0)
    const = lambda b, i: (0, 0)
    modspec = lambda col: pl.BlockSpec((1, 1, D_MODEL), lambda b, i: (layer * bsz + b, 0, col))
    return pl.pallas_call(
        _out_kernel,
        grid=(bsz, nt),
        in_specs=[pl.BlockSpec((tm, ATTN_WIDTH), row),
                  pl.BlockSpec((tm, RWKV_WIDTH), row),
                  pl.BlockSpec((tm, D_MODEL), row),
                  pl.BlockSpec((ATTN_WIDTH, D_MODEL), const),
                  pl.BlockSpec((RWKV_WIDTH, D_MODEL), const),
                  modspec(2),
                  pl.BlockSpec((1, D_MODEL), const),
                  modspec(4),
                  modspec(3),
                  pl.BlockSpec((2, D_MODEL, ROUTE_LANES), lambda b, i: (0, 0, 0)),
                  pl.BlockSpec((1, ROUTE_LANES), const)],
        out_specs=[pl.BlockSpec((tm, D_MODEL), row),
                   pl.BlockSpec((tm, D_MODEL), row),
                   pl.BlockSpec((tm, ROUTE_LANES), row)],
        out_shape=[jax.ShapeDtypeStruct((tok, D_MODEL), F32),
                   jax.ShapeDtypeStruct((tok, D_MODEL), F32),
                   jax.ShapeDtypeStruct((tok, ROUTE_LANES), F32)],
        compiler_params=_cparams(("arbitrary", "arbitrary")),
        name="out_proj_router",
    )(attn_o, rwkv_o, x2, w_a, w_r, mod3, g2, mod3, mod3, w_rt, b_rt)


def _dispatch_kernel(dest_ref, h_ref, xin_ref, xs_ref, sem):
    del xin_ref
    tt = h_ref.shape[0]

    def row_copy(t, kk):
        d = dest_ref[0, 0, TOP_K * t + kk]
        return pltpu.make_async_copy(h_ref.at[pl.ds(t, 1)], xs_ref.at[pl.ds(d, 1)], sem)

    def issue(t, c):
        for kk in range(TOP_K):
            row_copy(t, kk).start(priority=kk)
        return c

    lax.fori_loop(0, tt, issue, 0, unroll=DMA_UNROLL)
    for kk in range(TOP_K):
        pltpu.make_async_copy(h_ref, xs_ref.at[pl.ds(0, tt)], sem).wait()


def _dispatch(dest3, h2, xs_init):
    tok = h2.shape[0]
    nt, _, two_tt = dest3.shape
    tt = two_tt // TOP_K
    return pl.pallas_call(
        _dispatch_kernel,
        grid=(nt,),
        in_specs=[pl.BlockSpec((1, 1, two_tt), lambda i: (i, 0, 0), memory_space=pltpu.SMEM),
                  pl.BlockSpec((tt, D_MODEL), lambda i: (i, 0)),
                  pl.BlockSpec(memory_space=pl.ANY)],
        out_specs=pl.BlockSpec(memory_space=pl.ANY),
        out_shape=jax.ShapeDtypeStruct(xs_init.shape, F32),
        scratch_shapes=[pltpu.SemaphoreType.DMA],
        input_output_aliases={2: 0},
        compiler_params=_cparams(("arbitrary",)),
        name="moe_dispatch",
    )(dest3, h2, xs_init)


def _expert_kernel(be_ref, x_ref, wg_ref, wu_ref, wd_ref, y_ref, wg_lo, wu_lo, wd_lo):
    i = pl.program_id(0)

    @pl.when((i == 0) | (be_ref[i] != be_ref[jnp.maximum(i - 1, 0)]))
    def _():
        wg_lo[...] = wg_ref[0].astype(BF16)
        wu_lo[...] = wu_ref[0].astype(BF16)
        wd_lo[...] = wd_ref[0].astype(BF16)

    xb = x_ref[...].astype(BF16)
    gate = jnp.dot(xb, wg_lo[...], preferred_element_type=F32)
    up = jnp.dot(xb, wu_lo[...], preferred_element_type=F32)
    hid = gate * _sigmoid(gate) * up
    y_ref[...] = jnp.dot(hid.astype(BF16), wd_lo[...], preferred_element_type=F32)


def _experts(block_e, xs, w_gate, w_up, w_down, layer):
    rows = xs.shape[0]
    nb = rows // MOE_ROWS
    grid_spec = pltpu.PrefetchScalarGridSpec(
        num_scalar_prefetch=1,
        grid=(nb,),
        in_specs=[pl.BlockSpec((MOE_ROWS, D_MODEL), lambda i, be: (i, 0)),
                  pl.BlockSpec((1, D_MODEL, D_EXPERT), lambda i, be: (layer * N_EXPERTS + be[i], 0, 0)),
                  pl.BlockSpec((1, D_MODEL, D_EXPERT), lambda i, be: (layer * N_EXPERTS + be[i], 0, 0)),
                  pl.BlockSpec((1, D_EXPERT, D_MODEL), lambda i, be: (layer * N_EXPERTS + be[i], 0, 0))],
        out_specs=pl.BlockSpec((MOE_ROWS, D_MODEL), lambda i, be: (i, 0)),
        scratch_shapes=[pltpu.VMEM((D_MODEL, D_EXPERT), BF16),
                        pltpu.VMEM((D_MODEL, D_EXPERT), BF16),
                        pltpu.VMEM((D_EXPERT, D_MODEL), BF16)],
    )
    return pl.pallas_call(
        _expert_kernel,
        grid_spec=grid_spec,
        out_shape=jax.ShapeDtypeStruct((rows, D_MODEL), F32),
        compiler_params=_cparams(("arbitrary",)),
        name="moe_experts",
    )(block_e, xs, w_gate, w_up, w_down)


def _combine_kernel(dest_ref, yb_ref, rt_ref, x_ref, gt_ref, fg_ref, o_ref, buf_ref, sem, *,
                    final_norm):
    tt = x_ref.shape[0]

    def row_copy(t, kk):
        d = dest_ref[0, 0, TOP_K * t + kk]
        return pltpu.make_async_copy(yb_ref.at[pl.ds(d, 1)], buf_ref.at[kk, pl.ds(t, 1)], sem)

    def issue(t, c):
        for kk in range(TOP_K):
            row_copy(t, kk).start(priority=kk)
        return c

    lax.fori_loop(0, tt, issue, 0, unroll=DMA_UNROLL)
    for kk in range(TOP_K):
        pltpu.make_async_copy(yb_ref.at[pl.ds(0, tt)], buf_ref.at[kk], sem).wait()
    rt = rt_ref[...]
    y = buf_ref[0] * rt[:, 2:3] + buf_ref[1] * rt[:, 3:4]
    x = x_ref[...] + gt_ref[0] * y
    if final_norm:
        ms = jnp.mean(x * x, axis=-1, keepdims=True)
        x = x * lax.rsqrt(ms + NORM_EPS) * fg_ref[...]
    o_ref[...] = x


def _combine(dest3, yb, rt, x1, mod3, final_g, layer, bsz, seq, final_norm):
    tok = x1.shape[0]
    nt, _, two_tt = dest3.shape
    tt = two_tt // TOP_K
    per_seq = seq // tt
    return pl.pallas_call(
        functools.partial(_combine_kernel, final_norm=final_norm),
        grid=(nt,),
        in_specs=[pl.BlockSpec((1, 1, two_tt), lambda i: (i, 0, 0), memory_space=pltpu.SMEM),
                  pl.BlockSpec(memory_space=pl.ANY),
                  pl.BlockSpec((tt, ROUTE_LANES), lambda i: (i, 0)),
                  pl.BlockSpec((tt, D_MODEL), lambda i: (i, 0)),
                  pl.BlockSpec((1, 1, D_MODEL), lambda i: (layer * bsz + i // per_seq, 0, 5)),
                  pl.BlockSpec((1, D_MODEL), lambda i: (0, 0))],
        out_specs=pl.BlockSpec((tt, D_MODEL), lambda i: (i, 0)),
        out_shape=jax.ShapeDtypeStruct((tok, D_MODEL), F32),
        scratch_shapes=[pltpu.VMEM((TOP_K, tt, D_MODEL), F32), pltpu.SemaphoreType.DMA],
        compiler_params=_cparams(("arbitrary",)),
        name="moe_combine",
    )(dest3, yb, rt, x1, mod3, final_g)


def _routing_tables(rt, tok):
    m = tok * TOP_K
    flat_e = rt[:, :TOP_K].astype(jnp.int32).reshape(m)
    onehot = (flat_e[:, None] == jnp.arange(N_EXPERTS, dtype=jnp.int32)[None, :]).astype(jnp.int32)
    ct = min(COUNT_TILE, m)
    tri = (jnp.arange(ct)[None, :] <= jnp.arange(ct)[:, None]).astype(BF16)
    within = jnp.einsum("ts,nse->nte", tri, onehot.reshape(m // ct, ct, N_EXPERTS).astype(BF16),
                        preferred_element_type=F32)
    tile_tot = within[:, -1, :]
    tile_base = jnp.cumsum(tile_tot, axis=0) - tile_tot
    running = (within + tile_base[:, None, :]).reshape(m, N_EXPERTS).astype(jnp.int32)
    counts = running[-1]
    padded = (counts + MOE_ROWS - 1) // MOE_ROWS * MOE_ROWS
    pad_end = jnp.cumsum(padded)
    pad_start = pad_end - padded
    dest = jnp.sum(onehot * (running - 1 + pad_start[None, :]), axis=1)
    rows = -(-m // MOE_ROWS) * MOE_ROWS + N_EXPERTS * MOE_ROWS
    nb = rows // MOE_ROWS
    block_start = jnp.arange(nb, dtype=jnp.int32) * MOE_ROWS
    block_e = jnp.minimum(jnp.sum((pad_end[None, :] <= block_start[:, None]).astype(jnp.int32), axis=1),
                          N_EXPERTS - 1).astype(jnp.int32)
    return dest.astype(jnp.int32), block_e, rows


def _rotary_tables(positions):
    half = ROT_DIM // 2
    inv_freq = ROPE_THETA ** (-jnp.arange(0, ROT_DIM, 2, dtype=F32) / ROT_DIM)
    ang = positions.astype(F32).reshape(-1, 1) * inv_freq[None, :]
    cos, sin = jnp.cos(ang), jnp.sin(ang)
    dim = jnp.arange(128)[None, :] % HEAD_DIM
    freq = jnp.arange(half)[:, None]
    lo = (dim == freq).astype(F32)
    hi = (dim == freq + half).astype(F32)
    rest = (dim >= ROT_DIM).astype(F32)
    place = functools.partial(jnp.dot, precision=lax.Precision.HIGHEST)
    return place(cos, lo + hi) + rest, place(sin, hi), place(sin, -lo)


def _lambda_init(layer):
    return 0.8 - 0.6 * math.exp(-0.3 * layer)


def kernel(x, c, positions, ada_w, ada_b, norm1_g, norm2_g, w_in, w_out, attn_lambda, attn_subln_g, rwkv_shift_mu, rwkv_w0, rwkv_w_up, rwkv_a0, rwkv_a_up, rwkv_g_up, rwkv_k_k, rwkv_k_a, rwkv_r_k, rwkv_lnx_g, rwkv_lnx_b, moe_w_group, moe_b_group, moe_w_router, moe_b_router, moe_w_gate, moe_w_up, moe_w_down, final_g):
    bsz, seq, d = x.shape
    depth = ada_w.shape[0]
    tok = bsz * seq
    x2 = x.reshape(tok, d)

    mod = _modulation(c, ada_w, ada_b)
    mod3 = mod.reshape(depth * bsz, 1, 6 * d)
    cos_t, s1_t, s2_t = _rotary_tables(positions)
    masks = _block_masks()
    tt = min(512, seq)

    for layer in range(depth):
        pad = LORA_PAD - LORA
        w_pad = jnp.pad(w_in[layer], ((0, 0), (0, pad))).astype(BF16)
        mu_pad = jnp.pad(rwkv_shift_mu[layer], (0, pad)).reshape(1, RWKV_COLS)
        qa, qb, k, v, p = _in_projection(x2, norm1_g[layer].reshape(1, d), mod3, layer, bsz, seq,
                                         w_pad, mu_pad, cos_t, s1_t, s2_t)
        attn_o = _diff_attention(qa, qb, k, v, attn_lambda[layer],
                                 attn_subln_g[layer].reshape(1, 2 * HEAD_DIM),
                                 _lambda_init(layer), bsz, seq)

        zrow = jnp.zeros((RWKV_WIDTH,), F32)
        prm = jnp.stack([rwkv_w0[layer], rwkv_a0[layer], rwkv_k_k[layer], rwkv_k_a[layer],
                         rwkv_r_k[layer].reshape(RWKV_WIDTH), rwkv_lnx_g[layer],
                         rwkv_lnx_b[layer], zrow])
        up_w = jnp.zeros((LORA_PAD, 3 * RWKV_WIDTH), F32)
        up_w = up_w.at[:W_LORA, :RWKV_WIDTH].set(rwkv_w_up[layer])
        up_w = up_w.at[W_LORA:W_LORA + A_LORA, RWKV_WIDTH:2 * RWKV_WIDTH].set(rwkv_a_up[layer])
        up_w = up_w.at[W_LORA + A_LORA:LORA, 2 * RWKV_WIDTH:].set(rwkv_g_up[layer])
        rwkv_o = _rwkv_mix(p, prm, up_w.astype(BF16), masks, bsz, seq)

        w_o = w_out[layer].astype(BF16)
        w_rt = jnp.zeros((d, ROUTE_LANES), F32)
        w_rt = w_rt.at[:, :N_GROUPS].set(moe_w_group[layer])
        w_rt = w_rt.at[:, N_GROUPS:N_GROUPS + N_EXPERTS].set(moe_w_router[layer])
        w_rt_hi = w_rt.astype(BF16)
        w_rt = jnp.stack([w_rt_hi, (w_rt - w_rt_hi.astype(F32)).astype(BF16)])
        b_rt = jnp.zeros((1, ROUTE_LANES), F32)
        b_rt = b_rt.at[0, :N_GROUPS].set(moe_b_group[layer])
        b_rt = b_rt.at[0, N_GROUPS:N_GROUPS + N_EXPERTS].set(moe_b_router[layer])
        x1, h2, rt = _out_projection(attn_o, rwkv_o, x2, w_o[:ATTN_WIDTH], w_o[ATTN_WIDTH:], mod3,
                                     norm2_g[layer].reshape(1, d), w_rt, b_rt, layer, bsz, seq)

        dest, block_e, rows = _routing_tables(rt, tok)
        dest3 = dest.reshape(tok // tt, 1, TOP_K * tt)
        xs = _dispatch(dest3, h2, jnp.zeros((rows, d), F32))
        yb = _experts(block_e, xs, moe_w_gate.reshape(-1, d, D_EXPERT),
                      moe_w_up.reshape(-1, d, D_EXPERT), moe_w_down.reshape(-1, D_EXPERT, d), layer)
        x2 = _combine(dest3, yb, rt, x1, mod3, final_g.reshape(1, d), layer, bsz, seq,
                      final_norm=(layer == depth - 1))

    return x2.reshape(bsz, seq, d)
```

```python
import functools
import math

import jax
import jax.numpy as jnp
from jax import lax
from jax.experimental import pallas as pl
from jax.experimental.pallas import tpu as pltpu

F32 = jnp.float32
BF16 = jnp.bfloat16

D_MODEL = 1024
ATTN_WIDTH = 512
ATTN_HEADS = 4
HEAD_DIM = 64
ROT_DIM = 16
ROPE_THETA = 500000.0
SUBLN_EPS = 1e-5
ATTN_CHUNK = 64

RWKV_WIDTH = 512
RWKV_HEAD = 64
W_LORA, A_LORA, G_LORA = 32, 32, 96
LORA = W_LORA + A_LORA + G_LORA
LORA_PAD = 256
GN_EPS = 64e-5
RWKV_COLS = 3 * RWKV_WIDTH + LORA_PAD
QKV_COLS = 3 * ATTN_WIDTH
WKV_CHUNK = 64
WKV_STEP = 256
SLAB = 256

N_GROUPS = 4
EXPERTS_PER_GROUP = 8
N_EXPERTS = 32
D_EXPERT = 256
TOP_K = 2
ROUTE_LANES = 128
MOE_ROWS = 512
DMA_UNROLL = 8
COUNT_TILE = 256

NORM_EPS = 1e-6
NEG_BIG = -1e30
VMEM_LIMIT = 56 * 1024 * 1024


def _cparams(sem):
    return pltpu.CompilerParams(dimension_semantics=sem, vmem_limit_bytes=VMEM_LIMIT)


def _mm(a, b):
    return jnp.dot(a.astype(BF16), b.astype(BF16), preferred_element_type=F32)


def _mm_nt(a, b):
    return lax.dot_general(a.astype(BF16), b.astype(BF16), (((1,), (1,)), ((), ())),
                           preferred_element_type=F32)


def _mm_tn(a, b):
    return lax.dot_general(a.astype(BF16), b.astype(BF16), (((0,), (0,)), ((), ())),
                           preferred_element_type=F32)


def _mm_split(m_exact, x):
    hi = x.astype(BF16)
    lo = (x - hi.astype(F32)).astype(BF16)
    return (jnp.dot(m_exact, hi, preferred_element_type=F32)
            + jnp.dot(m_exact, lo, preferred_element_type=F32))


def _split_mm(x, m_exact):
    hi = x.astype(BF16)
    lo = (x - hi.astype(F32)).astype(BF16)
    return (jnp.dot(hi, m_exact, preferred_element_type=F32)
            + jnp.dot(lo, m_exact, preferred_element_type=F32))


def _sigmoid(x):
    return 1.0 / (1.0 + jnp.exp(-x))


def _mod_kernel(c_ref, w_ref, b_ref, o_ref):
    c = c_ref[...]
    ca = c * _sigmoid(c)
    o_ref[0] = jnp.dot(ca, w_ref[0], preferred_element_type=F32,
                       precision=lax.Precision.HIGHEST) + b_ref[0]


def _modulation(c, ada_w, ada_b):
    depth, d, n = ada_w.shape
    bsz = c.shape[0]
    nb = n // d
    return pl.pallas_call(
        _mod_kernel,
        grid=(depth, nb),
        in_specs=[pl.BlockSpec((bsz, d), lambda l, j: (0, 0)),
                  pl.BlockSpec((1, d, d), lambda l, j: (l, 0, j)),
                  pl.BlockSpec((1, 1, d), lambda l, j: (l, 0, j))],
        out_specs=pl.BlockSpec((1, bsz, d), lambda l, j: (l, 0, j)),
        out_shape=jax.ShapeDtypeStruct((depth, bsz, n), F32),
        compiler_params=_cparams(("arbitrary", "arbitrary")),
        name="adaln_mod",
    )(c, ada_w, ada_b.reshape(depth, 1, n))


def _inproj_kernel(x_ref, g_ref, sc_ref, sh_ref, w_ref, mu_ref, cos_ref, s1_ref, s2_ref,
                   qa_ref, qb_ref, k_ref, vt_ref, p_ref, carry_ref):
    i = pl.program_id(1)
    tm = x_ref.shape[0]
    x = x_ref[...]
    ms = jnp.mean(x * x, axis=-1, keepdims=True)
    h = x * lax.rsqrt(ms + NORM_EPS) * g_ref[...]
    h = h * (1.0 + sc_ref[0]) + sh_ref[0]
    hb = h.astype(BF16)

    cosv, s1v, s2v = cos_ref[...], s1_ref[...], s2_ref[...]

    def rot128(t):
        return (t * cosv + pltpu.roll(t, ROT_DIM // 2, 1) * s1v
                + pltpu.roll(t, 128 - ROT_DIM // 2, 1) * s2v)

    lane = lax.broadcasted_iota(jnp.int32, (1, 128), 1)
    first_map = lane < HEAD_DIM
    scale = HEAD_DIM ** -0.5 * LOG2E
    qkv = jnp.dot(hb, w_ref[:, :QKV_COLS], preferred_element_type=F32)
    for hd in range(ATTN_HEADS):
        lo, hi = hd * 128, (hd + 1) * 128
        q = rot128(qkv[:, lo:hi]) * scale
        qa_ref[:, lo:hi] = jnp.where(first_map, q, 0.0).astype(BF16)
        qb_ref[:, lo:hi] = jnp.where(first_map, 0.0, q).astype(BF16)
        k_ref[:, lo:hi] = rot128(qkv[:, ATTN_WIDTH + lo:ATTN_WIDTH + hi]).astype(BF16)
    vt_ref[...] = qkv[:, 2 * ATTN_WIDTH:].T.astype(BF16)

    @pl.when(i == 0)
    def _():
        carry_ref[...] = jnp.zeros_like(carry_ref)

    p = jnp.dot(hb, w_ref[:, QKV_COLS:], preferred_element_type=F32)
    row = lax.broadcasted_iota(jnp.int32, p.shape, 0)
    prev = jnp.where(row == 0, carry_ref[...], pltpu.roll(p, 1, 0))
    carry_ref[...] = p[tm - 1:tm, :]
    p_ref[...] = p + (prev - p) * mu_ref[...]


def _in_projection(x2, g1, mod3, layer, bsz, seq, w_pad, mu_pad, cos_t, s1_t, s2_t):
    tm = min(512, seq)
    nt = seq // tm
    tok = x2.shape[0]
    row = lambda b, i: (b * nt + i, 0)
    const = lambda b, i: (0, 0)
    wide = w_pad.shape[1]
    outs = pl.pallas_call(
        _inproj_kernel,
        grid=(bsz, nt),
        in_specs=[pl.BlockSpec((tm, D_MODEL), row),
                  pl.BlockSpec((1, D_MODEL), const),
                  pl.BlockSpec((1, 1, D_MODEL), lambda b, i: (layer * bsz + b, 0, 1)),
                  pl.BlockSpec((1, 1, D_MODEL), lambda b, i: (layer * bsz + b, 0, 0)),
                  pl.BlockSpec((D_MODEL, wide), const),
                  pl.BlockSpec((1, RWKV_COLS), const),
                  pl.BlockSpec((tm, 128), row),
                  pl.BlockSpec((tm, 128), row),
                  pl.BlockSpec((tm, 128), row)],
        out_specs=[pl.BlockSpec((tm, ATTN_WIDTH), row)] * 3
        + [pl.BlockSpec((ATTN_WIDTH, tm), lambda b, i: (b, i)), pl.BlockSpec((tm, RWKV_COLS), row)],
        out_shape=[jax.ShapeDtypeStruct((tok, ATTN_WIDTH), BF16)] * 3
        + [jax.ShapeDtypeStruct((bsz * ATTN_WIDTH, seq), BF16),
           jax.ShapeDtypeStruct((tok, RWKV_COLS), F32)],
        scratch_shapes=[pltpu.VMEM((1, RWKV_COLS), F32)],
        compiler_params=_cparams(("arbitrary", "arbitrary")),
        name="in_proj",
    )(x2, g1, mod3, mod3, w_pad, mu_pad, cos_t, s1_t, s2_t)
    return outs


ATTN_TILE = 1024
ATTN_QLANES = 256
ATTN_KEYS = 512
ATTN_AHEAD = 4
LOG2E = 1.4426950408889634


def _attn_kernel(qi_ref, kj_ref, qa_ref, qb_ref, k_ref, vt_ref, lam_ref, g_ref, o_ref,
                 m_ref, l_ref, acc_ref, *, lambda_init):
    step = pl.program_id(1)
    i = qi_ref[step]
    j = kj_ref[step]
    tq = qa_ref.shape[0]
    tk = k_ref.shape[0]
    hq = min(ATTN_QLANES, tq)
    n_half = tq // hq

    @pl.when(j == 0)
    def _():
        m_ref[...] = jnp.full(m_ref.shape, NEG_BIG, F32)
        l_ref[...] = jnp.zeros_like(l_ref)
        acc_ref[...] = jnp.zeros_like(acc_ref)

    kb = min(ATTN_KEYS, tk)

    def accumulate(diagonal):
        def visible(half, k0, kn):
            kc = (lax.broadcasted_iota(jnp.int32, (kn, hq), 0) + k0) // ATTN_CHUNK
            qc = (lax.broadcasted_iota(jnp.int32, (kn, hq), 1) + half * hq) // ATTN_CHUNK
            return kc <= qc

        spans = []
        for half in range(n_half):
            if not diagonal:
                spans += [(half, k0, kb, False) for k0 in range(0, tk, kb)]
            else:
                if half > 0:
                    spans.append((half, 0, half * hq, False))
                spans.append((half, half * hq, hq, True))
        chains = [(half, k0, kn, masked, hd, mp) for (half, k0, kn, masked) in spans
                  for hd in range(ATTN_HEADS) for mp in range(2)]

        def scores(chain):
            half, k0, kn, _, hd, mp = chain
            q_ref = (qa_ref, qb_ref)[mp]
            return lax.dot_general(k_ref[k0:k0 + kn, hd * 128:(hd + 1) * 128],
                                   q_ref[half * hq:(half + 1) * hq, hd * 128:(hd + 1) * 128],
                                   (((1,), (1,)), ((), ())), preferred_element_type=F32)

        ahead = [scores(c) for c in chains[:ATTN_AHEAD]]
        for n, (half, k0, kn, masked, hd, mp) in enumerate(chains):
            s = ahead.pop(0)
            if n + ATTN_AHEAD < len(chains):
                ahead.append(scores(chains[n + ATTN_AHEAD]))
            idx = (2 * hd + mp) * n_half + half
            if masked:
                s = jnp.where(visible(half, k0, kn), s, NEG_BIG)
            m_old = m_ref[idx]
            m_new = jnp.maximum(m_old, jnp.max(s, axis=0, keepdims=True))
            alpha = jnp.exp2(m_old - m_new)
            p = jnp.exp2(s - m_new)
            l_ref[idx] = alpha * l_ref[idx] + jnp.sum(p, axis=0, keepdims=True)
            acc_ref[idx] = alpha * acc_ref[idx] + jnp.dot(
                vt_ref[hd * 128:(hd + 1) * 128, k0:k0 + kn], p.astype(BF16),
                preferred_element_type=F32)
            m_ref[idx] = m_new

    @pl.when(j < i)
    def _():
        accumulate(False)

    @pl.when(j == i)
    def _():
        accumulate(True)
        lv = lam_ref[...]
        lam = (jnp.exp(jnp.sum(lv[0:1] * lv[1:2], axis=-1, keepdims=True))
               - jnp.exp(jnp.sum(lv[2:3] * lv[3:4], axis=-1, keepdims=True)) + lambda_init)
        gcol = jnp.concatenate([g_ref[...]] * (hq // 128), axis=1)
        for hd in range(ATTN_HEADS):
            for half in range(n_half):
                a = (2 * hd) * n_half + half
                b = (2 * hd + 1) * n_half + half
                o = acc_ref[a] / l_ref[a] - lam * (acc_ref[b] / l_ref[b])
                ms = jnp.mean(o * o, axis=0, keepdims=True)
                o = o * lax.rsqrt(ms + SUBLN_EPS) * gcol * (1.0 - lambda_init)
                o_ref[half * hq:(half + 1) * hq, hd * 128:(hd + 1) * 128] = o.T.astype(BF16)


def _diff_attention(qa, qb, k, vt, lam_vecs, subln_g, lambda_init, bsz, seq):
    tq = min(ATTN_TILE, seq)
    nq = seq // tq
    tok = qa.shape[0]
    hq = min(ATTN_QLANES, tq)
    n_stat = 2 * ATTN_HEADS * (tq // hq)
    pairs = [(i, j) for i in range(nq) for j in range(i + 1)]
    qi = jnp.asarray([p[0] for p in pairs], jnp.int32)
    kj = jnp.asarray([p[1] for p in pairs], jnp.int32)
    qmap = lambda b, s, qi, kj: (b * nq + qi[s], 0)
    kmap = lambda b, s, qi, kj: (b * nq + kj[s], 0)
    vmap = lambda b, s, qi, kj: (b, kj[s])
    const = lambda b, s, qi, kj: (0, 0)
    g_col = jnp.broadcast_to(subln_g.reshape(2 * HEAD_DIM, 1), (2 * HEAD_DIM, 128))
    grid_spec = pltpu.PrefetchScalarGridSpec(
        num_scalar_prefetch=2,
        grid=(bsz, len(pairs)),
        in_specs=[pl.BlockSpec((tq, ATTN_WIDTH), qmap),
                  pl.BlockSpec((tq, ATTN_WIDTH), qmap),
                  pl.BlockSpec((tq, ATTN_WIDTH), kmap),
                  pl.BlockSpec((ATTN_WIDTH, tq), vmap),
                  pl.BlockSpec((4, HEAD_DIM), const),
                  pl.BlockSpec((2 * HEAD_DIM, 128), const)],
        out_specs=pl.BlockSpec((tq, ATTN_WIDTH), qmap),
        scratch_shapes=[pltpu.VMEM((n_stat, 1, hq), F32),
                        pltpu.VMEM((n_stat, 1, hq), F32),
                        pltpu.VMEM((n_stat, 2 * HEAD_DIM, hq), F32)],
    )
    return pl.pallas_call(
        functools.partial(_attn_kernel, lambda_init=lambda_init),
        grid_spec=grid_spec,
        out_shape=jax.ShapeDtypeStruct((tok, ATTN_WIDTH), BF16),
        compiler_params=_cparams(("arbitrary", "arbitrary")),
        name="diff_attn",
    )(qi, kj, qa, qb, k, vt, lam_vecs, g_col)


def _softplus(z):
    return jnp.maximum(z, 0.0) + jnp.log(1.0 + jnp.exp(-jnp.abs(z)))


def _rwkv_kernel(p_ref, prm_ref, up_ref, tri_ref, bd_ref, sl_ref, il_ref, eye_ref,
                 o_ref, h_ref):
    t = pl.program_id(1)
    C = WKV_CHUNK
    n_chunks = p_ref.shape[0] // C
    W = RWKV_WIDTH

    @pl.when(t == 0)
    def _():
        h_ref[...] = jnp.zeros_like(h_ref)

    p = p_ref[...]
    r, k, v, lo = p[:, :W], p[:, W:2 * W], p[:, 2 * W:3 * W], p[:, 3 * W:]
    lane = lax.broadcasted_iota(jnp.int32, lo.shape, 1)
    act = jnp.where(lane < W_LORA, jnp.tanh(lo), jnp.where(lane < W_LORA + A_LORA, lo, _sigmoid(lo)))
    up = _mm(act, up_ref[...])
    prm = prm_ref[...]
    w0, a0, k_k, k_a, r_k, ln_g, ln_b = (prm[n:n + 1] for n in range(7))
    w_raw = -_softplus(-(w0 + up[:, :W])) - 0.5
    logd = -jnp.exp(w_raw)
    a_sig = _sigmoid(a0 + up[:, W:2 * W])
    gate = up[:, 2 * W:]

    bd = bd_ref[...]
    bd_lo = bd.astype(BF16)

    def head_sum(z):
        return jnp.concatenate([_mm(z[:, g * SLAB:(g + 1) * SLAB], bd_lo)
                                for g in range(W // SLAB)], axis=1)

    kkr = k * k_k
    kk = kkr / jnp.maximum(jnp.sqrt(head_sum(kkr * kkr)), 1e-12)
    kf = k * (1.0 + (a_sig - 1.0) * k_a)
    bonus = head_sum(r * kf * r_k) * v
    a_vec = -kk
    b_vec = kk * a_sig

    cum = _mm_split(tri_ref[...], logd)
    cum_end = jnp.concatenate(
        [jnp.broadcast_to(cum[(ch + 1) * C - 1:(ch + 1) * C, :], (C, W)) for ch in range(n_chunks)],
        axis=0)
    d_in = jnp.exp(cum)
    d_ex = jnp.exp(cum - logd)
    d_inv = jnp.exp(-cum)
    d_end = jnp.exp(cum_end)
    d_rest = jnp.exp(cum_end - cum)

    at, rt = a_vec * d_ex, r * d_in
    bt, kt = b_vec * d_inv, kf * d_inv
    be, ke = b_vec * d_rest, kf * d_rest

    sl = sl_ref[...]
    il = il_ref[...]
    eye_c = il - sl
    eye_s = eye_ref[...]
    heads = SLAB // RWKV_HEAD
    n_slabs = W // SLAB
    probs = [(ch, g) for ch in range(n_chunks) for g in range(n_slabs)]

    def cut(z, ch, g):
        return z[ch * C:(ch + 1) * C, g * SLAB:(g + 1) * SLAB]

    def stack(zz):
        return jnp.concatenate([zz.astype(BF16)] * heads, axis=0) * bd_lo

    ats = [cut(at, *pr) for pr in probs]
    rts = [cut(rt, *pr) for pr in probs]
    vs = [cut(v, *pr) for pr in probs]
    btm = [stack(cut(bt, *pr)) for pr in probs]
    ktm = [stack(cut(kt, *pr)) for pr in probs]
    vm = [stack(v_) for v_ in vs]
    l_ab = [sl * _mm_nt(a_, b_) for a_, b_ in zip(ats, btm)]
    l_ak = [sl * _mm_nt(a_, k_) for a_, k_ in zip(ats, ktm)]
    lpm = [stack(l_) for l_ in l_ab]
    lp = [_mm(l_, m_) for l_, m_ in zip(l_ab, lpm)]
    inv = [eye_c + l_ for l_ in l_ab]
    wv = [_mm(l_, v_) for l_, v_ in zip(l_ak, vm)]
    m_rb = [il * _mm_nt(r_, b_) for r_, b_ in zip(rts, btm)]
    m_rk = [il * _mm_nt(r_, k_) for r_, k_ in zip(rts, ktm)]
    levels = int(math.log2(C)) - 1
    for lev in range(levels):
        lpm = [stack(l_) for l_ in lp]
        inv_new = [i_ + _mm(i_, m_) for i_, m_ in zip(inv, lpm)]
        if lev + 1 < levels:
            lp = [_mm(l_, m_) for l_, m_ in zip(lp, lpm)]
        inv = inv_new
    abar = [_mm(i_, stack(a_)) for i_, a_ in zip(inv, ats)]
    vbar = [_mm(i_, stack(w_)) for i_, w_ in zip(inv, wv)]
    abm = [stack(a_) for a_ in abar]
    vbm = [stack(v_) for v_ in vbar]
    rhat = [r_ + _mm(m_, a_) for r_, m_, a_ in zip(rts, m_rb, abm)]
    yhat = [_mm(mb_, vb_) + _mm(mk_, v_) for mb_, vb_, mk_, v_ in zip(m_rb, vbm, m_rk, vm)]
    bes = [cut(be, *pr) for pr in probs]
    kes = [cut(ke, *pr) for pr in probs]
    gmat = [eye_s * cut(d_end, *pr)[0:1] + bd * _mm_tn(b_, a_) for pr, b_, a_ in zip(probs, bes, abar)]
    fmat = [bd * _mm_tn(jnp.concatenate([b_, k_], axis=0), jnp.concatenate([vb_, v_], axis=0))
            for b_, k_, vb_, v_ in zip(bes, kes, vbar, vs)]

    hs = [h_ref[g] for g in range(n_slabs)]
    rows = []
    for ch in range(n_chunks):
        ys = []
        for g in range(n_slabs):
            n = ch * n_slabs + g
            ys.append(_mm(rhat[n], hs[g]) + yhat[n])
            hs[g] = _mm(gmat[n], hs[g]) + fmat[n]
        rows.append(jnp.concatenate(ys, axis=1))
    for g in range(n_slabs):
        h_ref[g] = hs[g]
    y = jnp.concatenate(rows, axis=0) if n_chunks > 1 else rows[0]

    inv_n = 1.0 / RWKV_HEAD
    mu = head_sum(y) * inv_n
    yc = y - mu
    var = head_sum(yc * yc) * inv_n
    yn = yc * lax.rsqrt(var + GN_EPS) * ln_g + ln_b
    o_ref[...] = ((yn + bonus) * gate).astype(BF16)


def _block_masks():
    n = SLAB
    r = jnp.arange(n)[:, None]
    c = jnp.arange(n)[None, :]
    same = (r // RWKV_HEAD) == (c // RWKV_HEAD)
    bd = same.astype(F32)
    eye = (r == c).astype(F32)
    t = jnp.arange(WKV_CHUNK)[:, None]
    s_in_head = c % RWKV_HEAD
    sl = (s_in_head < t).astype(F32)
    il = (s_in_head <= t).astype(F32)
    tt = jnp.arange(WKV_STEP)
    tri = ((tt[None, :] <= tt[:, None])
           & (tt[None, :] // WKV_CHUNK == tt[:, None] // WKV_CHUNK)).astype(BF16)
    return tri, bd, sl, il, eye


def _rwkv_mix(p, prm, up_w, masks, bsz, seq):
    C = WKV_STEP
    nc = seq // C
    tok = p.shape[0]
    tri, bd, sl, il, eye = masks
    row = lambda b, t: (b * nc + t, 0)
    const = lambda b, t: (0, 0)
    return pl.pallas_call(
        _rwkv_kernel,
        grid=(bsz, nc),
        in_specs=[pl.BlockSpec((C, RWKV_COLS), row),
                  pl.BlockSpec((8, RWKV_WIDTH), const),
                  pl.BlockSpec((LORA_PAD, 3 * RWKV_WIDTH), const),
                  pl.BlockSpec((C, C), const),
                  pl.BlockSpec((SLAB, SLAB), const),
                  pl.BlockSpec((WKV_CHUNK, SLAB), const),
                  pl.BlockSpec((WKV_CHUNK, SLAB), const),
                  pl.BlockSpec((SLAB, SLAB), const)],
        out_specs=pl.BlockSpec((C, RWKV_WIDTH), row),
        out_shape=jax.ShapeDtypeStruct((tok, RWKV_WIDTH), BF16),
        scratch_shapes=[pltpu.VMEM((RWKV_WIDTH // SLAB, SLAB, SLAB), F32)],
        compiler_params=_cparams(("arbitrary", "arbitrary")),
        name="rwkv7_mix",
    )(p, prm, up_w, tri, bd, sl, il, eye)


def _out_kernel(a_ref, r_ref, x_ref, wa_ref, wr_ref, gt_ref, g2_ref, sc_ref, sh_ref, wrt_ref,
                brt_ref, x1_ref, h2_ref, rt_ref):
    mixed = (jnp.dot(a_ref[...], wa_ref[...], preferred_element_type=F32)
             + jnp.dot(r_ref[...], wr_ref[...], preferred_element_type=F32))
    x1 = x_ref[...] + gt_ref[0] * mixed
    x1_ref[...] = x1
    ms = jnp.mean(x1 * x1, axis=-1, keepdims=True)
    h2 = x1 * lax.rsqrt(ms + NORM_EPS) * g2_ref[...]
    h2 = h2 * (1.0 + sc_ref[0]) + sh_ref[0]
    h2_ref[...] = h2
    h_hi = h2.astype(BF16)
    h_lo = (h2 - h_hi.astype(F32)).astype(BF16)
    logits = (jnp.dot(h_hi, wrt_ref[0], preferred_element_type=F32)
              + jnp.dot(h_lo, wrt_ref[0], preferred_element_type=F32)
              + jnp.dot(h_hi, wrt_ref[1], preferred_element_type=F32)) + brt_ref[...]

    lane_i = lax.broadcasted_iota(jnp.int32, logits.shape, 1)
    lane = lane_i.astype(F32)
    far = float(ROUTE_LANES)
    is_g = lane_i < N_GROUPS
    gmax = jnp.max(jnp.where(is_g, logits, NEG_BIG), axis=-1, keepdims=True)
    gsum = jnp.sum(jnp.where(is_g, jnp.exp(logits - gmax), 0.0), axis=-1, keepdims=True)
    g_top = 1.0 / gsum
    g_idx = jnp.min(jnp.where(is_g & (logits == gmax), lane, far), axis=-1, keepdims=True)
    base = N_GROUPS + g_idx * EXPERTS_PER_GROUP
    in_g = (lane >= base) & (lane < base + EXPERTS_PER_GROUP)
    v1 = jnp.max(jnp.where(in_g, logits, NEG_BIG), axis=-1, keepdims=True)
    i1 = jnp.min(jnp.where(in_g & (logits == v1), lane, far), axis=-1, keepdims=True)
    rest = in_g & (lane != i1)
    v2 = jnp.max(jnp.where(rest, logits, NEG_BIG), axis=-1, keepdims=True)
    i2 = jnp.min(jnp.where(rest & (logits == v2), lane, far), axis=-1, keepdims=True)
    e21 = jnp.exp(v2 - v1)
    w1 = g_top / (1.0 + e21)
    w2 = g_top * e21 / (1.0 + e21)
    rt_ref[...] = jnp.where(lane_i == 0, i1 - N_GROUPS,
                            jnp.where(lane_i == 1, i2 - N_GROUPS,
                                      jnp.where(lane_i == 2, w1,
                                                jnp.where(lane_i == 3, w2, 0.0))))


def _out_projection(attn_o, rwkv_o, x2, w_a, w_r, mod3, g2, w_rt, b_rt, layer, bsz, seq):
    tm = min(512, seq)
    nt = seq // tm
    tok = x2.shape[0]
    row = lambda b, i: (b * nt + i, 0)
    const = lambda b, i: (0, 0)
    modspec = lambda col: pl.BlockSpec((1, 1, D_MODEL), lambda b, i: (layer * bsz + b, 0, col))
    return pl.pallas_call(
        _out_kernel,
        grid=(bsz, nt),
        in_specs=[pl.BlockSpec((tm, ATTN_WIDTH), row),
                  pl.BlockSpec((tm, RWKV_WIDTH), row),
                  pl.BlockSpec((tm, D_MODEL), row),
                  pl.BlockSpec((ATTN_WIDTH, D_MODEL), const),
                  pl.BlockSpec((RWKV_WIDTH, D_MODEL), const),
                  modspec(2),
                  pl.BlockSpec((1, D_MODEL), const),
                  modspec(4),
                  modspec(3),
                  pl.BlockSpec((2, D_MODEL, ROUTE_LANES), lambda b, i: (0, 0, 0)),
                  pl.BlockSpec((1, ROUTE_LANES), const)],
        out_specs=[pl.BlockSpec((tm, D_MODEL), row),
                   pl.BlockSpec((tm, D_MODEL), row),
                   pl.BlockSpec((tm, ROUTE_LANES), row)],
        out_shape=[jax.ShapeDtypeStruct((tok, D_MODEL), F32),
                   jax.ShapeDtypeStruct((tok, D_MODEL), F32),
                   jax.ShapeDtypeStruct((tok, ROUTE_LANES), F32)],
        compiler_params=_cparams(("arbitrary", "arbitrary")),
        name="out_proj_router",
    )(attn_o, rwkv_o, x2, w_a, w_r, mod3, g2, mod3, mod3, w_rt, b_rt)


def _dispatch_kernel(dest_ref, h_ref, xin_ref, xs_ref, sem):
    del xin_ref
    tt = h_ref.shape[0]

    def row_copy(t, kk):
        d = dest_ref[0, 0, TOP_K * t + kk]
        return pltpu.make_async_copy(h_ref.at[pl.ds(t, 1)], xs_ref.at[pl.ds(d, 1)], sem)

    def issue(t, c):
        for kk in range(TOP_K):
            row_copy(t, kk).start(priority=kk)
        return c

    lax.fori_loop(0, tt, issue, 0, unroll=DMA_UNROLL)
    for kk in range(TOP_K):
        pltpu.make_async_copy(h_ref, xs_ref.at[pl.ds(0, tt)], sem).wait()


def _dispatch(dest3, h2, xs_init):
    tok = h2.shape[0]
    nt, _, two_tt = dest3.shape
    tt = two_tt // TOP_K
    return pl.pallas_call(
        _dispatch_kernel,
        grid=(nt,),
        in_specs=[pl.BlockSpec((1, 1, two_tt), lambda i: (i, 0, 0), memory_space=pltpu.SMEM),
                  pl.BlockSpec((tt, D_MODEL), lambda i: (i, 0)),
                  pl.BlockSpec(memory_space=pl.ANY)],
        out_specs=pl.BlockSpec(memory_space=pl.ANY),
        out_shape=jax.ShapeDtypeStruct(xs_init.shape, F32),
        scratch_shapes=[pltpu.SemaphoreType.DMA],
        input_output_aliases={2: 0},
        compiler_params=_cparams(("arbitrary",)),
        name="moe_dispatch",
    )(dest3, h2, xs_init)


def _expert_kernel(be_ref, x_ref, wg_ref, wu_ref, wd_ref, y_ref, wg_lo, wu_lo, wd_lo):
    i = pl.program_id(0)

    @pl.when((i == 0) | (be_ref[i] != be_ref[jnp.maximum(i - 1, 0)]))
    def _():
        wg_lo[...] = wg_ref[0].astype(BF16)
        wu_lo[...] = wu_ref[0].astype(BF16)
        wd_lo[...] = wd_ref[0].astype(BF16)

    xb = x_ref[...].astype(BF16)
    gate = jnp.dot(xb, wg_lo[...], preferred_element_type=F32)
    up = jnp.dot(xb, wu_lo[...], preferred_element_type=F32)
    hid = gate * _sigmoid(gate) * up
    y_ref[...] = jnp.dot(hid.astype(BF16), wd_lo[...], preferred_element_type=F32)


def _experts(block_e, xs, w_gate, w_up, w_down, layer):
    rows = xs.shape[0]
    nb = rows // MOE_ROWS
    grid_spec = pltpu.PrefetchScalarGridSpec(
        num_scalar_prefetch=1,
        grid=(nb,),
        in_specs=[pl.BlockSpec((MOE_ROWS, D_MODEL), lambda i, be: (i, 0)),
                  pl.BlockSpec((1, D_MODEL, D_EXPERT), lambda i, be: (layer * N_EXPERTS + be[i], 0, 0)),
                  pl.BlockSpec((1, D_MODEL, D_EXPERT), lambda i, be: (layer * N_EXPERTS + be[i], 0, 0)),
                  pl.BlockSpec((1, D_EXPERT, D_MODEL), lambda i, be: (layer * N_EXPERTS + be[i], 0, 0))],
        out_specs=pl.BlockSpec((MOE_ROWS, D_MODEL), lambda i, be: (i, 0)),
        scratch_shapes=[pltpu.VMEM((D_MODEL, D_EXPERT), BF16),
                        pltpu.VMEM((D_MODEL, D_EXPERT), BF16),
                        pltpu.VMEM((D_EXPERT, D_MODEL), BF16)],
    )
    return pl.pallas_call(
        _expert_kernel,
        grid_spec=grid_spec,
        out_shape=jax.ShapeDtypeStruct((rows, D_MODEL), F32),
        compiler_params=_cparams(("arbitrary",)),
        name="moe_experts",
    )(block_e, xs, w_gate, w_up, w_down)


def _combine_kernel(dest_ref, yb_ref, rt_ref, x_ref, gt_ref, fg_ref, o_ref, buf_ref, sem, *,
                    final_norm):
    tt = x_ref.shape[0]

    def row_copy(t, kk):
        d = dest_ref[0, 0, TOP_K * t + kk]
        return pltpu.make_async_copy(yb_ref.at[pl.ds(d, 1)], buf_ref.at[kk, pl.ds(t, 1)], sem)

    def issue(t, c):
        for kk in range(TOP_K):
            row_copy(t, kk).start(priority=kk)
        return c

    lax.fori_loop(0, tt, issue, 0, unroll=DMA_UNROLL)
    for kk in range(TOP_K):
        pltpu.make_async_copy(yb_ref.at[pl.ds(0, tt)], buf_ref.at[kk], sem).wait()
    rt = rt_ref[...]
    y = buf_ref[0] * rt[:, 2:3] + buf_ref[1] * rt[:, 3:4]
    x = x_ref[...] + gt_ref[0] * y
    if final_norm:
        ms = jnp.mean(x * x, axis=-1, keepdims=True)
        x = x * lax.rsqrt(ms + NORM_EPS) * fg_ref[...]
    o_ref[...] = x


def _combine(dest3, yb, rt, x1, mod3, final_g, layer, bsz, seq, final_norm):
    tok = x1.shape[0]
    nt, _, two_tt = dest3.shape
    tt = two_tt // TOP_K
    per_seq = seq // tt
    return pl.pallas_call(
        functools.partial(_combine_kernel, final_norm=final_norm),
        grid=(nt,),
        in_specs=[pl.BlockSpec((1, 1, two_tt), lambda i: (i, 0, 0), memory_space=pltpu.SMEM),
                  pl.BlockSpec(memory_space=pl.ANY),
                  pl.BlockSpec((tt, ROUTE_LANES), lambda i: (i, 0)),
                  pl.BlockSpec((tt, D_MODEL), lambda i: (i, 0)),
                  pl.BlockSpec((1, 1, D_MODEL), lambda i: (layer * bsz + i // per_seq, 0, 5)),
                  pl.BlockSpec((1, D_MODEL), lambda i: (0, 0))],
        out_specs=pl.BlockSpec((tt, D_MODEL), lambda i: (i, 0)),
        out_shape=jax.ShapeDtypeStruct((tok, D_MODEL), F32),
        scratch_shapes=[pltpu.VMEM((TOP_K, tt, D_MODEL), F32), pltpu.SemaphoreType.DMA],
        compiler_params=_cparams(("arbitrary",)),
        name="moe_combine",
    )(dest3, yb, rt, x1, mod3, final_g)


def _routing_tables(rt, tok):
    m = tok * TOP_K
    flat_e = rt[:, :TOP_K].astype(jnp.int32).reshape(m)
    onehot = (flat_e[:, None] == jnp.arange(N_EXPERTS, dtype=jnp.int32)[None, :]).astype(jnp.int32)
    ct = min(COUNT_TILE, m)
    tri = (jnp.arange(ct)[None, :] <= jnp.arange(ct)[:, None]).astype(BF16)
    within = jnp.einsum("ts,nse->nte", tri, onehot.reshape(m // ct, ct, N_EXPERTS).astype(BF16),
                        preferred_element_type=F32)
    tile_tot = within[:, -1, :]
    tile_base = jnp.cumsum(tile_tot, axis=0) - tile_tot
    running = (within + tile_base[:, None, :]).reshape(m, N_EXPERTS).astype(jnp.int32)
    counts = running[-1]
    padded = (counts + MOE_ROWS - 1) // MOE_ROWS * MOE_ROWS
    pad_end = jnp.cumsum(padded)
    pad_start = pad_end - padded
    dest = jnp.sum(onehot * (running - 1 + pad_start[None, :]), axis=1)
    rows = -(-m // MOE_ROWS) * MOE_ROWS + N_EXPERTS * MOE_ROWS
    nb = rows // MOE_ROWS
    block_start = jnp.arange(nb, dtype=jnp.int32) * MOE_ROWS
    block_e = jnp.minimum(jnp.sum((pad_end[None, :] <= block_start[:, None]).astype(jnp.int32), axis=1),
                          N_EXPERTS - 1).astype(jnp.int32)
    return dest.astype(jnp.int32), block_e, rows


def _rotary_tables(positions):
    half = ROT_DIM // 2
    inv_freq = ROPE_THETA ** (-jnp.arange(0, ROT_DIM, 2, dtype=F32) / ROT_DIM)
    ang = positions.astype(F32).reshape(-1, 1) * inv_freq[None, :]
    cos, sin = jnp.cos(ang), jnp.sin(ang)
    dim = jnp.arange(128)[None, :] % HEAD_DIM
    freq = jnp.arange(half)[:, None]
    lo = (dim == freq).astype(F32)
    hi = (dim == freq + half).astype(F32)
    rest = (dim >= ROT_DIM).astype(F32)
    place = functools.partial(jnp.dot, precision=lax.Precision.HIGHEST)
    return place(cos, lo + hi) + rest, place(sin, hi), place(sin, -lo)


def _lambda_init(layer):
    return 0.8 - 0.6 * math.exp(-0.3 * layer)


def kernel(x, c, positions, ada_w, ada_b, norm1_g, norm2_g, w_in, w_out, attn_lambda, attn_subln_g, rwkv_shift_mu, rwkv_w0, rwkv_w_up, rwkv_a0, rwkv_a_up, rwkv_g_up, rwkv_k_k, rwkv_k_a, rwkv_r_k, rwkv_lnx_g, rwkv_lnx_b, moe_w_group, moe_b_group, moe_w_router, moe_b_router, moe_w_gate, moe_w_up, moe_w_down, final_g):
    bsz, seq, d = x.shape
    depth = ada_w.shape[0]
    tok = bsz * seq
    x2 = x.reshape(tok, d)

    mod = _modulation(c, ada_w, ada_b)
    mod3 = mod.reshape(depth * bsz, 1, 6 * d)
    cos_t, s1_t, s2_t = _rotary_tables(positions)
    masks = _block_masks()
    tt = min(512, seq)

    for layer in range(depth):
        pad = LORA_PAD - LORA
        w_pad = jnp.pad(w_in[layer], ((0, 0), (0, pad))).astype(BF16)
        mu_pad = jnp.pad(rwkv_shift_mu[layer], (0, pad)).reshape(1, RWKV_COLS)
        qa, qb, k, v, p = _in_projection(x2, norm1_g[layer].reshape(1, d), mod3, layer, bsz, seq,
                                         w_pad, mu_pad, cos_t, s1_t, s2_t)
        attn_o = _diff_attention(qa, qb, k, v, attn_lambda[layer],
                                 attn_subln_g[layer].reshape(1, 2 * HEAD_DIM),
                                 _lambda_init(layer), bsz, seq)

        zrow = jnp.zeros((RWKV_WIDTH,), F32)
        prm = jnp.stack([rwkv_w0[layer], rwkv_a0[layer], rwkv_k_k[layer], rwkv_k_a[layer],
                         rwkv_r_k[layer].reshape(RWKV_WIDTH), rwkv_lnx_g[layer],
                         rwkv_lnx_b[layer], zrow])
        up_w = jnp.zeros((LORA_PAD, 3 * RWKV_WIDTH), F32)
        up_w = up_w.at[:W_LORA, :RWKV_WIDTH].set(rwkv_w_up[layer])
        up_w = up_w.at[W_LORA:W_LORA + A_LORA, RWKV_WIDTH:2 * RWKV_WIDTH].set(rwkv_a_up[layer])
        up_w = up_w.at[W_LORA + A_LORA:LORA, 2 * RWKV_WIDTH:].set(rwkv_g_up[layer])
        rwkv_o = _rwkv_mix(p, prm, up_w.astype(BF16), masks, bsz, seq)

        w_o = w_out[layer].astype(BF16)
        w_rt = jnp.zeros((d, ROUTE_LANES), F32)
        w_rt = w_rt.at[:, :N_GROUPS].set(moe_w_group[layer])
        w_rt = w_rt.at[:, N_GROUPS:N_GROUPS + N_EXPERTS].set(moe_w_router[layer])
        w_rt_hi = w_rt.astype(BF16)
        w_rt = jnp.stack([w_rt_hi, (w_rt - w_rt_hi.astype(F32)).astype(BF16)])
        b_rt = jnp.zeros((1, ROUTE_LANES), F32)
        b_rt = b_rt.at[0, :N_GROUPS].set(moe_b_group[layer])
        b_rt = b_rt.at[0, N_GROUPS:N_GROUPS + N_EXPERTS].set(moe_b_router[layer])
        x1, h2, rt = _out_projection(attn_o, rwkv_o, x2, w_o[:ATTN_WIDTH], w_o[ATTN_WIDTH:], mod3,
                                     norm2_g[layer].reshape(1, d), w_rt, b_rt, layer, bsz, seq)

        dest, block_e, rows = _routing_tables(rt, tok)
        dest3 = dest.reshape(tok // tt, 1, TOP_K * tt)
        xs = _dispatch(dest3, h2, jnp.zeros((rows, d), F32))
        yb = _experts(block_e, xs, moe_w_gate.reshape(-1, d, D_EXPERT),
                      moe_w_up.reshape(-1, d, D_EXPERT), moe_w_down.reshape(-1, D_EXPERT, d), layer)
        x2 = _combine(dest3, yb, rt, x1, mod3, final_g.reshape(1, d), layer, bsz, seq,
                      final_norm=(layer == depth - 1))

    return x2.reshape(bsz, seq, d)
```

```python
import functools
import math

import jax
import jax.numpy as jnp
from jax import lax
from jax.experimental import pallas as pl
from jax.experimental.pallas import tpu as pltpu

F32 = jnp.float32
BF16 = jnp.bfloat16

D_MODEL = 1024
ATTN_WIDTH = 512
ATTN_HEADS = 4
HEAD_DIM = 64
ROT_DIM = 16
ROPE_THETA = 500000.0
SUBLN_EPS = 1e-5
ATTN_CHUNK = 64

RWKV_WIDTH = 512
RWKV_HEAD = 64
W_LORA, A_LORA, G_LORA = 32, 32, 96
LORA = W_LORA + A_LORA + G_LORA
LORA_PAD = 256
GN_EPS = 64e-5
RWKV_COLS = 3 * RWKV_WIDTH + LORA_PAD
QKV_COLS = 3 * ATTN_WIDTH
WKV_CHUNK = 64
WKV_STEP = 256
SLAB = 256

N_GROUPS = 4
EXPERTS_PER_GROUP = 8
N_EXPERTS = 32
D_EXPERT = 256
TOP_K = 2
ROUTE_LANES = 128
MOE_ROWS = 512
DMA_UNROLL = 8
COUNT_TILE = 256

NORM_EPS = 1e-6
NEG_BIG = -1e30
VMEM_LIMIT = 56 * 1024 * 1024


def _cparams(sem):
    return pltpu.CompilerParams(dimension_semantics=sem, vmem_limit_bytes=VMEM_LIMIT)


def _mm(a, b):
    return jnp.dot(a.astype(BF16), b.astype(BF16), preferred_element_type=F32)


def _mm_nt(a, b):
    return lax.dot_general(a.astype(BF16), b.astype(BF16), (((1,), (1,)), ((), ())),
                           preferred_element_type=F32)


def _mm_tn(a, b):
    return lax.dot_general(a.astype(BF16), b.astype(BF16), (((0,), (0,)), ((), ())),
                           preferred_element_type=F32)


def _mm_split(m_exact, x):
    hi = x.astype(BF16)
    lo = (x - hi.astype(F32)).astype(BF16)
    return (jnp.dot(m_exact, hi, preferred_element_type=F32)
            + jnp.dot(m_exact, lo, preferred_element_type=F32))


def _split_mm(x, m_exact):
    hi = x.astype(BF16)
    lo = (x - hi.astype(F32)).astype(BF16)
    return (jnp.dot(hi, m_exact, preferred_element_type=F32)
            + jnp.dot(lo, m_exact, preferred_element_type=F32))


def _sigmoid(x):
    return 1.0 / (1.0 + jnp.exp(-x))


PACKED = D_MODEL // 2


def _pack_rows(x):
    bits = pltpu.bitcast(x.astype(BF16).astype(F32), jnp.uint32)
    return (bits[:, :PACKED] >> 16) | (bits[:, PACKED:] & jnp.uint32(0xFFFF0000))


def _unpack_rows(w):
    lo = pltpu.bitcast(w << 16, F32)
    hi = pltpu.bitcast(w & jnp.uint32(0xFFFF0000), F32)
    return jnp.concatenate([lo, hi], axis=1)


def _mod_kernel(c_ref, w_ref, b_ref, o_ref):
    c = c_ref[...]
    ca = c * _sigmoid(c)
    o_ref[0] = jnp.dot(ca, w_ref[0], preferred_element_type=F32,
                       precision=lax.Precision.HIGHEST) + b_ref[0]


def _modulation(c, ada_w, ada_b):
    depth, d, n = ada_w.shape
    bsz = c.shape[0]
    nb = n // d
    return pl.pallas_call(
        _mod_kernel,
        grid=(depth, nb),
        in_specs=[pl.BlockSpec((bsz, d), lambda l, j: (0, 0)),
                  pl.BlockSpec((1, d, d), lambda l, j: (l, 0, j)),
                  pl.BlockSpec((1, 1, d), lambda l, j: (l, 0, j))],
        out_specs=pl.BlockSpec((1, bsz, d), lambda l, j: (l, 0, j)),
        out_shape=jax.ShapeDtypeStruct((depth, bsz, n), F32),
        compiler_params=_cparams(("arbitrary", "arbitrary")),
        name="adaln_mod",
    )(c, ada_w, ada_b.reshape(depth, 1, n))


def _inproj_kernel(x_ref, g_ref, sc_ref, sh_ref, w_ref, mu_ref, cos_ref, s1_ref, s2_ref,
                   qa_ref, qb_ref, k_ref, vt_ref, p_ref, carry_ref):
    i = pl.program_id(1)
    tm = x_ref.shape[0]
    x = x_ref[...]
    ms = jnp.mean(x * x, axis=-1, keepdims=True)
    h = x * lax.rsqrt(ms + NORM_EPS) * g_ref[...]
    h = h * (1.0 + sc_ref[0]) + sh_ref[0]
    hb = h.astype(BF16)

    cosv, s1v, s2v = cos_ref[...], s1_ref[...], s2_ref[...]

    def rot128(t):
        return (t * cosv + pltpu.roll(t, ROT_DIM // 2, 1) * s1v
                + pltpu.roll(t, 128 - ROT_DIM // 2, 1) * s2v)

    lane = lax.broadcasted_iota(jnp.int32, (1, 128), 1)
    first_map = lane < HEAD_DIM
    scale = HEAD_DIM ** -0.5 * LOG2E
    qkv = jnp.dot(hb, w_ref[:, :QKV_COLS], preferred_element_type=F32)
    for hd in range(ATTN_HEADS):
        lo, hi = hd * 128, (hd + 1) * 128
        q = rot128(qkv[:, lo:hi]) * scale
        qa_ref[:, lo:hi] = jnp.where(first_map, q, 0.0).astype(BF16)
        qb_ref[:, lo:hi] = jnp.where(first_map, 0.0, q).astype(BF16)
        k_ref[:, lo:hi] = rot128(qkv[:, ATTN_WIDTH + lo:ATTN_WIDTH + hi]).astype(BF16)
    vt_ref[...] = qkv[:, 2 * ATTN_WIDTH:].T.astype(BF16)

    @pl.when(i == 0)
    def _():
        carry_ref[...] = jnp.zeros_like(carry_ref)

    p = jnp.dot(hb, w_ref[:, QKV_COLS:], preferred_element_type=F32)
    row = lax.broadcasted_iota(jnp.int32, p.shape, 0)
    prev = jnp.where(row == 0, carry_ref[...], pltpu.roll(p, 1, 0))
    carry_ref[...] = p[tm - 1:tm, :]
    p_ref[...] = (p + (prev - p) * mu_ref[...]).astype(BF16)


def _in_projection(x2, g1, mod3, layer, bsz, seq, w_pad, mu_pad, cos_t, s1_t, s2_t):
    tm = min(512, seq)
    nt = seq // tm
    tok = x2.shape[0]
    row = lambda b, i: (b * nt + i, 0)
    const = lambda b, i: (0, 0)
    wide = w_pad.shape[1]
    outs = pl.pallas_call(
        _inproj_kernel,
        grid=(bsz, nt),
        in_specs=[pl.BlockSpec((tm, D_MODEL), row),
                  pl.BlockSpec((1, D_MODEL), const),
                  pl.BlockSpec((1, 1, D_MODEL), lambda b, i: (layer * bsz + b, 0, 1)),
                  pl.BlockSpec((1, 1, D_MODEL), lambda b, i: (layer * bsz + b, 0, 0)),
                  pl.BlockSpec((D_MODEL, wide), const),
                  pl.BlockSpec((1, RWKV_COLS), const),
                  pl.BlockSpec((tm, 128), row),
                  pl.BlockSpec((tm, 128), row),
                  pl.BlockSpec((tm, 128), row)],
        out_specs=[pl.BlockSpec((tm, ATTN_WIDTH), row)] * 3
        + [pl.BlockSpec((ATTN_WIDTH, tm), lambda b, i: (b, i)), pl.BlockSpec((tm, RWKV_COLS), row)],
        out_shape=[jax.ShapeDtypeStruct((tok, ATTN_WIDTH), BF16)] * 3
        + [jax.ShapeDtypeStruct((bsz * ATTN_WIDTH, seq), BF16),
           jax.ShapeDtypeStruct((tok, RWKV_COLS), BF16)],
        scratch_shapes=[pltpu.VMEM((1, RWKV_COLS), F32)],
        compiler_params=_cparams(("arbitrary", "arbitrary")),
        name="in_proj",
    )(x2, g1, mod3, mod3, w_pad, mu_pad, cos_t, s1_t, s2_t)
    return outs


ATTN_TILE = 1024
ATTN_QLANES = 256
ATTN_KEYS = 512
ATTN_AHEAD = 4
LOG2E = 1.4426950408889634


def _attn_kernel(qi_ref, kj_ref, qa_ref, qb_ref, k_ref, vt_ref, lam_ref, g_ref, o_ref,
                 m_ref, l_ref, acc_ref, *, lambda_init):
    step = pl.program_id(1)
    i = qi_ref[step]
    j = kj_ref[step]
    tq = qa_ref.shape[0]
    tk = k_ref.shape[0]
    hq = min(ATTN_QLANES, tq)
    n_half = tq // hq

    @pl.when(j == 0)
    def _():
        m_ref[...] = jnp.full(m_ref.shape, NEG_BIG, F32)
        l_ref[...] = jnp.zeros_like(l_ref)
        acc_ref[...] = jnp.zeros_like(acc_ref)

    kb = min(ATTN_KEYS, tk)

    def accumulate(diagonal):
        def visible(half, k0, kn):
            kc = (lax.broadcasted_iota(jnp.int32, (kn, hq), 0) + k0) // ATTN_CHUNK
            qc = (lax.broadcasted_iota(jnp.int32, (kn, hq), 1) + half * hq) // ATTN_CHUNK
            return kc <= qc

        spans = []
        for half in range(n_half):
            if not diagonal:
                spans += [(half, k0, kb, False) for k0 in range(0, tk, kb)]
            else:
                if half > 0:
                    spans.append((half, 0, half * hq, False))
                spans.append((half, half * hq, hq, True))
        chains = [(half, k0, kn, masked, hd, mp) for (half, k0, kn, masked) in spans
                  for hd in range(ATTN_HEADS) for mp in range(2)]

        def scores(chain):
            half, k0, kn, _, hd, mp = chain
            q_ref = (qa_ref, qb_ref)[mp]
            return lax.dot_general(k_ref[k0:k0 + kn, hd * 128:(hd + 1) * 128],
                                   q_ref[half * hq:(half + 1) * hq, hd * 128:(hd + 1) * 128],
                                   (((1,), (1,)), ((), ())), preferred_element_type=F32)

        ahead = [scores(c) for c in chains[:ATTN_AHEAD]]
        for n, (half, k0, kn, masked, hd, mp) in enumerate(chains):
            s = ahead.pop(0)
            if n + ATTN_AHEAD < len(chains):
                ahead.append(scores(chains[n + ATTN_AHEAD]))
            idx = (2 * hd + mp) * n_half + half
            if masked:
                s = jnp.where(visible(half, k0, kn), s, NEG_BIG)
            m_old = m_ref[idx]
            m_new = jnp.maximum(m_old, jnp.max(s, axis=0, keepdims=True))
            alpha = jnp.exp2(m_old - m_new)
            p = jnp.exp2(s - m_new)
            l_ref[idx] = alpha * l_ref[idx] + jnp.sum(p, axis=0, keepdims=True)
            acc_ref[idx] = alpha * acc_ref[idx] + jnp.dot(
                vt_ref[hd * 128:(hd + 1) * 128, k0:k0 + kn], p.astype(BF16),
                preferred_element_type=F32)
            m_ref[idx] = m_new

    @pl.when(j < i)
    def _():
        accumulate(False)

    @pl.when(j == i)
    def _():
        accumulate(True)
        lv = lam_ref[...]
        lam = (jnp.exp(jnp.sum(lv[0:1] * lv[1:2], axis=-1, keepdims=True))
               - jnp.exp(jnp.sum(lv[2:3] * lv[3:4], axis=-1, keepdims=True)) + lambda_init)
        gcol = jnp.concatenate([g_ref[...]] * (hq // 128), axis=1)
        for hd in range(ATTN_HEADS):
            for half in range(n_half):
                a = (2 * hd) * n_half + half
                b = (2 * hd + 1) * n_half + half
                o = acc_ref[a] / l_ref[a] - lam * (acc_ref[b] / l_ref[b])
                ms = jnp.mean(o * o, axis=0, keepdims=True)
                o = o * lax.rsqrt(ms + SUBLN_EPS) * gcol * (1.0 - lambda_init)
                o_ref[half * hq:(half + 1) * hq, hd * 128:(hd + 1) * 128] = o.T.astype(BF16)


def _diff_attention(qa, qb, k, vt, lam_vecs, subln_g, lambda_init, bsz, seq):
    tq = min(ATTN_TILE, seq)
    nq = seq // tq
    tok = qa.shape[0]
    hq = min(ATTN_QLANES, tq)
    n_stat = 2 * ATTN_HEADS * (tq // hq)
    pairs = [(i, j) for i in range(nq) for j in range(i + 1)]
    qi = jnp.asarray([p[0] for p in pairs], jnp.int32)
    kj = jnp.asarray([p[1] for p in pairs], jnp.int32)
    qmap = lambda b, s, qi, kj: (b * nq + qi[s], 0)
    kmap = lambda b, s, qi, kj: (b * nq + kj[s], 0)
    vmap = lambda b, s, qi, kj: (b, kj[s])
    const = lambda b, s, qi, kj: (0, 0)
    g_col = jnp.broadcast_to(subln_g.reshape(2 * HEAD_DIM, 1), (2 * HEAD_DIM, 128))
    grid_spec = pltpu.PrefetchScalarGridSpec(
        num_scalar_prefetch=2,
        grid=(bsz, len(pairs)),
        in_specs=[pl.BlockSpec((tq, ATTN_WIDTH), qmap),
                  pl.BlockSpec((tq, ATTN_WIDTH), qmap),
                  pl.BlockSpec((tq, ATTN_WIDTH), kmap),
                  pl.BlockSpec((ATTN_WIDTH, tq), vmap),
                  pl.BlockSpec((4, HEAD_DIM), const),
                  pl.BlockSpec((2 * HEAD_DIM, 128), const)],
        out_specs=pl.BlockSpec((tq, ATTN_WIDTH), qmap),
        scratch_shapes=[pltpu.VMEM((n_stat, 1, hq), F32),
                        pltpu.VMEM((n_stat, 1, hq), F32),
                        pltpu.VMEM((n_stat, 2 * HEAD_DIM, hq), F32)],
    )
    return pl.pallas_call(
        functools.partial(_attn_kernel, lambda_init=lambda_init),
        grid_spec=grid_spec,
        out_shape=jax.ShapeDtypeStruct((tok, ATTN_WIDTH), BF16),
        compiler_params=_cparams(("arbitrary", "arbitrary")),
        name="diff_attn",
    )(qi, kj, qa, qb, k, vt, lam_vecs, g_col)


def _softplus(z):
    return jnp.maximum(z, 0.0) + jnp.log(1.0 + jnp.exp(-jnp.abs(z)))


def _rwkv_kernel(p_ref, prm_ref, up_ref, tri_ref, bd_ref, sl_ref, il_ref, eye_ref,
                 o_ref, h_ref):
    t = pl.program_id(1)
    C = WKV_CHUNK
    n_chunks = p_ref.shape[0] // C
    W = RWKV_WIDTH

    @pl.when(t == 0)
    def _():
        h_ref[...] = jnp.zeros_like(h_ref)

    p = p_ref[...].astype(F32)
    r, k, v, lo = p[:, :W], p[:, W:2 * W], p[:, 2 * W:3 * W], p[:, 3 * W:]
    lane = lax.broadcasted_iota(jnp.int32, lo.shape, 1)
    act = jnp.where(lane < W_LORA, jnp.tanh(lo), jnp.where(lane < W_LORA + A_LORA, lo, _sigmoid(lo)))
    up = _mm(act, up_ref[...])
    prm = prm_ref[...]
    w0, a0, k_k, k_a, r_k, ln_g, ln_b = (prm[n:n + 1] for n in range(7))
    w_raw = -_softplus(-(w0 + up[:, :W])) - 0.5
    logd = -jnp.exp(w_raw)
    a_sig = _sigmoid(a0 + up[:, W:2 * W])
    gate = up[:, 2 * W:]

    bd = bd_ref[...]
    bd_lo = bd.astype(BF16)

    def head_sum(z):
        return jnp.concatenate([_mm(z[:, g * SLAB:(g + 1) * SLAB], bd_lo)
                                for g in range(W // SLAB)], axis=1)

    kkr = k * k_k
    kk = kkr / jnp.maximum(jnp.sqrt(head_sum(kkr * kkr)), 1e-12)
    kf = k * (1.0 + (a_sig - 1.0) * k_a)
    bonus = head_sum(r * kf * r_k) * v
    a_vec = -kk
    b_vec = kk * a_sig

    cum = _mm_split(tri_ref[...], logd)
    cum_end = jnp.concatenate(
        [jnp.broadcast_to(cum[(ch + 1) * C - 1:(ch + 1) * C, :], (C, W)) for ch in range(n_chunks)],
        axis=0)
    d_in = jnp.exp(cum)
    d_ex = jnp.exp(cum - logd)
    d_inv = jnp.exp(-cum)
    d_end = jnp.exp(cum_end)
    d_rest = jnp.exp(cum_end - cum)

    at, rt = a_vec * d_ex, r * d_in
    bt, kt = b_vec * d_inv, kf * d_inv
    be, ke = b_vec * d_rest, kf * d_rest

    sl = sl_ref[...]
    il = il_ref[...]
    eye_c = il - sl
    eye_s = eye_ref[...]
    heads = SLAB // RWKV_HEAD
    n_slabs = W // SLAB
    probs = [(ch, g) for ch in range(n_chunks) for g in range(n_slabs)]

    def cut(z, ch, g):
        return z[ch * C:(ch + 1) * C, g * SLAB:(g + 1) * SLAB]

    def stack(zz):
        return jnp.concatenate([zz.astype(BF16)] * heads, axis=0) * bd_lo

    ats = [cut(at, *pr) for pr in probs]
    rts = [cut(rt, *pr) for pr in probs]
    vs = [cut(v, *pr) for pr in probs]
    btm = [stack(cut(bt, *pr)) for pr in probs]
    ktm = [stack(cut(kt, *pr)) for pr in probs]
    vm = [stack(v_) for v_ in vs]
    l_ab = [sl * _mm_nt(a_, b_) for a_, b_ in zip(ats, btm)]
    l_ak = [sl * _mm_nt(a_, k_) for a_, k_ in zip(ats, ktm)]
    lpm = [stack(l_) for l_ in l_ab]
    lp = [_mm(l_, m_) for l_, m_ in zip(l_ab, lpm)]
    inv = [eye_c + l_ for l_ in l_ab]
    wv = [_mm(l_, v_) for l_, v_ in zip(l_ak, vm)]
    m_rb = [il * _mm_nt(r_, b_) for r_, b_ in zip(rts, btm)]
    m_rk = [il * _mm_nt(r_, k_) for r_, k_ in zip(rts, ktm)]
    levels = int(math.log2(C)) - 1
    for lev in range(levels):
        lpm = [stack(l_) for l_ in lp]
        inv_new = [i_ + _mm(i_, m_) for i_, m_ in zip(inv, lpm)]
        if lev + 1 < levels:
            lp = [_mm(l_, m_) for l_, m_ in zip(lp, lpm)]
        inv = inv_new
    abar = [_mm(i_, stack(a_)) for i_, a_ in zip(inv, ats)]
    vbar = [_mm(i_, stack(w_)) for i_, w_ in zip(inv, wv)]
    abm = [stack(a_) for a_ in abar]
    vbm = [stack(v_) for v_ in vbar]
    rhat = [r_ + _mm(m_, a_) for r_, m_, a_ in zip(rts, m_rb, abm)]
    yhat = [_mm(mb_, vb_) + _mm(mk_, v_) for mb_, vb_, mk_, v_ in zip(m_rb, vbm, m_rk, vm)]
    bes = [cut(be, *pr) for pr in probs]
    kes = [cut(ke, *pr) for pr in probs]
    gmat = [eye_s * cut(d_end, *pr)[0:1] + bd * _mm_tn(b_, a_) for pr, b_, a_ in zip(probs, bes, abar)]
    fmat = [bd * _mm_tn(jnp.concatenate([b_, k_], axis=0), jnp.concatenate([vb_, v_], axis=0))
            for b_, k_, vb_, v_ in zip(bes, kes, vbar, vs)]

    hs = [h_ref[g] for g in range(n_slabs)]
    rows = []
    for ch in range(n_chunks):
        ys = []
        for g in range(n_slabs):
            n = ch * n_slabs + g
            ys.append(_mm(rhat[n], hs[g]) + yhat[n])
            hs[g] = _mm(gmat[n], hs[g]) + fmat[n]
        rows.append(jnp.concatenate(ys, axis=1))
    for g in range(n_slabs):
        h_ref[g] = hs[g]
    y = jnp.concatenate(rows, axis=0) if n_chunks > 1 else rows[0]

    inv_n = 1.0 / RWKV_HEAD
    mu = head_sum(y) * inv_n
    yc = y - mu
    var = head_sum(yc * yc) * inv_n
    yn = yc * lax.rsqrt(var + GN_EPS) * ln_g + ln_b
    o_ref[...] = ((yn + bonus) * gate).astype(BF16)


def _block_masks():
    n = SLAB
    r = jnp.arange(n)[:, None]
    c = jnp.arange(n)[None, :]
    same = (r // RWKV_HEAD) == (c // RWKV_HEAD)
    bd = same.astype(F32)
    eye = (r == c).astype(F32)
    t = jnp.arange(WKV_CHUNK)[:, None]
    s_in_head = c % RWKV_HEAD
    sl = (s_in_head < t).astype(F32)
    il = (s_in_head <= t).astype(F32)
    tt = jnp.arange(WKV_STEP)
    tri = ((tt[None, :] <= tt[:, None])
           & (tt[None, :] // WKV_CHUNK == tt[:, None] // WKV_CHUNK)).astype(BF16)
    return tri, bd, sl, il, eye


def _rwkv_mix(p, prm, up_w, masks, bsz, seq):
    C = WKV_STEP
    nc = seq // C
    tok = p.shape[0]
    tri, bd, sl, il, eye = masks
    row = lambda b, t: (b * nc + t, 0)
    const = lambda b, t: (0, 0)
    return pl.pallas_call(
        _rwkv_kernel,
        grid=(bsz, nc),
        in_specs=[pl.BlockSpec((C, RWKV_COLS), row),
                  pl.BlockSpec((8, RWKV_WIDTH), const),
                  pl.BlockSpec((LORA_PAD, 3 * RWKV_WIDTH), const),
                  pl.BlockSpec((C, C), const),
                  pl.BlockSpec((SLAB, SLAB), const),
                  pl.BlockSpec((WKV_CHUNK, SLAB), const),
                  pl.BlockSpec((WKV_CHUNK, SLAB), const),
                  pl.BlockSpec((SLAB, SLAB), const)],
        out_specs=pl.BlockSpec((C, RWKV_WIDTH), row),
        out_shape=jax.ShapeDtypeStruct((tok, RWKV_WIDTH), BF16),
        scratch_shapes=[pltpu.VMEM((RWKV_WIDTH // SLAB, SLAB, SLAB), F32)],
        compiler_params=_cparams(("arbitrary", "arbitrary")),
        name="rwkv7_mix",
    )(p, prm, up_w, tri, bd, sl, il, eye)


def _out_kernel(a_ref, r_ref, x_ref, wa_ref, wr_ref, gt_ref, g2_ref, sc_ref, sh_ref, wrt_ref,
                brt_ref, x1_ref, h2_ref, rt_ref):
    mixed = (jnp.dot(a_ref[...], wa_ref[...], preferred_element_type=F32)
             + jnp.dot(r_ref[...], wr_ref[...], preferred_element_type=F32))
    x1 = x_ref[...] + gt_ref[0] * mixed
    x1_ref[...] = x1
    ms = jnp.mean(x1 * x1, axis=-1, keepdims=True)
    h2 = x1 * lax.rsqrt(ms + NORM_EPS) * g2_ref[...]
    h2 = h2 * (1.0 + sc_ref[0]) + sh_ref[0]
    h2_ref[...] = _pack_rows(h2)
    h_hi = h2.astype(BF16)
    h_lo = (h2 - h_hi.astype(F32)).astype(BF16)
    logits = (jnp.dot(h_hi, wrt_ref[0], preferred_element_type=F32)
              + jnp.dot(h_lo, wrt_ref[0], preferred_element_type=F32)
              + jnp.dot(h_hi, wrt_ref[1], preferred_element_type=F32)) + brt_ref[...]

    lane_i = lax.broadcasted_iota(jnp.int32, logits.shape, 1)
    lane = lane_i.astype(F32)
    far = float(ROUTE_LANES)
    is_g = lane_i < N_GROUPS
    gmax = jnp.max(jnp.where(is_g, logits, NEG_BIG), axis=-1, keepdims=True)
    gsum = jnp.sum(jnp.where(is_g, jnp.exp(logits - gmax), 0.0), axis=-1, keepdims=True)
    g_top = 1.0 / gsum
    g_idx = jnp.min(jnp.where(is_g & (logits == gmax), lane, far), axis=-1, keepdims=True)
    base = N_GROUPS + g_idx * EXPERTS_PER_GROUP
    in_g = (lane >= base) & (lane < base + EXPERTS_PER_GROUP)
    v1 = jnp.max(jnp.where(in_g, logits, NEG_BIG), axis=-1, keepdims=True)
    i1 = jnp.min(jnp.where(in_g & (logits == v1), lane, far), axis=-1, keepdims=True)
    rest = in_g & (lane != i1)
    v2 = jnp.max(jnp.where(rest, logits, NEG_BIG), axis=-1, keepdims=True)
    i2 = jnp.min(jnp.where(rest & (logits == v2), lane, far), axis=-1, keepdims=True)
    e21 = jnp.exp(v2 - v1)
    w1 = g_top / (1.0 + e21)
    w2 = g_top * e21 / (1.0 + e21)
    rt_ref[...] = jnp.where(lane_i == 0, i1 - N_GROUPS,
                            jnp.where(lane_i == 1, i2 - N_GROUPS,
                                      jnp.where(lane_i == 2, w1,
                                                jnp.where(lane_i == 3, w2, 0.0))))


def _out_projection(attn_o, rwkv_o, x2, w_a, w_r, mod3, g2, w_rt, b_rt, layer, bsz, seq):
    tm = min(512, seq)
    nt = seq // tm
    tok = x2.shape[0]
    row = lambda b, i: (b * nt + i, 0)
    const = lambda b, i: (0, 0)
    modspec = lambda col: pl.BlockSpec((1, 1, D_MODEL), lambda b, i: (layer * bsz + b, 0, col))
    return pl.pallas_call(
        _out_kernel,
        grid=(bsz, nt),
        in_specs=[pl.BlockSpec((tm, ATTN_WIDTH), row),
                  pl.BlockSpec((tm, RWKV_WIDTH), row),
                  pl.BlockSpec((tm, D_MODEL), row),
                  pl.BlockSpec((ATTN_WIDTH, D_MODEL), const),
                  pl.BlockSpec((RWKV_WIDTH, D_MODEL), const),
                  modspec(2),
                  pl.BlockSpec((1, D_MODEL), const),
                  modspec(4),
                  modspec(3),
                  pl.BlockSpec((2, D_MODEL, ROUTE_LANES), lambda b, i: (0, 0, 0)),
                  pl.BlockSpec((1, ROUTE_LANES), const)],
        out_specs=[pl.BlockSpec((tm, D_MODEL), row),
                   pl.BlockSpec((tm, PACKED), row),
                   pl.BlockSpec((tm, ROUTE_LANES), row)],
        out_shape=[jax.ShapeDtypeStruct((tok, D_MODEL), F32),
                   jax.ShapeDtypeStruct((tok, PACKED), jnp.uint32),
                   jax.ShapeDtypeStruct((tok, ROUTE_LANES), F32)],
        compiler_params=_cparams(("arbitrary", "arbitrary")),
        name="out_proj_router",
    )(attn_o, rwkv_o, x2, w_a, w_r, mod3, g2, mod3, mod3, w_rt, b_rt)


def _dispatch_kernel(dest_ref, h_ref, xin_ref, xs_ref, sem):
    del xin_ref
    tt = h_ref.shape[0]

    def row_copy(t, kk):
        d = dest_ref[0, 0, TOP_K * t + kk]
        return pltpu.make_async_copy(h_ref.at[pl.ds(t, 1)], xs_ref.at[pl.ds(d, 1)], sem)

    def issue(t, c):
        for kk in range(TOP_K):
            row_copy(t, kk).start(priority=kk)
        return c

    lax.fori_loop(0, tt, issue, 0, unroll=DMA_UNROLL)
    for kk in range(TOP_K):
        pltpu.make_async_copy(h_ref, xs_ref.at[pl.ds(0, tt)], sem).wait()


def _dispatch(dest3, h2, xs_init):
    tok = h2.shape[0]
    nt, _, two_tt = dest3.shape
    tt = two_tt // TOP_K
    return pl.pallas_call(
        _dispatch_kernel,
        grid=(nt,),
        in_specs=[pl.BlockSpec((1, 1, two_tt), lambda i: (i, 0, 0), memory_space=pltpu.SMEM),
                  pl.BlockSpec((tt, PACKED), lambda i: (i, 0)),
                  pl.BlockSpec(memory_space=pl.ANY)],
        out_specs=pl.BlockSpec(memory_space=pl.ANY),
        out_shape=jax.ShapeDtypeStruct(xs_init.shape, jnp.uint32),
        scratch_shapes=[pltpu.SemaphoreType.DMA],
        input_output_aliases={2: 0},
        compiler_params=_cparams(("arbitrary",)),
        name="moe_dispatch",
    )(dest3, h2, xs_init)


def _expert_kernel(be_ref, x_ref, wg_ref, wu_ref, wd_ref, y_ref, wg_lo, wu_lo, wd_lo):
    i = pl.program_id(0)

    @pl.when((i == 0) | (be_ref[i] != be_ref[jnp.maximum(i - 1, 0)]))
    def _():
        wg_lo[...] = wg_ref[0].astype(BF16)
        wu_lo[...] = wu_ref[0].astype(BF16)
        wd_lo[...] = wd_ref[0].astype(BF16)

    xb = _unpack_rows(x_ref[...]).astype(BF16)
    gate = jnp.dot(xb, wg_lo[...], preferred_element_type=F32)
    up = jnp.dot(xb, wu_lo[...], preferred_element_type=F32)
    hid = gate * _sigmoid(gate) * up
    y_ref[...] = _pack_rows(jnp.dot(hid.astype(BF16), wd_lo[...], preferred_element_type=F32))


def _experts(block_e, xs, w_gate, w_up, w_down, layer):
    rows = xs.shape[0]
    nb = rows // MOE_ROWS
    grid_spec = pltpu.PrefetchScalarGridSpec(
        num_scalar_prefetch=1,
        grid=(nb,),
        in_specs=[pl.BlockSpec((MOE_ROWS, PACKED), lambda i, be: (i, 0)),
                  pl.BlockSpec((1, D_MODEL, D_EXPERT), lambda i, be: (layer * N_EXPERTS + be[i], 0, 0)),
                  pl.BlockSpec((1, D_MODEL, D_EXPERT), lambda i, be: (layer * N_EXPERTS + be[i], 0, 0)),
                  pl.BlockSpec((1, D_EXPERT, D_MODEL), lambda i, be: (layer * N_EXPERTS + be[i], 0, 0))],
        out_specs=pl.BlockSpec((MOE_ROWS, PACKED), lambda i, be: (i, 0)),
        scratch_shapes=[pltpu.VMEM((D_MODEL, D_EXPERT), BF16),
                        pltpu.VMEM((D_MODEL, D_EXPERT), BF16),
                        pltpu.VMEM((D_EXPERT, D_MODEL), BF16)],
    )
    return pl.pallas_call(
        _expert_kernel,
        grid_spec=grid_spec,
        out_shape=jax.ShapeDtypeStruct((rows, PACKED), jnp.uint32),
        compiler_params=_cparams(("arbitrary",)),
        name="moe_experts",
    )(block_e, xs, w_gate, w_up, w_down)


def _combine_kernel(dest_ref, yb_ref, rt_ref, x_ref, gt_ref, fg_ref, o_ref, buf_ref, sem, *,
                    final_norm):
    tt = x_ref.shape[0]

    def row_copy(t, kk):
        d = dest_ref[0, 0, TOP_K * t + kk]
        return pltpu.make_async_copy(yb_ref.at[pl.ds(d, 1)], buf_ref.at[kk, pl.ds(t, 1)], sem)

    def issue(t, c):
        for kk in range(TOP_K):
            row_copy(t, kk).start(priority=kk)
        return c

    lax.fori_loop(0, tt, issue, 0, unroll=DMA_UNROLL)
    for kk in range(TOP_K):
        pltpu.make_async_copy(yb_ref.at[pl.ds(0, tt)], buf_ref.at[kk], sem).wait()
    rt = rt_ref[...]
    y = _unpack_rows(buf_ref[0]) * rt[:, 2:3] + _unpack_rows(buf_ref[1]) * rt[:, 3:4]
    x = x_ref[...] + gt_ref[0] * y
    if final_norm:
        ms = jnp.mean(x * x, axis=-1, keepdims=True)
        x = x * lax.rsqrt(ms + NORM_EPS) * fg_ref[...]
    o_ref[...] = x


def _combine(dest3, yb, rt, x1, mod3, final_g, layer, bsz, seq, final_norm):
    tok = x1.shape[0]
    nt, _, two_tt = dest3.shape
    tt = two_tt // TOP_K
    per_seq = seq // tt
    return pl.pallas_call(
        functools.partial(_combine_kernel, final_norm=final_norm),
        grid=(nt,),
        in_specs=[pl.BlockSpec((1, 1, two_tt), lambda i: (i, 0, 0), memory_space=pltpu.SMEM),
                  pl.BlockSpec(memory_space=pl.ANY),
                  pl.BlockSpec((tt, ROUTE_LANES), lambda i: (i, 0)),
                  pl.BlockSpec((tt, D_MODEL), lambda i: (i, 0)),
                  pl.BlockSpec((1, 1, D_MODEL), lambda i: (layer * bsz + i // per_seq, 0, 5)),
                  pl.BlockSpec((1, D_MODEL), lambda i: (0, 0))],
        out_specs=pl.BlockSpec((tt, D_MODEL), lambda i: (i, 0)),
        out_shape=jax.ShapeDtypeStruct((tok, D_MODEL), F32),
        scratch_shapes=[pltpu.VMEM((TOP_K, tt, PACKED), jnp.uint32), pltpu.SemaphoreType.DMA],
        compiler_params=_cparams(("arbitrary",)),
        name="moe_combine",
    )(dest3, yb, rt, x1, mod3, final_g)


def _routing_tables(rt, tok):
    m = tok * TOP_K
    flat_e = rt[:, :TOP_K].astype(jnp.int32).reshape(m)
    onehot = (flat_e[:, None] == jnp.arange(N_EXPERTS, dtype=jnp.int32)[None, :]).astype(jnp.int32)
    ct = min(COUNT_TILE, m)
    tri = (jnp.arange(ct)[None, :] <= jnp.arange(ct)[:, None]).astype(BF16)
    within = jnp.einsum("ts,nse->nte", tri, onehot.reshape(m // ct, ct, N_EXPERTS).astype(BF16),
                        preferred_element_type=F32)
    tile_tot = within[:, -1, :]
    tile_base = jnp.cumsum(tile_tot, axis=0) - tile_tot
    running = (within + tile_base[:, None, :]).reshape(m, N_EXPERTS).astype(jnp.int32)
    counts = running[-1]
    padded = (counts + MOE_ROWS - 1) // MOE_ROWS * MOE_ROWS
    pad_end = jnp.cumsum(padded)
    pad_start = pad_end - padded
    dest = jnp.sum(onehot * (running - 1 + pad_start[None, :]), axis=1)
    rows = -(-m // MOE_ROWS) * MOE_ROWS + N_EXPERTS * MOE_ROWS
    nb = rows // MOE_ROWS
    block_start = jnp.arange(nb, dtype=jnp.int32) * MOE_ROWS
    block_e = jnp.minimum(jnp.sum((pad_end[None, :] <= block_start[:, None]).astype(jnp.int32), axis=1),
                          N_EXPERTS - 1).astype(jnp.int32)
    return dest.astype(jnp.int32), block_e, rows


def _rotary_tables(positions):
    half = ROT_DIM // 2
    inv_freq = ROPE_THETA ** (-jnp.arange(0, ROT_DIM, 2, dtype=F32) / ROT_DIM)
    ang = positions.astype(F32).reshape(-1, 1) * inv_freq[None, :]
    cos, sin = jnp.cos(ang), jnp.sin(ang)
    dim = jnp.arange(128)[None, :] % HEAD_DIM
    freq = jnp.arange(half)[:, None]
    lo = (dim == freq).astype(F32)
    hi = (dim == freq + half).astype(F32)
    rest = (dim >= ROT_DIM).astype(F32)
    place = functools.partial(jnp.dot, precision=lax.Precision.HIGHEST)
    return place(cos, lo + hi) + rest, place(sin, hi), place(sin, -lo)


def _lambda_init(layer):
    return 0.8 - 0.6 * math.exp(-0.3 * layer)


def kernel(x, c, positions, ada_w, ada_b, norm1_g, norm2_g, w_in, w_out, attn_lambda, attn_subln_g, rwkv_shift_mu, rwkv_w0, rwkv_w_up, rwkv_a0, rwkv_a_up, rwkv_g_up, rwkv_k_k, rwkv_k_a, rwkv_r_k, rwkv_lnx_g, rwkv_lnx_b, moe_w_group, moe_b_group, moe_w_router, moe_b_router, moe_w_gate, moe_w_up, moe_w_down, final_g):
    bsz, seq, d = x.shape
    depth = ada_w.shape[0]
    tok = bsz * seq
    x2 = x.reshape(tok, d)

    mod = _modulation(c, ada_w, ada_b)
    mod3 = mod.reshape(depth * bsz, 1, 6 * d)
    cos_t, s1_t, s2_t = _rotary_tables(positions)
    masks = _block_masks()
    tt = min(512, seq)

    for layer in range(depth):
        pad = LORA_PAD - LORA
        w_pad = jnp.pad(w_in[layer], ((0, 0), (0, pad))).astype(BF16)
        mu_pad = jnp.pad(rwkv_shift_mu[layer], (0, pad)).reshape(1, RWKV_COLS)
        qa, qb, k, v, p = _in_projection(x2, norm1_g[layer].reshape(1, d), mod3, layer, bsz, seq,
                                         w_pad, mu_pad, cos_t, s1_t, s2_t)
        attn_o = _diff_attention(qa, qb, k, v, attn_lambda[layer],
                                 attn_subln_g[layer].reshape(1, 2 * HEAD_DIM),
                                 _lambda_init(layer), bsz, seq)

        zrow = jnp.zeros((RWKV_WIDTH,), F32)
        prm = jnp.stack([rwkv_w0[layer], rwkv_a0[layer], rwkv_k_k[layer], rwkv_k_a[layer],
                         rwkv_r_k[layer].reshape(RWKV_WIDTH), rwkv_lnx_g[layer],
                         rwkv_lnx_b[layer], zrow])
        up_w = jnp.zeros((LORA_PAD, 3 * RWKV_WIDTH), F32)
        up_w = up_w.at[:W_LORA, :RWKV_WIDTH].set(rwkv_w_up[layer])
        up_w = up_w.at[W_LORA:W_LORA + A_LORA, RWKV_WIDTH:2 * RWKV_WIDTH].set(rwkv_a_up[layer])
        up_w = up_w.at[W_LORA + A_LORA:LORA, 2 * RWKV_WIDTH:].set(rwkv_g_up[layer])
        rwkv_o = _rwkv_mix(p, prm, up_w.astype(BF16), masks, bsz, seq)

        w_o = w_out[layer].astype(BF16)
        w_rt = jnp.zeros((d, ROUTE_LANES), F32)
        w_rt = w_rt.at[:, :N_GROUPS].set(moe_w_group[layer])
        w_rt = w_rt.at[:, N_GROUPS:N_GROUPS + N_EXPERTS].set(moe_w_router[layer])
        w_rt_hi = w_rt.astype(BF16)
        w_rt = jnp.stack([w_rt_hi, (w_rt - w_rt_hi.astype(F32)).astype(BF16)])
        b_rt = jnp.zeros((1, ROUTE_LANES), F32)
        b_rt = b_rt.at[0, :N_GROUPS].set(moe_b_group[layer])
        b_rt = b_rt.at[0, N_GROUPS:N_GROUPS + N_EXPERTS].set(moe_b_router[layer])
        x1, h2, rt = _out_projection(attn_o, rwkv_o, x2, w_o[:ATTN_WIDTH], w_o[ATTN_WIDTH:], mod3,
                                     norm2_g[layer].reshape(1, d), w_rt, b_rt, layer, bsz, seq)

        dest, block_e, rows = _routing_tables(rt, tok)
        dest3 = dest.reshape(tok // tt, 1, TOP_K * tt)
        xs = _dispatch(dest3, h2, jnp.zeros((rows, PACKED), jnp.uint32))
        yb = _experts(block_e, xs, moe_w_gate.reshape(-1, d, D_EXPERT),
                      moe_w_up.reshape(-1, d, D_EXPERT), moe_w_down.reshape(-1, D_EXPERT, d), layer)
        x2 = _combine(dest3, yb, rt, x1, mod3, final_g.reshape(1, d), layer, bsz, seq,
                      final_norm=(layer == depth - 1))

    return x2.reshape(bsz, seq, d)
```

```python
import functools
import math

import jax
import jax.numpy as jnp
from jax import lax
from jax.experimental import pallas as pl
from jax.experimental.pallas import tpu as pltpu

F32 = jnp.float32
BF16 = jnp.bfloat16

D_MODEL = 1024
ATTN_WIDTH = 512
ATTN_HEADS = 4
HEAD_DIM = 64
ROT_DIM = 16
ROPE_THETA = 500000.0
SUBLN_EPS = 1e-5
ATTN_CHUNK = 64

RWKV_WIDTH = 512
RWKV_HEAD = 64
W_LORA, A_LORA, G_LORA = 32, 32, 96
LORA = W_LORA + A_LORA + G_LORA
LORA_PAD = 256
GN_EPS = 64e-5
RWKV_COLS = 3 * RWKV_WIDTH + LORA_PAD
QKV_COLS = 3 * ATTN_WIDTH
WKV_CHUNK = 64
WKV_GROUP = 8
WKV_STEP = 256
SLAB = 256

N_GROUPS = 4
EXPERTS_PER_GROUP = 8
N_EXPERTS = 32
D_EXPERT = 256
TOP_K = 2
ROUTE_LANES = 128
MOE_ROWS = 512
DMA_UNROLL = 8
COUNT_TILE = 256

NORM_EPS = 1e-6
NEG_BIG = -1e30
VMEM_LIMIT = 56 * 1024 * 1024


def _cparams(sem):
    return pltpu.CompilerParams(dimension_semantics=sem, vmem_limit_bytes=VMEM_LIMIT)


def _mm(a, b):
    return jnp.dot(a.astype(BF16), b.astype(BF16), preferred_element_type=F32)


def _mm_nt(a, b):
    return lax.dot_general(a.astype(BF16), b.astype(BF16), (((1,), (1,)), ((), ())),
                           preferred_element_type=F32)


def _mm_tn(a, b):
    return lax.dot_general(a.astype(BF16), b.astype(BF16), (((0,), (0,)), ((), ())),
                           preferred_element_type=F32)


def _mm_split(m_exact, x):
    hi = x.astype(BF16)
    lo = (x - hi.astype(F32)).astype(BF16)
    return (jnp.dot(m_exact, hi, preferred_element_type=F32)
            + jnp.dot(m_exact, lo, preferred_element_type=F32))


def _split_mm(x, m_exact):
    hi = x.astype(BF16)
    lo = (x - hi.astype(F32)).astype(BF16)
    return (jnp.dot(hi, m_exact, preferred_element_type=F32)
            + jnp.dot(lo, m_exact, preferred_element_type=F32))


def _sigmoid(x):
    return 1.0 / (1.0 + jnp.exp(-x))


PACKED = D_MODEL // 2


def _pack_rows(x):
    bits = pltpu.bitcast(x.astype(BF16).astype(F32), jnp.uint32)
    return (bits[:, :PACKED] >> 16) | (bits[:, PACKED:] & jnp.uint32(0xFFFF0000))


def _unpack_rows(w):
    lo = pltpu.bitcast(w << 16, F32)
    hi = pltpu.bitcast(w & jnp.uint32(0xFFFF0000), F32)
    return jnp.concatenate([lo, hi], axis=1)


def _mod_kernel(c_ref, w_ref, b_ref, o_ref):
    c = c_ref[...]
    ca = c * _sigmoid(c)
    o_ref[0] = jnp.dot(ca, w_ref[0], preferred_element_type=F32,
                       precision=lax.Precision.HIGHEST) + b_ref[0]


def _modulation(c, ada_w, ada_b):
    depth, d, n = ada_w.shape
    bsz = c.shape[0]
    nb = n // d
    return pl.pallas_call(
        _mod_kernel,
        grid=(depth, nb),
        in_specs=[pl.BlockSpec((bsz, d), lambda l, j: (0, 0)),
                  pl.BlockSpec((1, d, d), lambda l, j: (l, 0, j)),
                  pl.BlockSpec((1, 1, d), lambda l, j: (l, 0, j))],
        out_specs=pl.BlockSpec((1, bsz, d), lambda l, j: (l, 0, j)),
        out_shape=jax.ShapeDtypeStruct((depth, bsz, n), F32),
        compiler_params=_cparams(("arbitrary", "arbitrary")),
        name="adaln_mod",
    )(c, ada_w, ada_b.reshape(depth, 1, n))


def _inproj_kernel(x_ref, g_ref, sc_ref, sh_ref, w_ref, mu_ref, cos_ref, s1_ref, s2_ref,
                   qa_ref, qb_ref, k_ref, vt_ref, p_ref, carry_ref):
    i = pl.program_id(1)
    tm = x_ref.shape[0]
    x = x_ref[...]
    ms = jnp.mean(x * x, axis=-1, keepdims=True)
    h = x * lax.rsqrt(ms + NORM_EPS) * g_ref[...]
    h = h * (1.0 + sc_ref[0]) + sh_ref[0]
    hb = h.astype(BF16)

    cosv, s1v, s2v = cos_ref[...], s1_ref[...], s2_ref[...]

    def rot128(t):
        return (t * cosv + pltpu.roll(t, ROT_DIM // 2, 1) * s1v
                + pltpu.roll(t, 128 - ROT_DIM // 2, 1) * s2v)

    lane = lax.broadcasted_iota(jnp.int32, (1, 128), 1)
    first_map = lane < HEAD_DIM
    scale = HEAD_DIM ** -0.5 * LOG2E
    qkv = jnp.dot(hb, w_ref[:, :QKV_COLS], preferred_element_type=F32)
    for hd in range(ATTN_HEADS):
        lo, hi = hd * 128, (hd + 1) * 128
        q = rot128(qkv[:, lo:hi]) * scale
        qa_ref[:, lo:hi] = jnp.where(first_map, q, 0.0).astype(BF16)
        qb_ref[:, lo:hi] = jnp.where(first_map, 0.0, q).astype(BF16)
        k_ref[:, lo:hi] = rot128(qkv[:, ATTN_WIDTH + lo:ATTN_WIDTH + hi]).astype(BF16)
    vt_ref[...] = qkv[:, 2 * ATTN_WIDTH:].T.astype(BF16)

    @pl.when(i == 0)
    def _():
        carry_ref[...] = jnp.zeros_like(carry_ref)

    p = jnp.dot(hb, w_ref[:, QKV_COLS:], preferred_element_type=F32)
    row = lax.broadcasted_iota(jnp.int32, p.shape, 0)
    prev = jnp.where(row == 0, carry_ref[...], pltpu.roll(p, 1, 0))
    carry_ref[...] = p[tm - 1:tm, :]
    p_ref[...] = (p + (prev - p) * mu_ref[...]).astype(BF16)


def _in_projection(x2, g1, mod3, layer, bsz, seq, w_pad, mu_pad, cos_t, s1_t, s2_t):
    tm = min(512, seq)
    nt = seq // tm
    tok = x2.shape[0]
    row = lambda b, i: (b * nt + i, 0)
    const = lambda b, i: (0, 0)
    wide = w_pad.shape[1]
    outs = pl.pallas_call(
        _inproj_kernel,
        grid=(bsz, nt),
        in_specs=[pl.BlockSpec((tm, D_MODEL), row),
                  pl.BlockSpec((1, D_MODEL), const),
                  pl.BlockSpec((1, 1, D_MODEL), lambda b, i: (layer * bsz + b, 0, 1)),
                  pl.BlockSpec((1, 1, D_MODEL), lambda b, i: (layer * bsz + b, 0, 0)),
                  pl.BlockSpec((D_MODEL, wide), const),
                  pl.BlockSpec((1, RWKV_COLS), const),
                  pl.BlockSpec((tm, 128), row),
                  pl.BlockSpec((tm, 128), row),
                  pl.BlockSpec((tm, 128), row)],
        out_specs=[pl.BlockSpec((tm, ATTN_WIDTH), row)] * 3
        + [pl.BlockSpec((ATTN_WIDTH, tm), lambda b, i: (b, i)), pl.BlockSpec((tm, RWKV_COLS), row)],
        out_shape=[jax.ShapeDtypeStruct((tok, ATTN_WIDTH), BF16)] * 3
        + [jax.ShapeDtypeStruct((bsz * ATTN_WIDTH, seq), BF16),
           jax.ShapeDtypeStruct((tok, RWKV_COLS), BF16)],
        scratch_shapes=[pltpu.VMEM((1, RWKV_COLS), F32)],
        compiler_params=_cparams(("arbitrary", "arbitrary")),
        name="in_proj",
    )(x2, g1, mod3, mod3, w_pad, mu_pad, cos_t, s1_t, s2_t)
    return outs


ATTN_TILE = 1024
ATTN_QLANES = 256
ATTN_KEYS = 512
ATTN_AHEAD = 4
LOG2E = 1.4426950408889634


def _attn_kernel(qi_ref, kj_ref, qa_ref, qb_ref, k_ref, vt_ref, lam_ref, g_ref, o_ref,
                 m_ref, l_ref, acc_ref, *, lambda_init):
    step = pl.program_id(1)
    i = qi_ref[step]
    j = kj_ref[step]
    tq = qa_ref.shape[0]
    tk = k_ref.shape[0]
    hq = min(ATTN_QLANES, tq)
    n_half = tq // hq

    @pl.when(j == 0)
    def _():
        m_ref[...] = jnp.full(m_ref.shape, NEG_BIG, F32)
        l_ref[...] = jnp.zeros_like(l_ref)
        acc_ref[...] = jnp.zeros_like(acc_ref)

    kb = min(ATTN_KEYS, tk)

    def accumulate(diagonal):
        def visible(half, k0, kn):
            kc = (lax.broadcasted_iota(jnp.int32, (kn, hq), 0) + k0) // ATTN_CHUNK
            qc = (lax.broadcasted_iota(jnp.int32, (kn, hq), 1) + half * hq) // ATTN_CHUNK
            return kc <= qc

        spans = []
        for half in range(n_half):
            if not diagonal:
                spans += [(half, k0, kb, False) for k0 in range(0, tk, kb)]
            else:
                if half > 0:
                    spans.append((half, 0, half * hq, False))
                spans.append((half, half * hq, hq, True))
        chains = [(half, k0, kn, masked, hd, mp) for (half, k0, kn, masked) in spans
                  for hd in range(ATTN_HEADS) for mp in range(2)]

        def scores(chain):
            half, k0, kn, _, hd, mp = chain
            q_ref = (qa_ref, qb_ref)[mp]
            return lax.dot_general(k_ref[k0:k0 + kn, hd * 128:(hd + 1) * 128],
                                   q_ref[half * hq:(half + 1) * hq, hd * 128:(hd + 1) * 128],
                                   (((1,), (1,)), ((), ())), preferred_element_type=F32)

        ahead = [scores(c) for c in chains[:ATTN_AHEAD]]
        for n, (half, k0, kn, masked, hd, mp) in enumerate(chains):
            s = ahead.pop(0)
            if n + ATTN_AHEAD < len(chains):
                ahead.append(scores(chains[n + ATTN_AHEAD]))
            idx = (2 * hd + mp) * n_half + half
            if masked:
                s = jnp.where(visible(half, k0, kn), s, NEG_BIG)
            m_old = m_ref[idx]
            m_new = jnp.maximum(m_old, jnp.max(s, axis=0, keepdims=True))
            alpha = jnp.exp2(m_old - m_new)
            p = jnp.exp2(s - m_new)
            l_ref[idx] = alpha * l_ref[idx] + jnp.sum(p, axis=0, keepdims=True)
            acc_ref[idx] = alpha * acc_ref[idx] + jnp.dot(
                vt_ref[hd * 128:(hd + 1) * 128, k0:k0 + kn], p.astype(BF16),
                preferred_element_type=F32)
            m_ref[idx] = m_new

    @pl.when(j < i)
    def _():
        accumulate(False)

    @pl.when(j == i)
    def _():
        accumulate(True)
        lv = lam_ref[...]
        lam = (jnp.exp(jnp.sum(lv[0:1] * lv[1:2], axis=-1, keepdims=True))
               - jnp.exp(jnp.sum(lv[2:3] * lv[3:4], axis=-1, keepdims=True)) + lambda_init)
        gcol = jnp.concatenate([g_ref[...]] * (hq // 128), axis=1)
        for hd in range(ATTN_HEADS):
            for half in range(n_half):
                a = (2 * hd) * n_half + half
                b = (2 * hd + 1) * n_half + half
                o = acc_ref[a] / l_ref[a] - lam * (acc_ref[b] / l_ref[b])
                ms = jnp.mean(o * o, axis=0, keepdims=True)
                o = o * lax.rsqrt(ms + SUBLN_EPS) * gcol * (1.0 - lambda_init)
                o_ref[half * hq:(half + 1) * hq, hd * 128:(hd + 1) * 128] = o.T.astype(BF16)


def _diff_attention(qa, qb, k, vt, lam_vecs, subln_g, lambda_init, bsz, seq):
    tq = min(ATTN_TILE, seq)
    nq = seq // tq
    tok = qa.shape[0]
    hq = min(ATTN_QLANES, tq)
    n_stat = 2 * ATTN_HEADS * (tq // hq)
    pairs = [(i, j) for i in range(nq) for j in range(i + 1)]
    qi = jnp.asarray([p[0] for p in pairs], jnp.int32)
    kj = jnp.asarray([p[1] for p in pairs], jnp.int32)
    qmap = lambda b, s, qi, kj: (b * nq + qi[s], 0)
    kmap = lambda b, s, qi, kj: (b * nq + kj[s], 0)
    vmap = lambda b, s, qi, kj: (b, kj[s])
    const = lambda b, s, qi, kj: (0, 0)
    g_col = jnp.broadcast_to(subln_g.reshape(2 * HEAD_DIM, 1), (2 * HEAD_DIM, 128))
    grid_spec = pltpu.PrefetchScalarGridSpec(
        num_scalar_prefetch=2,
        grid=(bsz, len(pairs)),
        in_specs=[pl.BlockSpec((tq, ATTN_WIDTH), qmap),
                  pl.BlockSpec((tq, ATTN_WIDTH), qmap),
                  pl.BlockSpec((tq, ATTN_WIDTH), kmap),
                  pl.BlockSpec((ATTN_WIDTH, tq), vmap),
                  pl.BlockSpec((4, HEAD_DIM), const),
                  pl.BlockSpec((2 * HEAD_DIM, 128), const)],
        out_specs=pl.BlockSpec((tq, ATTN_WIDTH), qmap),
        scratch_shapes=[pltpu.VMEM((n_stat, 1, hq), F32),
                        pltpu.VMEM((n_stat, 1, hq), F32),
                        pltpu.VMEM((n_stat, 2 * HEAD_DIM, hq), F32)],
    )
    return pl.pallas_call(
        functools.partial(_attn_kernel, lambda_init=lambda_init),
        grid_spec=grid_spec,
        out_shape=jax.ShapeDtypeStruct((tok, ATTN_WIDTH), BF16),
        compiler_params=_cparams(("arbitrary", "arbitrary")),
        name="diff_attn",
    )(qi, kj, qa, qb, k, vt, lam_vecs, g_col)


def _softplus(z):
    return jnp.maximum(z, 0.0) + jnp.log(1.0 + jnp.exp(-jnp.abs(z)))


def _rwkv_kernel(p_ref, prm_ref, up_ref, tri_ref, bd_ref, sl_ref, il_ref, eye_ref,
                 o_ref, h_ref):
    t = pl.program_id(1)
    C = WKV_CHUNK
    n_chunks = p_ref.shape[0] // C
    W = RWKV_WIDTH

    @pl.when(t == 0)
    def _():
        h_ref[...] = jnp.zeros_like(h_ref)

    p = p_ref[...].astype(F32)
    r, k, v, lo = p[:, :W], p[:, W:2 * W], p[:, 2 * W:3 * W], p[:, 3 * W:]
    lane = lax.broadcasted_iota(jnp.int32, lo.shape, 1)
    act = jnp.where(lane < W_LORA, jnp.tanh(lo), jnp.where(lane < W_LORA + A_LORA, lo, _sigmoid(lo)))
    up = _mm(act, up_ref[...])
    prm = prm_ref[...]
    w0, a0, k_k, k_a, r_k, ln_g, ln_b = (prm[n:n + 1] for n in range(7))
    w_raw = -_softplus(-(w0 + up[:, :W])) - 0.5
    logd = -jnp.exp(w_raw)
    a_sig = _sigmoid(a0 + up[:, W:2 * W])
    gate = up[:, 2 * W:]

    bd = bd_ref[...]
    bd_lo = bd.astype(BF16)

    def head_sum(z):
        return jnp.concatenate([_mm(z[:, g * SLAB:(g + 1) * SLAB], bd_lo)
                                for g in range(W // SLAB)], axis=1)

    kkr = k * k_k
    kk = kkr / jnp.maximum(jnp.sqrt(head_sum(kkr * kkr)), 1e-12)
    kf = k * (1.0 + (a_sig - 1.0) * k_a)
    bonus = head_sum(r * kf * r_k) * v
    a_vec = -kk
    b_vec = kk * a_sig

    cum = _mm_split(tri_ref[...], logd)
    cum_end = jnp.concatenate(
        [jnp.broadcast_to(cum[(ch + 1) * C - 1:(ch + 1) * C, :], (C, W)) for ch in range(n_chunks)],
        axis=0)
    d_in = jnp.exp(cum)
    d_ex = jnp.exp(cum - logd)
    d_inv = jnp.exp(-cum)
    d_end = jnp.exp(cum_end)
    d_rest = jnp.exp(cum_end - cum)

    at, rt = a_vec * d_ex, r * d_in
    bt, kt = b_vec * d_inv, kf * d_inv
    be, ke = b_vec * d_rest, kf * d_rest

    sl = sl_ref[...]
    il = il_ref[...]
    eye_c = il - sl
    eye_s = eye_ref[...]
    heads = SLAB // RWKV_HEAD
    n_slabs = W // SLAB
    all_probs = [(ch, g) for ch in range(n_chunks) for g in range(n_slabs)]

    def cut(z, ch, g):
        return z[ch * C:(ch + 1) * C, g * SLAB:(g + 1) * SLAB]

    def stack(zz):
        return jnp.concatenate([zz.astype(BF16)] * heads, axis=0) * bd_lo

    def solve(probs):
        ats = [cut(at, *pr) for pr in probs]
        rts = [cut(rt, *pr) for pr in probs]
        vs = [cut(v, *pr) for pr in probs]
        btm = [stack(cut(bt, *pr)) for pr in probs]
        ktm = [stack(cut(kt, *pr)) for pr in probs]
        vm = [stack(v_) for v_ in vs]
        l_ab = [sl * _mm_nt(a_, b_) for a_, b_ in zip(ats, btm)]
        l_ak = [sl * _mm_nt(a_, k_) for a_, k_ in zip(ats, ktm)]
        lpm = [stack(l_) for l_ in l_ab]
        lp = [_mm(l_, m_) for l_, m_ in zip(l_ab, lpm)]
        inv = [eye_c + l_ for l_ in l_ab]
        wv = [_mm(l_, v_) for l_, v_ in zip(l_ak, vm)]
        m_rb = [il * _mm_nt(r_, b_) for r_, b_ in zip(rts, btm)]
        m_rk = [il * _mm_nt(r_, k_) for r_, k_ in zip(rts, ktm)]
        levels = int(math.log2(C)) - 1
        for lev in range(levels):
            lpm = [stack(l_) for l_ in lp]
            inv_new = [i_ + _mm(i_, m_) for i_, m_ in zip(inv, lpm)]
            if lev + 1 < levels:
                lp = [_mm(l_, m_) for l_, m_ in zip(lp, lpm)]
            inv = inv_new
        abar = [_mm(i_, stack(a_)) for i_, a_ in zip(inv, ats)]
        vbar = [_mm(i_, stack(w_)) for i_, w_ in zip(inv, wv)]
        abm = [stack(a_) for a_ in abar]
        vbm = [stack(v_) for v_ in vbar]
        rhat = [r_ + _mm(m_, a_) for r_, m_, a_ in zip(rts, m_rb, abm)]
        yhat = [_mm(mb_, vb_) + _mm(mk_, v_) for mb_, vb_, mk_, v_ in zip(m_rb, vbm, m_rk, vm)]
        bes = [cut(be, *pr) for pr in probs]
        kes = [cut(ke, *pr) for pr in probs]
        gmat = [eye_s * cut(d_end, *pr)[0:1] + bd * _mm_tn(b_, a_)
                for pr, b_, a_ in zip(probs, bes, abar)]
        fmat = [bd * _mm_tn(jnp.concatenate([b_, k_], axis=0), jnp.concatenate([vb_, v_], axis=0))
                for b_, k_, vb_, v_ in zip(bes, kes, vbar, vs)]
        return rhat, yhat, gmat, fmat

    rhat, yhat, gmat, fmat = [], [], [], []
    for first in range(0, len(all_probs), WKV_GROUP):
        for total, part in zip((rhat, yhat, gmat, fmat), solve(all_probs[first:first + WKV_GROUP])):
            total += part

    hs = [h_ref[g] for g in range(n_slabs)]
    rows = []
    for ch in range(n_chunks):
        ys = []
        for g in range(n_slabs):
            n = ch * n_slabs + g
            ys.append(_mm(rhat[n], hs[g]) + yhat[n])
            hs[g] = _mm(gmat[n], hs[g]) + fmat[n]
        rows.append(jnp.concatenate(ys, axis=1))
    for g in range(n_slabs):
        h_ref[g] = hs[g]
    y = jnp.concatenate(rows, axis=0) if n_chunks > 1 else rows[0]

    inv_n = 1.0 / RWKV_HEAD
    mu = head_sum(y) * inv_n
    yc = y - mu
    var = head_sum(yc * yc) * inv_n
    yn = yc * lax.rsqrt(var + GN_EPS) * ln_g + ln_b
    o_ref[...] = ((yn + bonus) * gate).astype(BF16)


def _block_masks():
    n = SLAB
    r = jnp.arange(n)[:, None]
    c = jnp.arange(n)[None, :]
    same = (r // RWKV_HEAD) == (c // RWKV_HEAD)
    bd = same.astype(F32)
    eye = (r == c).astype(F32)
    t = jnp.arange(WKV_CHUNK)[:, None]
    s_in_head = c % RWKV_HEAD
    sl = (s_in_head < t).astype(F32)
    il = (s_in_head <= t).astype(F32)
    tt = jnp.arange(WKV_STEP)
    tri = ((tt[None, :] <= tt[:, None])
           & (tt[None, :] // WKV_CHUNK == tt[:, None] // WKV_CHUNK)).astype(BF16)
    return tri, bd, sl, il, eye


def _rwkv_mix(p, prm, up_w, masks, bsz, seq):
    C = WKV_STEP
    nc = seq // C
    tok = p.shape[0]
    tri, bd, sl, il, eye = masks
    row = lambda b, t: (b * nc + t, 0)
    const = lambda b, t: (0, 0)
    return pl.pallas_call(
        _rwkv_kernel,
        grid=(bsz, nc),
        in_specs=[pl.BlockSpec((C, RWKV_COLS), row),
                  pl.BlockSpec((8, RWKV_WIDTH), const),
                  pl.BlockSpec((LORA_PAD, 3 * RWKV_WIDTH), const),
                  pl.BlockSpec((C, C), const),
                  pl.BlockSpec((SLAB, SLAB), const),
                  pl.BlockSpec((WKV_CHUNK, SLAB), const),
                  pl.BlockSpec((WKV_CHUNK, SLAB), const),
                  pl.BlockSpec((SLAB, SLAB), const)],
        out_specs=pl.BlockSpec((C, RWKV_WIDTH), row),
        out_shape=jax.ShapeDtypeStruct((tok, RWKV_WIDTH), BF16),
        scratch_shapes=[pltpu.VMEM((RWKV_WIDTH // SLAB, SLAB, SLAB), F32)],
        compiler_params=_cparams(("arbitrary", "arbitrary")),
        name="rwkv7_mix",
    )(p, prm, up_w, tri, bd, sl, il, eye)


def _out_kernel(a_ref, r_ref, x_ref, wa_ref, wr_ref, gt_ref, g2_ref, sc_ref, sh_ref, wrt_ref,
                brt_ref, x1_ref, h2_ref, rt_ref):
    mixed = (jnp.dot(a_ref[...], wa_ref[...], preferred_element_type=F32)
             + jnp.dot(r_ref[...], wr_ref[...], preferred_element_type=F32))
    x1 = x_ref[...] + gt_ref[0] * mixed
    x1_ref[...] = x1
    ms = jnp.mean(x1 * x1, axis=-1, keepdims=True)
    h2 = x1 * lax.rsqrt(ms + NORM_EPS) * g2_ref[...]
    h2 = h2 * (1.0 + sc_ref[0]) + sh_ref[0]
    h2_ref[...] = _pack_rows(h2)
    h_hi = h2.astype(BF16)
    h_lo = (h2 - h_hi.astype(F32)).astype(BF16)
    logits = (jnp.dot(h_hi, wrt_ref[0], preferred_element_type=F32)
              + jnp.dot(h_lo, wrt_ref[0], preferred_element_type=F32)
              + jnp.dot(h_hi, wrt_ref[1], preferred_element_type=F32)) + brt_ref[...]

    lane_i = lax.broadcasted_iota(jnp.int32, logits.shape, 1)
    lane = lane_i.astype(F32)
    far = float(ROUTE_LANES)
    is_g = lane_i < N_GROUPS
    gmax = jnp.max(jnp.where(is_g, logits, NEG_BIG), axis=-1, keepdims=True)
    gsum = jnp.sum(jnp.where(is_g, jnp.exp(logits - gmax), 0.0), axis=-1, keepdims=True)
    g_top = 1.0 / gsum
    g_idx = jnp.min(jnp.where(is_g & (logits == gmax), lane, far), axis=-1, keepdims=True)
    base = N_GROUPS + g_idx * EXPERTS_PER_GROUP
    in_g = (lane >= base) & (lane < base + EXPERTS_PER_GROUP)
    v1 = jnp.max(jnp.where(in_g, logits, NEG_BIG), axis=-1, keepdims=True)
    i1 = jnp.min(jnp.where(in_g & (logits == v1), lane, far), axis=-1, keepdims=True)
    rest = in_g & (lane != i1)
    v2 = jnp.max(jnp.where(rest, logits, NEG_BIG), axis=-1, keepdims=True)
    i2 = jnp.min(jnp.where(rest & (logits == v2), lane, far), axis=-1, keepdims=True)
    e21 = jnp.exp(v2 - v1)
    w1 = g_top / (1.0 + e21)
    w2 = g_top * e21 / (1.0 + e21)
    rt_ref[...] = jnp.where(lane_i == 0, i1 - N_GROUPS,
                            jnp.where(lane_i == 1, i2 - N_GROUPS,
                                      jnp.where(lane_i == 2, w1,
                                                jnp.where(lane_i == 3, w2, 0.0))))


def _out_projection(attn_o, rwkv_o, x2, w_a, w_r, mod3, g2, w_rt, b_rt, layer, bsz, seq):
    tm = min(512, seq)
    nt = seq // tm
    tok = x2.shape[0]
    row = lambda b, i: (b * nt + i, 0)
    const = lambda b, i: (0, 0)
    modspec = lambda col: pl.BlockSpec((1, 1, D_MODEL), lambda b, i: (layer * bsz + b, 0, col))
    return pl.pallas_call(
        _out_kernel,
        grid=(bsz, nt),
        in_specs=[pl.BlockSpec((tm, ATTN_WIDTH), row),
                  pl.BlockSpec((tm, RWKV_WIDTH), row),
                  pl.BlockSpec((tm, D_MODEL), row),
                  pl.BlockSpec((ATTN_WIDTH, D_MODEL), const),
                  pl.BlockSpec((RWKV_WIDTH, D_MODEL), const),
                  modspec(2),
                  pl.BlockSpec((1, D_MODEL), const),
                  modspec(4),
                  modspec(3),
                  pl.BlockSpec((2, D_MODEL, ROUTE_LANES), lambda b, i: (0, 0, 0)),
                  pl.BlockSpec((1, ROUTE_LANES), const)],
        out_specs=[pl.BlockSpec((tm, D_MODEL), row),
                   pl.BlockSpec((tm, PACKED), row),
                   pl.BlockSpec((tm, ROUTE_LANES), row)],
        out_shape=[jax.ShapeDtypeStruct((tok, D_MODEL), F32),
                   jax.ShapeDtypeStruct((tok, PACKED), jnp.uint32),
                   jax.ShapeDtypeStruct((tok, ROUTE_LANES), F32)],
        compiler_params=_cparams(("arbitrary", "arbitrary")),
        name="out_proj_router",
    )(attn_o, rwkv_o, x2, w_a, w_r, mod3, g2, mod3, mod3, w_rt, b_rt)


def _dispatch_kernel(dest_ref, h_ref, xin_ref, xs_ref, sem):
    del xin_ref
    tt = h_ref.shape[0]

    def issue(g, c):
        base = pl.multiple_of(g * DMA_UNROLL, DMA_UNROLL)
        for u in range(DMA_UNROLL):
            for kk in range(TOP_K):
                d = dest_ref[0, 0, TOP_K * base + (TOP_K * u + kk)]
                pltpu.make_async_copy(h_ref.at[pl.ds(base + u, 1)], xs_ref.at[pl.ds(d, 1)],
                                      sem).start(priority=kk)
        return c

    lax.fori_loop(0, tt // DMA_UNROLL, issue, 0)
    for kk in range(TOP_K):
        pltpu.make_async_copy(h_ref, xs_ref.at[pl.ds(0, tt)], sem).wait()


def _dispatch(dest3, h2, xs_init):
    tok = h2.shape[0]
    nt, _, two_tt = dest3.shape
    tt = two_tt // TOP_K
    return pl.pallas_call(
        _dispatch_kernel,
        grid=(nt,),
        in_specs=[pl.BlockSpec((1, 1, two_tt), lambda i: (i, 0, 0), memory_space=pltpu.SMEM),
                  pl.BlockSpec((tt, PACKED), lambda i: (i, 0)),
                  pl.BlockSpec(memory_space=pl.ANY)],
        out_specs=pl.BlockSpec(memory_space=pl.ANY),
        out_shape=jax.ShapeDtypeStruct(xs_init.shape, jnp.uint32),
        scratch_shapes=[pltpu.SemaphoreType.DMA],
        input_output_aliases={2: 0},
        compiler_params=_cparams(("arbitrary",)),
        name="moe_dispatch",
    )(dest3, h2, xs_init)


def _expert_kernel(be_ref, x_ref, wg_ref, wu_ref, wd_ref, y_ref, wg_lo, wu_lo, wd_lo):
    i = pl.program_id(0)

    @pl.when((i == 0) | (be_ref[i] != be_ref[jnp.maximum(i - 1, 0)]))
    def _():
        wg_lo[...] = wg_ref[0].astype(BF16)
        wu_lo[...] = wu_ref[0].astype(BF16)
        wd_lo[...] = wd_ref[0].astype(BF16)

    xb = _unpack_rows(x_ref[...]).astype(BF16)
    gate = jnp.dot(xb, wg_lo[...], preferred_element_type=F32)
    up = jnp.dot(xb, wu_lo[...], preferred_element_type=F32)
    hid = gate * _sigmoid(gate) * up
    y_ref[...] = _pack_rows(jnp.dot(hid.astype(BF16), wd_lo[...], preferred_element_type=F32))


def _experts(block_e, xs, w_gate, w_up, w_down, layer):
    rows = xs.shape[0]
    nb = rows // MOE_ROWS
    grid_spec = pltpu.PrefetchScalarGridSpec(
        num_scalar_prefetch=1,
        grid=(nb,),
        in_specs=[pl.BlockSpec((MOE_ROWS, PACKED), lambda i, be: (i, 0)),
                  pl.BlockSpec((1, D_MODEL, D_EXPERT), lambda i, be: (layer * N_EXPERTS + be[i], 0, 0)),
                  pl.BlockSpec((1, D_MODEL, D_EXPERT), lambda i, be: (layer * N_EXPERTS + be[i], 0, 0)),
                  pl.BlockSpec((1, D_EXPERT, D_MODEL), lambda i, be: (layer * N_EXPERTS + be[i], 0, 0))],
        out_specs=pl.BlockSpec((MOE_ROWS, PACKED), lambda i, be: (i, 0)),
        scratch_shapes=[pltpu.VMEM((D_MODEL, D_EXPERT), BF16),
                        pltpu.VMEM((D_MODEL, D_EXPERT), BF16),
                        pltpu.VMEM((D_EXPERT, D_MODEL), BF16)],
    )
    return pl.pallas_call(
        _expert_kernel,
        grid_spec=grid_spec,
        out_shape=jax.ShapeDtypeStruct((rows, PACKED), jnp.uint32),
        compiler_params=_cparams(("arbitrary",)),
        name="moe_experts",
    )(block_e, xs, w_gate, w_up, w_down)


def _combine_kernel(dest_ref, next_ref, yb_ref, rt_ref, x_ref, gt_ref, fg_ref, o_ref, buf_ref, sems,
                    *, final_norm):
    tt = x_ref.shape[0]
    i = pl.program_id(0)
    slot = i % 2

    def gather(table_ref, into):
        def issue(g, c):
            base = pl.multiple_of(g * DMA_UNROLL, DMA_UNROLL)
            for u in range(DMA_UNROLL):
                for kk in range(TOP_K):
                    d = table_ref[0, 0, TOP_K * base + (TOP_K * u + kk)]
                    pltpu.make_async_copy(yb_ref.at[pl.ds(d, 1)],
                                          buf_ref.at[into, kk, pl.ds(base + u, 1)],
                                          sems.at[into]).start(priority=kk)
            return c

        lax.fori_loop(0, tt // DMA_UNROLL, issue, 0)

    @pl.when(i == 0)
    def _():
        gather(dest_ref, 0)

    @pl.when(i + 1 < pl.num_programs(0))
    def _():
        gather(next_ref, 1 - slot)

    for kk in range(TOP_K):
        pltpu.make_async_copy(yb_ref.at[pl.ds(0, tt)], buf_ref.at[slot, kk], sems.at[slot]).wait()
    rt = rt_ref[...]
    y = (_unpack_rows(buf_ref[slot, 0]) * rt[:, 2:3]
         + _unpack_rows(buf_ref[slot, 1]) * rt[:, 3:4])
    x = x_ref[...] + gt_ref[0] * y
    if final_norm:
        ms = jnp.mean(x * x, axis=-1, keepdims=True)
        x = x * lax.rsqrt(ms + NORM_EPS) * fg_ref[...]
    o_ref[...] = x


def _combine(dest3, yb, rt, x1, mod3, final_g, layer, bsz, seq, final_norm):
    tok = x1.shape[0]
    nt, _, two_tt = dest3.shape
    tt = two_tt // TOP_K
    per_seq = seq // tt
    return pl.pallas_call(
        functools.partial(_combine_kernel, final_norm=final_norm),
        grid=(nt,),
        in_specs=[pl.BlockSpec((1, 1, two_tt), lambda i: (i, 0, 0), memory_space=pltpu.SMEM),
                  pl.BlockSpec((1, 1, two_tt), lambda i: (jnp.minimum(i + 1, nt - 1), 0, 0),
                               memory_space=pltpu.SMEM),
                  pl.BlockSpec(memory_space=pl.ANY),
                  pl.BlockSpec((tt, ROUTE_LANES), lambda i: (i, 0)),
                  pl.BlockSpec((tt, D_MODEL), lambda i: (i, 0)),
                  pl.BlockSpec((1, 1, D_MODEL), lambda i: (layer * bsz + i // per_seq, 0, 5)),
                  pl.BlockSpec((1, D_MODEL), lambda i: (0, 0))],
        out_specs=pl.BlockSpec((tt, D_MODEL), lambda i: (i, 0)),
        out_shape=jax.ShapeDtypeStruct((tok, D_MODEL), F32),
        scratch_shapes=[pltpu.VMEM((2, TOP_K, tt, PACKED), jnp.uint32),
                        pltpu.SemaphoreType.DMA((2,))],
        compiler_params=_cparams(("arbitrary",)),
        name="moe_combine",
    )(dest3, dest3, yb, rt, x1, mod3, final_g)


def _routing_tables(rt, tok):
    m = tok * TOP_K
    flat_e = rt[:, :TOP_K].astype(jnp.int32).reshape(m)
    onehot = (flat_e[:, None] == jnp.arange(N_EXPERTS, dtype=jnp.int32)[None, :]).astype(jnp.int32)
    ct = min(COUNT_TILE, m)
    tri = (jnp.arange(ct)[None, :] <= jnp.arange(ct)[:, None]).astype(BF16)
    within = jnp.einsum("ts,nse->nte", tri, onehot.reshape(m // ct, ct, N_EXPERTS).astype(BF16),
                        preferred_element_type=F32)
    tile_tot = within[:, -1, :]
    tile_base = jnp.cumsum(tile_tot, axis=0) - tile_tot
    running = (within + tile_base[:, None, :]).reshape(m, N_EXPERTS).astype(jnp.int32)
    counts = running[-1]
    padded = (counts + MOE_ROWS - 1) // MOE_ROWS * MOE_ROWS
    pad_end = jnp.cumsum(padded)
    pad_start = pad_end - padded
    dest = jnp.sum(onehot * (running - 1 + pad_start[None, :]), axis=1)
    rows = -(-m // MOE_ROWS) * MOE_ROWS + N_EXPERTS * MOE_ROWS
    nb = rows // MOE_ROWS
    block_start = jnp.arange(nb, dtype=jnp.int32) * MOE_ROWS
    block_e = jnp.minimum(jnp.sum((pad_end[None, :] <= block_start[:, None]).astype(jnp.int32), axis=1),
                          N_EXPERTS - 1).astype(jnp.int32)
    return dest.astype(jnp.int32), block_e, rows


def _rotary_tables(positions):
    half = ROT_DIM // 2
    inv_freq = ROPE_THETA ** (-jnp.arange(0, ROT_DIM, 2, dtype=F32) / ROT_DIM)
    ang = positions.astype(F32).reshape(-1, 1) * inv_freq[None, :]
    cos, sin = jnp.cos(ang), jnp.sin(ang)
    dim = jnp.arange(128)[None, :] % HEAD_DIM
    freq = jnp.arange(half)[:, None]
    lo = (dim == freq).astype(F32)
    hi = (dim == freq + half).astype(F32)
    rest = (dim >= ROT_DIM).astype(F32)
    place = functools.partial(jnp.dot, precision=lax.Precision.HIGHEST)
    return place(cos, lo + hi) + rest, place(sin, hi), place(sin, -lo)


def _lambda_init(layer):
    return 0.8 - 0.6 * math.exp(-0.3 * layer)


def kernel(x, c, positions, ada_w, ada_b, norm1_g, norm2_g, w_in, w_out, attn_lambda, attn_subln_g, rwkv_shift_mu, rwkv_w0, rwkv_w_up, rwkv_a0, rwkv_a_up, rwkv_g_up, rwkv_k_k, rwkv_k_a, rwkv_r_k, rwkv_lnx_g, rwkv_lnx_b, moe_w_group, moe_b_group, moe_w_router, moe_b_router, moe_w_gate, moe_w_up, moe_w_down, final_g):
    bsz, seq, d = x.shape
    depth = ada_w.shape[0]
    tok = bsz * seq
    x2 = x.reshape(tok, d)

    mod = _modulation(c, ada_w, ada_b)
    mod3 = mod.reshape(depth * bsz, 1, 6 * d)
    cos_t, s1_t, s2_t = _rotary_tables(positions)
    masks = _block_masks()
    tt = min(512, seq)

    for layer in range(depth):
        pad = LORA_PAD - LORA
        w_pad = jnp.pad(w_in[layer], ((0, 0), (0, pad))).astype(BF16)
        mu_pad = jnp.pad(rwkv_shift_mu[layer], (0, pad)).reshape(1, RWKV_COLS)
        qa, qb, k, v, p = _in_projection(x2, norm1_g[layer].reshape(1, d), mod3, layer, bsz, seq,
                                         w_pad, mu_pad, cos_t, s1_t, s2_t)
        attn_o = _diff_attention(qa, qb, k, v, attn_lambda[layer],
                                 attn_subln_g[layer].reshape(1, 2 * HEAD_DIM),
                                 _lambda_init(layer), bsz, seq)

        zrow = jnp.zeros((RWKV_WIDTH,), F32)
        prm = jnp.stack([rwkv_w0[layer], rwkv_a0[layer], rwkv_k_k[layer], rwkv_k_a[layer],
                         rwkv_r_k[layer].reshape(RWKV_WIDTH), rwkv_lnx_g[layer],
                         rwkv_lnx_b[layer], zrow])
        up_w = jnp.zeros((LORA_PAD, 3 * RWKV_WIDTH), F32)
        up_w = up_w.at[:W_LORA, :RWKV_WIDTH].set(rwkv_w_up[layer])
        up_w = up_w.at[W_LORA:W_LORA + A_LORA, RWKV_WIDTH:2 * RWKV_WIDTH].set(rwkv_a_up[layer])
        up_w = up_w.at[W_LORA + A_LORA:LORA, 2 * RWKV_WIDTH:].set(rwkv_g_up[layer])
        rwkv_o = _rwkv_mix(p, prm, up_w.astype(BF16), masks, bsz, seq)

        w_o = w_out[layer].astype(BF16)
        w_rt = jnp.zeros((d, ROUTE_LANES), F32)
        w_rt = w_rt.at[:, :N_GROUPS].set(moe_w_group[layer])
        w_rt = w_rt.at[:, N_GROUPS:N_GROUPS + N_EXPERTS].set(moe_w_router[layer])
        w_rt_hi = w_rt.astype(BF16)
        w_rt = jnp.stack([w_rt_hi, (w_rt - w_rt_hi.astype(F32)).astype(BF16)])
        b_rt = jnp.zeros((1, ROUTE_LANES), F32)
        b_rt = b_rt.at[0, :N_GROUPS].set(moe_b_group[layer])
        b_rt = b_rt.at[0, N_GROUPS:N_GROUPS + N_EXPERTS].set(moe_b_router[layer])
        x1, h2, rt = _out_projection(attn_o, rwkv_o, x2, w_o[:ATTN_WIDTH], w_o[ATTN_WIDTH:], mod3,
                                     norm2_g[layer].reshape(1, d), w_rt, b_rt, layer, bsz, seq)

        dest, block_e, rows = _routing_tables(rt, tok)
        dest3 = dest.reshape(tok // tt, 1, TOP_K * tt)
        xs = _dispatch(dest3, h2, jnp.zeros((rows, PACKED), jnp.uint32))
        yb = _experts(block_e, xs, moe_w_gate.reshape(-1, d, D_EXPERT),
                      moe_w_up.reshape(-1, d, D_EXPERT), moe_w_down.reshape(-1, D_EXPERT, d), layer)
        x2 = _combine(dest3, yb, rt, x1, mod3, final_g.reshape(1, d), layer, bsz, seq,
                      final_norm=(layer == depth - 1))

    return x2.reshape(bsz, seq, d)
```

```python
import functools
import math

import jax
import jax.numpy as jnp
from jax import lax
from jax.experimental import pallas as pl
from jax.experimental.pallas import tpu as pltpu

F32 = jnp.float32
BF16 = jnp.bfloat16

D_MODEL = 1024
ATTN_WIDTH = 512
ATTN_HEADS = 4
HEAD_DIM = 64
ROT_DIM = 16
ROPE_THETA = 500000.0
SUBLN_EPS = 1e-5
ATTN_CHUNK = 64

RWKV_WIDTH = 512
RWKV_HEAD = 64
W_LORA, A_LORA, G_LORA = 32, 32, 96
LORA = W_LORA + A_LORA + G_LORA
LORA_PAD = 256
GN_EPS = 64e-5
RWKV_COLS = 3 * RWKV_WIDTH + LORA_PAD
QKV_COLS = 3 * ATTN_WIDTH
WKV_CHUNK = 64
WKV_GROUP = 8
WKV_STEP = 256
SLAB = 256

N_GROUPS = 4
EXPERTS_PER_GROUP = 8
N_EXPERTS = 32
D_EXPERT = 256
TOP_K = 2
ROUTE_LANES = 128
MOE_ROWS = 512
DMA_UNROLL = 8
COUNT_TILE = 256

NORM_EPS = 1e-6
NEG_BIG = -1e30
VMEM_LIMIT = 56 * 1024 * 1024


def _cparams(sem):
    return pltpu.CompilerParams(dimension_semantics=sem, vmem_limit_bytes=VMEM_LIMIT)


def _mm(a, b):
    return jnp.dot(a.astype(BF16), b.astype(BF16), preferred_element_type=F32)


def _mm_nt(a, b):
    return lax.dot_general(a.astype(BF16), b.astype(BF16), (((1,), (1,)), ((), ())),
                           preferred_element_type=F32)


def _mm_tn(a, b):
    return lax.dot_general(a.astype(BF16), b.astype(BF16), (((0,), (0,)), ((), ())),
                           preferred_element_type=F32)


def _mm_split(m_exact, x):
    hi = x.astype(BF16)
    lo = (x - hi.astype(F32)).astype(BF16)
    return (jnp.dot(m_exact, hi, preferred_element_type=F32)
            + jnp.dot(m_exact, lo, preferred_element_type=F32))


def _split_mm(x, m_exact):
    hi = x.astype(BF16)
    lo = (x - hi.astype(F32)).astype(BF16)
    return (jnp.dot(hi, m_exact, preferred_element_type=F32)
            + jnp.dot(lo, m_exact, preferred_element_type=F32))


def _sigmoid(x):
    return 1.0 / (1.0 + jnp.exp(-x))


PACKED = D_MODEL // 2


def _pack_rows(x):
    bits = pltpu.bitcast(x.astype(BF16).astype(F32), jnp.uint32)
    return (bits[:, :PACKED] >> 16) | (bits[:, PACKED:] & jnp.uint32(0xFFFF0000))


def _unpack_rows(w):
    lo = pltpu.bitcast(w << 16, F32)
    hi = pltpu.bitcast(w & jnp.uint32(0xFFFF0000), F32)
    return jnp.concatenate([lo, hi], axis=1)


def _mod_kernel(c_ref, w_ref, b_ref, o_ref):
    c = c_ref[...]
    ca = c * _sigmoid(c)
    o_ref[0] = jnp.dot(ca, w_ref[0], preferred_element_type=F32,
                       precision=lax.Precision.HIGHEST) + b_ref[0]


def _modulation(c, ada_w, ada_b):
    depth, d, n = ada_w.shape
    bsz = c.shape[0]
    nb = n // d
    return pl.pallas_call(
        _mod_kernel,
        grid=(depth, nb),
        in_specs=[pl.BlockSpec((bsz, d), lambda l, j: (0, 0)),
                  pl.BlockSpec((1, d, d), lambda l, j: (l, 0, j)),
                  pl.BlockSpec((1, 1, d), lambda l, j: (l, 0, j))],
        out_specs=pl.BlockSpec((1, bsz, d), lambda l, j: (l, 0, j)),
        out_shape=jax.ShapeDtypeStruct((depth, bsz, n), F32),
        compiler_params=_cparams(("arbitrary", "arbitrary")),
        name="adaln_mod",
    )(c, ada_w, ada_b.reshape(depth, 1, n))


def _inproj_kernel(x_ref, g_ref, sc_ref, sh_ref, w_ref, mu_ref, cos_ref, s1_ref, s2_ref,
                   qa_ref, qb_ref, k_ref, vt_ref, p_ref, carry_ref):
    i = pl.program_id(1)
    tm = x_ref.shape[0]
    x = x_ref[...]
    ms = jnp.mean(x * x, axis=-1, keepdims=True)
    h = x * lax.rsqrt(ms + NORM_EPS) * g_ref[...]
    h = h * (1.0 + sc_ref[0]) + sh_ref[0]
    hb = h.astype(BF16)

    cosv, s1v, s2v = cos_ref[...], s1_ref[...], s2_ref[...]

    def rot128(t):
        return (t * cosv + pltpu.roll(t, ROT_DIM // 2, 1) * s1v
                + pltpu.roll(t, 128 - ROT_DIM // 2, 1) * s2v)

    lane = lax.broadcasted_iota(jnp.int32, (1, 128), 1)
    first_map = lane < HEAD_DIM
    scale = HEAD_DIM ** -0.5 * LOG2E
    qkv = jnp.dot(hb, w_ref[:, :QKV_COLS], preferred_element_type=F32)
    for hd in range(ATTN_HEADS):
        lo, hi = hd * 128, (hd + 1) * 128
        q = rot128(qkv[:, lo:hi]) * scale
        qa_ref[:, lo:hi] = jnp.where(first_map, q, 0.0).astype(BF16)
        qb_ref[:, lo:hi] = jnp.where(first_map, 0.0, q).astype(BF16)
        k_ref[:, lo:hi] = rot128(qkv[:, ATTN_WIDTH + lo:ATTN_WIDTH + hi]).astype(BF16)
    vt_ref[...] = qkv[:, 2 * ATTN_WIDTH:].T.astype(BF16)

    @pl.when(i == 0)
    def _():
        carry_ref[...] = jnp.zeros_like(carry_ref)

    p = jnp.dot(hb, w_ref[:, QKV_COLS:], preferred_element_type=F32)
    row = lax.broadcasted_iota(jnp.int32, p.shape, 0)
    prev = jnp.where(row == 0, carry_ref[...], pltpu.roll(p, 1, 0))
    carry_ref[...] = p[tm - 1:tm, :]
    p_ref[...] = (p + (prev - p) * mu_ref[...]).astype(BF16)


def _in_projection(x2, g1, mod3, layer, bsz, seq, w_pad, mu_pad, cos_t, s1_t, s2_t):
    tm = min(512, seq)
    nt = seq // tm
    tok = x2.shape[0]
    row = lambda b, i: (b * nt + i, 0)
    const = lambda b, i: (0, 0)
    wide = w_pad.shape[1]
    outs = pl.pallas_call(
        _inproj_kernel,
        grid=(bsz, nt),
        in_specs=[pl.BlockSpec((tm, D_MODEL), row),
                  pl.BlockSpec((1, D_MODEL), const),
                  pl.BlockSpec((1, 1, D_MODEL), lambda b, i: (layer * bsz + b, 0, 1)),
                  pl.BlockSpec((1, 1, D_MODEL), lambda b, i: (layer * bsz + b, 0, 0)),
                  pl.BlockSpec((D_MODEL, wide), const),
                  pl.BlockSpec((1, RWKV_COLS), const),
                  pl.BlockSpec((tm, 128), row),
                  pl.BlockSpec((tm, 128), row),
                  pl.BlockSpec((tm, 128), row)],
        out_specs=[pl.BlockSpec((tm, ATTN_WIDTH), row)] * 3
        + [pl.BlockSpec((ATTN_WIDTH, tm), lambda b, i: (b, i)), pl.BlockSpec((tm, RWKV_COLS), row)],
        out_shape=[jax.ShapeDtypeStruct((tok, ATTN_WIDTH), BF16)] * 3
        + [jax.ShapeDtypeStruct((bsz * ATTN_WIDTH, seq), BF16),
           jax.ShapeDtypeStruct((tok, RWKV_COLS), BF16)],
        scratch_shapes=[pltpu.VMEM((1, RWKV_COLS), F32)],
        compiler_params=_cparams(("arbitrary", "arbitrary")),
        name="in_proj",
    )(x2, g1, mod3, mod3, w_pad, mu_pad, cos_t, s1_t, s2_t)
    return outs


ATTN_TILE = 1024
ATTN_QLANES = 256
ATTN_KEYS = 512
ATTN_AHEAD = 4
LOG2E = 1.4426950408889634


def _attn_kernel(qi_ref, kj_ref, qa_ref, qb_ref, k_ref, vt_ref, lam_ref, g_ref, o_ref,
                 m_ref, l_ref, acc_ref, *, lambda_init):
    step = pl.program_id(1)
    i = qi_ref[step]
    j = kj_ref[step]
    tq = qa_ref.shape[0]
    tk = k_ref.shape[0]
    hq = min(ATTN_QLANES, tq)
    n_half = tq // hq

    @pl.when(j == 0)
    def _():
        m_ref[...] = jnp.full(m_ref.shape, NEG_BIG, F32)
        l_ref[...] = jnp.zeros_like(l_ref)
        acc_ref[...] = jnp.zeros_like(acc_ref)

    kb = min(ATTN_KEYS, tk)

    def accumulate(diagonal):
        def visible(half, k0, kn):
            kc = (lax.broadcasted_iota(jnp.int32, (kn, hq), 0) + k0) // ATTN_CHUNK
            qc = (lax.broadcasted_iota(jnp.int32, (kn, hq), 1) + half * hq) // ATTN_CHUNK
            return kc <= qc

        spans = []
        for half in range(n_half):
            if not diagonal:
                spans += [(half, k0, kb, False) for k0 in range(0, tk, kb)]
            else:
                if half > 0:
                    spans.append((half, 0, half * hq, False))
                spans.append((half, half * hq, hq, True))
        chains = [(half, k0, kn, masked, hd, mp) for (half, k0, kn, masked) in spans
                  for hd in range(ATTN_HEADS) for mp in range(2)]

        def scores(chain):
            half, k0, kn, _, hd, mp = chain
            q_ref = (qa_ref, qb_ref)[mp]
            return lax.dot_general(k_ref[k0:k0 + kn, hd * 128:(hd + 1) * 128],
                                   q_ref[half * hq:(half + 1) * hq, hd * 128:(hd + 1) * 128],
                                   (((1,), (1,)), ((), ())), preferred_element_type=F32)

        ahead = [scores(c) for c in chains[:ATTN_AHEAD]]
        for n, (half, k0, kn, masked, hd, mp) in enumerate(chains):
            s = ahead.pop(0)
            if n + ATTN_AHEAD < len(chains):
                ahead.append(scores(chains[n + ATTN_AHEAD]))
            idx = (2 * hd + mp) * n_half + half
            if masked:
                s = jnp.where(visible(half, k0, kn), s, NEG_BIG)
            m_old = m_ref[idx]
            m_new = jnp.maximum(m_old, jnp.max(s, axis=0, keepdims=True))
            alpha = jnp.exp2(m_old - m_new)
            p = jnp.exp2(s - m_new)
            l_ref[idx] = alpha * l_ref[idx] + jnp.sum(p, axis=0, keepdims=True)
            acc_ref[idx] = alpha * acc_ref[idx] + jnp.dot(
                vt_ref[hd * 128:(hd + 1) * 128, k0:k0 + kn], p.astype(BF16),
                preferred_element_type=F32)
            m_ref[idx] = m_new

    @pl.when(j < i)
    def _():
        accumulate(False)

    @pl.when(j == i)
    def _():
        accumulate(True)
        lv = lam_ref[...]
        lam = (jnp.exp(jnp.sum(lv[0:1] * lv[1:2], axis=-1, keepdims=True))
               - jnp.exp(jnp.sum(lv[2:3] * lv[3:4], axis=-1, keepdims=True)) + lambda_init)
        gcol = jnp.concatenate([g_ref[...]] * (hq // 128), axis=1)
        for hd in range(ATTN_HEADS):
            for half in range(n_half):
                a = (2 * hd) * n_half + half
                b = (2 * hd + 1) * n_half + half
                o = acc_ref[a] / l_ref[a] - lam * (acc_ref[b] / l_ref[b])
                ms = jnp.mean(o * o, axis=0, keepdims=True)
                o = o * lax.rsqrt(ms + SUBLN_EPS) * gcol * (1.0 - lambda_init)
                o_ref[half * hq:(half + 1) * hq, hd * 128:(hd + 1) * 128] = o.T.astype(BF16)


def _diff_attention(qa, qb, k, vt, lam_vecs, subln_g, lambda_init, bsz, seq):
    tq = min(ATTN_TILE, seq)
    nq = seq // tq
    tok = qa.shape[0]
    hq = min(ATTN_QLANES, tq)
    n_stat = 2 * ATTN_HEADS * (tq // hq)
    pairs = [(i, j) for i in range(nq) for j in range(i + 1)]
    qi = jnp.asarray([p[0] for p in pairs], jnp.int32)
    kj = jnp.asarray([p[1] for p in pairs], jnp.int32)
    qmap = lambda b, s, qi, kj: (b * nq + qi[s], 0)
    kmap = lambda b, s, qi, kj: (b * nq + kj[s], 0)
    vmap = lambda b, s, qi, kj: (b, kj[s])
    const = lambda b, s, qi, kj: (0, 0)
    g_col = jnp.broadcast_to(subln_g.reshape(2 * HEAD_DIM, 1), (2 * HEAD_DIM, 128))
    grid_spec = pltpu.PrefetchScalarGridSpec(
        num_scalar_prefetch=2,
        grid=(bsz, len(pairs)),
        in_specs=[pl.BlockSpec((tq, ATTN_WIDTH), qmap),
                  pl.BlockSpec((tq, ATTN_WIDTH), qmap),
                  pl.BlockSpec((tq, ATTN_WIDTH), kmap),
                  pl.BlockSpec((ATTN_WIDTH, tq), vmap),
                  pl.BlockSpec((4, HEAD_DIM), const),
                  pl.BlockSpec((2 * HEAD_DIM, 128), const)],
        out_specs=pl.BlockSpec((tq, ATTN_WIDTH), qmap),
        scratch_shapes=[pltpu.VMEM((n_stat, 1, hq), F32),
                        pltpu.VMEM((n_stat, 1, hq), F32),
                        pltpu.VMEM((n_stat, 2 * HEAD_DIM, hq), F32)],
    )
    return pl.pallas_call(
        functools.partial(_attn_kernel, lambda_init=lambda_init),
        grid_spec=grid_spec,
        out_shape=jax.ShapeDtypeStruct((tok, ATTN_WIDTH), BF16),
        compiler_params=_cparams(("arbitrary", "arbitrary")),
        name="diff_attn",
    )(qi, kj, qa, qb, k, vt, lam_vecs, g_col)


def _softplus(z):
    return jnp.maximum(z, 0.0) + jnp.log(1.0 + jnp.exp(-jnp.abs(z)))


def _rwkv_kernel(p_ref, prm_ref, up_ref, tri_ref, bd_ref, sl_ref, il_ref, eye_ref,
                 o_ref, h_ref):
    t = pl.program_id(1)
    C = WKV_CHUNK
    n_chunks = p_ref.shape[0] // C
    W = RWKV_WIDTH

    @pl.when(t == 0)
    def _():
        h_ref[...] = jnp.zeros_like(h_ref)

    p = p_ref[...].astype(F32)
    r, k, v, lo = p[:, :W], p[:, W:2 * W], p[:, 2 * W:3 * W], p[:, 3 * W:]
    lane = lax.broadcasted_iota(jnp.int32, lo.shape, 1)
    act = jnp.where(lane < W_LORA, jnp.tanh(lo), jnp.where(lane < W_LORA + A_LORA, lo, _sigmoid(lo)))
    up = _mm(act, up_ref[...])
    prm = prm_ref[...]
    w0, a0, k_k, k_a, r_k, ln_g, ln_b = (prm[n:n + 1] for n in range(7))
    w_raw = -_softplus(-(w0 + up[:, :W])) - 0.5
    logd = -jnp.exp(w_raw)
    a_sig = _sigmoid(a0 + up[:, W:2 * W])
    gate = up[:, 2 * W:]

    bd = bd_ref[...]
    bd_lo = bd.astype(BF16)

    def head_sum(z):
        return jnp.concatenate([_mm(z[:, g * SLAB:(g + 1) * SLAB], bd_lo)
                                for g in range(W // SLAB)], axis=1)

    kkr = k * k_k
    kk = kkr / jnp.maximum(jnp.sqrt(head_sum(kkr * kkr)), 1e-12)
    kf = k * (1.0 + (a_sig - 1.0) * k_a)
    bonus = head_sum(r * kf * r_k) * v
    a_vec = -kk
    b_vec = kk * a_sig

    cum = _mm_split(tri_ref[...], logd)
    cum_end = jnp.concatenate(
        [jnp.broadcast_to(cum[(ch + 1) * C - 1:(ch + 1) * C, :], (C, W)) for ch in range(n_chunks)],
        axis=0)
    d_in = jnp.exp(cum)
    d_ex = jnp.exp(cum - logd)
    d_inv = jnp.exp(-cum)
    d_end = jnp.exp(cum_end)
    d_rest = jnp.exp(cum_end - cum)

    at, rt = a_vec * d_ex, r * d_in
    bt, kt = b_vec * d_inv, kf * d_inv
    be, ke = b_vec * d_rest, kf * d_rest

    sl = sl_ref[...]
    il = il_ref[...]
    eye_c = il - sl
    eye_s = eye_ref[...]
    heads = SLAB // RWKV_HEAD
    n_slabs = W // SLAB
    all_probs = [(ch, g) for ch in range(n_chunks) for g in range(n_slabs)]

    def cut(z, ch, g):
        return z[ch * C:(ch + 1) * C, g * SLAB:(g + 1) * SLAB]

    def stack(zz):
        return jnp.concatenate([zz.astype(BF16)] * heads, axis=0) * bd_lo

    def solve(probs):
        ats = [cut(at, *pr) for pr in probs]
        rts = [cut(rt, *pr) for pr in probs]
        vs = [cut(v, *pr) for pr in probs]
        btm = [stack(cut(bt, *pr)) for pr in probs]
        ktm = [stack(cut(kt, *pr)) for pr in probs]
        vm = [stack(v_) for v_ in vs]
        l_ab = [sl * _mm_nt(a_, b_) for a_, b_ in zip(ats, btm)]
        l_ak = [sl * _mm_nt(a_, k_) for a_, k_ in zip(ats, ktm)]
        lpm = [stack(l_) for l_ in l_ab]
        lp = [_mm(l_, m_) for l_, m_ in zip(l_ab, lpm)]
        inv = [eye_c + l_ for l_ in l_ab]
        wv = [_mm(l_, v_) for l_, v_ in zip(l_ak, vm)]
        m_rb = [il * _mm_nt(r_, b_) for r_, b_ in zip(rts, btm)]
        m_rk = [il * _mm_nt(r_, k_) for r_, k_ in zip(rts, ktm)]
        levels = int(math.log2(C)) - 1
        for lev in range(levels):
            lpm = [stack(l_) for l_ in lp]
            inv_new = [i_ + _mm(i_, m_) for i_, m_ in zip(inv, lpm)]
            if lev + 1 < levels:
                lp = [_mm(l_, m_) for l_, m_ in zip(lp, lpm)]
            inv = inv_new
        abar = [_mm(i_, stack(a_)) for i_, a_ in zip(inv, ats)]
        vbar = [_mm(i_, stack(w_)) for i_, w_ in zip(inv, wv)]
        abm = [stack(a_) for a_ in abar]
        vbm = [stack(v_) for v_ in vbar]
        rhat = [r_ + _mm(m_, a_) for r_, m_, a_ in zip(rts, m_rb, abm)]
        yhat = [_mm(mb_, vb_) + _mm(mk_, v_) for mb_, vb_, mk_, v_ in zip(m_rb, vbm, m_rk, vm)]
        bes = [cut(be, *pr) for pr in probs]
        kes = [cut(ke, *pr) for pr in probs]
        gmat = [eye_s * cut(d_end, *pr)[0:1] + bd * _mm_tn(b_, a_)
                for pr, b_, a_ in zip(probs, bes, abar)]
        fmat = [bd * _mm_tn(jnp.concatenate([b_, k_], axis=0), jnp.concatenate([vb_, v_], axis=0))
                for b_, k_, vb_, v_ in zip(bes, kes, vbar, vs)]
        return rhat, yhat, gmat, fmat

    rhat, yhat, gmat, fmat = [], [], [], []
    for first in range(0, len(all_probs), WKV_GROUP):
        for total, part in zip((rhat, yhat, gmat, fmat), solve(all_probs[first:first + WKV_GROUP])):
            total += part

    hs = [h_ref[g] for g in range(n_slabs)]
    rows = []
    for ch in range(n_chunks):
        ys = []
        for g in range(n_slabs):
            n = ch * n_slabs + g
            ys.append(_mm(rhat[n], hs[g]) + yhat[n])
            hs[g] = _mm(gmat[n], hs[g]) + fmat[n]
        rows.append(jnp.concatenate(ys, axis=1))
    for g in range(n_slabs):
        h_ref[g] = hs[g]
    y = jnp.concatenate(rows, axis=0) if n_chunks > 1 else rows[0]

    inv_n = 1.0 / RWKV_HEAD
    mu = head_sum(y) * inv_n
    yc = y - mu
    var = head_sum(yc * yc) * inv_n
    yn = yc * lax.rsqrt(var + GN_EPS) * ln_g + ln_b
    o_ref[...] = ((yn + bonus) * gate).astype(BF16)


def _block_masks():
    n = SLAB
    r = jnp.arange(n)[:, None]
    c = jnp.arange(n)[None, :]
    same = (r // RWKV_HEAD) == (c // RWKV_HEAD)
    bd = same.astype(F32)
    eye = (r == c).astype(F32)
    t = jnp.arange(WKV_CHUNK)[:, None]
    s_in_head = c % RWKV_HEAD
    sl = (s_in_head < t).astype(F32)
    il = (s_in_head <= t).astype(F32)
    tt = jnp.arange(WKV_STEP)
    tri = ((tt[None, :] <= tt[:, None])
           & (tt[None, :] // WKV_CHUNK == tt[:, None] // WKV_CHUNK)).astype(BF16)
    return tri, bd, sl, il, eye


def _rwkv_mix(p, prm, up_w, masks, bsz, seq):
    C = WKV_STEP
    nc = seq // C
    tok = p.shape[0]
    tri, bd, sl, il, eye = masks
    row = lambda b, t: (b * nc + t, 0)
    const = lambda b, t: (0, 0)
    return pl.pallas_call(
        _rwkv_kernel,
        grid=(bsz, nc),
        in_specs=[pl.BlockSpec((C, RWKV_COLS), row),
                  pl.BlockSpec((8, RWKV_WIDTH), const),
                  pl.BlockSpec((LORA_PAD, 3 * RWKV_WIDTH), const),
                  pl.BlockSpec((C, C), const),
                  pl.BlockSpec((SLAB, SLAB), const),
                  pl.BlockSpec((WKV_CHUNK, SLAB), const),
                  pl.BlockSpec((WKV_CHUNK, SLAB), const),
                  pl.BlockSpec((SLAB, SLAB), const)],
        out_specs=pl.BlockSpec((C, RWKV_WIDTH), row),
        out_shape=jax.ShapeDtypeStruct((tok, RWKV_WIDTH), BF16),
        scratch_shapes=[pltpu.VMEM((RWKV_WIDTH // SLAB, SLAB, SLAB), F32)],
        compiler_params=_cparams(("arbitrary", "arbitrary")),
        name="rwkv7_mix",
    )(p, prm, up_w, tri, bd, sl, il, eye)


def _out_kernel(a_ref, r_ref, x_ref, wa_ref, wr_ref, gt_ref, g2_ref, sc_ref, sh_ref, wrt_ref,
                brt_ref, x1_ref, h2_ref, rt_ref):
    mixed = (jnp.dot(a_ref[...], wa_ref[...], preferred_element_type=F32)
             + jnp.dot(r_ref[...], wr_ref[...], preferred_element_type=F32))
    x1 = x_ref[...] + gt_ref[0] * mixed
    x1_ref[...] = x1
    ms = jnp.mean(x1 * x1, axis=-1, keepdims=True)
    h2 = x1 * lax.rsqrt(ms + NORM_EPS) * g2_ref[...]
    h2 = h2 * (1.0 + sc_ref[0]) + sh_ref[0]
    h2_ref[...] = _pack_rows(h2)
    h_hi = h2.astype(BF16)
    h_lo = (h2 - h_hi.astype(F32)).astype(BF16)
    logits = (jnp.dot(h_hi, wrt_ref[0], preferred_element_type=F32)
              + jnp.dot(h_lo, wrt_ref[0], preferred_element_type=F32)
              + jnp.dot(h_hi, wrt_ref[1], preferred_element_type=F32)) + brt_ref[...]

    lane_i = lax.broadcasted_iota(jnp.int32, logits.shape, 1)
    lane = lane_i.astype(F32)
    far = float(ROUTE_LANES)
    is_g = lane_i < N_GROUPS
    gmax = jnp.max(jnp.where(is_g, logits, NEG_BIG), axis=-1, keepdims=True)
    gsum = jnp.sum(jnp.where(is_g, jnp.exp(logits - gmax), 0.0), axis=-1, keepdims=True)
    g_top = 1.0 / gsum
    g_idx = jnp.min(jnp.where(is_g & (logits == gmax), lane, far), axis=-1, keepdims=True)
    base = N_GROUPS + g_idx * EXPERTS_PER_GROUP
    in_g = (lane >= base) & (lane < base + EXPERTS_PER_GROUP)
    v1 = jnp.max(jnp.where(in_g, logits, NEG_BIG), axis=-1, keepdims=True)
    i1 = jnp.min(jnp.where(in_g & (logits == v1), lane, far), axis=-1, keepdims=True)
    rest = in_g & (lane != i1)
    v2 = jnp.max(jnp.where(rest, logits, NEG_BIG), axis=-1, keepdims=True)
    i2 = jnp.min(jnp.where(rest & (logits == v2), lane, far), axis=-1, keepdims=True)
    e21 = jnp.exp(v2 - v1)
    w1 = g_top / (1.0 + e21)
    w2 = g_top * e21 / (1.0 + e21)
    rt_ref[...] = jnp.where(lane_i == 0, i1 - N_GROUPS,
                            jnp.where(lane_i == 1, i2 - N_GROUPS,
                                      jnp.where(lane_i == 2, w1,
                                                jnp.where(lane_i == 3, w2, 0.0))))


def _out_projection(attn_o, rwkv_o, x2, w_a, w_r, mod3, g2, w_rt, b_rt, layer, bsz, seq):
    tm = min(512, seq)
    nt = seq // tm
    tok = x2.shape[0]
    row = lambda b, i: (b * nt + i, 0)
    const = lambda b, i: (0, 0)
    modspec = lambda col: pl.BlockSpec((1, 1, D_MODEL), lambda b, i: (layer * bsz + b, 0, col))
    return pl.pallas_call(
        _out_kernel,
        grid=(bsz, nt),
        in_specs=[pl.BlockSpec((tm, ATTN_WIDTH), row),
                  pl.BlockSpec((tm, RWKV_WIDTH), row),
                  pl.BlockSpec((tm, D_MODEL), row),
                  pl.BlockSpec((ATTN_WIDTH, D_MODEL), const),
                  pl.BlockSpec((RWKV_WIDTH, D_MODEL), const),
                  modspec(2),
                  pl.BlockSpec((1, D_MODEL), const),
                  modspec(4),
                  modspec(3),
                  pl.BlockSpec((2, D_MODEL, ROUTE_LANES), lambda b, i: (0, 0, 0)),
                  pl.BlockSpec((1, ROUTE_LANES), const)],
        out_specs=[pl.BlockSpec((tm, D_MODEL), row),
                   pl.BlockSpec((tm, PACKED), row),
                   pl.BlockSpec((tm, ROUTE_LANES), row)],
        out_shape=[jax.ShapeDtypeStruct((tok, D_MODEL), F32),
                   jax.ShapeDtypeStruct((tok, PACKED), jnp.uint32),
                   jax.ShapeDtypeStruct((tok, ROUTE_LANES), F32)],
        compiler_params=_cparams(("arbitrary", "arbitrary")),
        name="out_proj_router",
    )(attn_o, rwkv_o, x2, w_a, w_r, mod3, g2, mod3, mod3, w_rt, b_rt)


def _dispatch_kernel(dest_ref, h_ref, xin_ref, xs_ref, stage_ref, sems):
    del xin_ref
    tt = h_ref.shape[0]
    i = pl.program_id(0)
    slot = i % 2
    stage_ref[slot] = h_ref[...]

    def issue(g, c):
        base = pl.multiple_of(g * DMA_UNROLL, DMA_UNROLL)
        for u in range(DMA_UNROLL):
            for kk in range(TOP_K):
                d = dest_ref[0, 0, TOP_K * base + (TOP_K * u + kk)]
                pltpu.make_async_copy(stage_ref.at[slot, pl.ds(base + u, 1)],
                                      xs_ref.at[pl.ds(d, 1)], sems.at[slot]).start(priority=kk)
        return c

    def drain(which):
        for kk in range(TOP_K):
            pltpu.make_async_copy(stage_ref.at[which], xs_ref.at[pl.ds(0, tt)],
                                  sems.at[which]).wait()

    lax.fori_loop(0, tt // DMA_UNROLL, issue, 0)

    @pl.when(i > 0)
    def _():
        drain(1 - slot)

    @pl.when(i + 1 == pl.num_programs(0))
    def _():
        drain(slot)


def _dispatch(dest3, h2, xs_init):
    tok = h2.shape[0]
    nt, _, two_tt = dest3.shape
    tt = two_tt // TOP_K
    return pl.pallas_call(
        _dispatch_kernel,
        grid=(nt,),
        in_specs=[pl.BlockSpec((1, 1, two_tt), lambda i: (i, 0, 0), memory_space=pltpu.SMEM),
                  pl.BlockSpec((tt, PACKED), lambda i: (i, 0)),
                  pl.BlockSpec(memory_space=pl.ANY)],
        out_specs=pl.BlockSpec(memory_space=pl.ANY),
        out_shape=jax.ShapeDtypeStruct(xs_init.shape, jnp.uint32),
        scratch_shapes=[pltpu.VMEM((2, tt, PACKED), jnp.uint32), pltpu.SemaphoreType.DMA((2,))],
        input_output_aliases={2: 0},
        compiler_params=_cparams(("arbitrary",)),
        name="moe_dispatch",
    )(dest3, h2, xs_init)


def _expert_kernel(be_ref, x_ref, wg_ref, wu_ref, wd_ref, y_ref, wg_lo, wu_lo, wd_lo):
    i = pl.program_id(0)

    @pl.when((i == 0) | (be_ref[i] != be_ref[jnp.maximum(i - 1, 0)]))
    def _():
        wg_lo[...] = wg_ref[0].astype(BF16)
        wu_lo[...] = wu_ref[0].astype(BF16)
        wd_lo[...] = wd_ref[0].astype(BF16)

    xb = _unpack_rows(x_ref[...]).astype(BF16)
    gate = jnp.dot(xb, wg_lo[...], preferred_element_type=F32)
    up = jnp.dot(xb, wu_lo[...], preferred_element_type=F32)
    hid = gate * _sigmoid(gate) * up
    y_ref[...] = _pack_rows(jnp.dot(hid.astype(BF16), wd_lo[...], preferred_element_type=F32))


def _experts(block_e, xs, w_gate, w_up, w_down, layer):
    rows = xs.shape[0]
    nb = rows // MOE_ROWS
    grid_spec = pltpu.PrefetchScalarGridSpec(
        num_scalar_prefetch=1,
        grid=(nb,),
        in_specs=[pl.BlockSpec((MOE_ROWS, PACKED), lambda i, be: (i, 0)),
                  pl.BlockSpec((1, D_MODEL, D_EXPERT), lambda i, be: (layer * N_EXPERTS + be[i], 0, 0)),
                  pl.BlockSpec((1, D_MODEL, D_EXPERT), lambda i, be: (layer * N_EXPERTS + be[i], 0, 0)),
                  pl.BlockSpec((1, D_EXPERT, D_MODEL), lambda i, be: (layer * N_EXPERTS + be[i], 0, 0))],
        out_specs=pl.BlockSpec((MOE_ROWS, PACKED), lambda i, be: (i, 0)),
        scratch_shapes=[pltpu.VMEM((D_MODEL, D_EXPERT), BF16),
                        pltpu.VMEM((D_MODEL, D_EXPERT), BF16),
                        pltpu.VMEM((D_EXPERT, D_MODEL), BF16)],
    )
    return pl.pallas_call(
        _expert_kernel,
        grid_spec=grid_spec,
        out_shape=jax.ShapeDtypeStruct((rows, PACKED), jnp.uint32),
        compiler_params=_cparams(("arbitrary",)),
        name="moe_experts",
    )(block_e, xs, w_gate, w_up, w_down)


def _combine_kernel(dest_ref, next_ref, yb_ref, rt_ref, x_ref, gt_ref, fg_ref, o_ref, buf_ref, sems,
                    *, final_norm):
    tt = x_ref.shape[0]
    i = pl.program_id(0)
    slot = i % 2

    def gather(table_ref, into):
        def issue(g, c):
            base = pl.multiple_of(g * DMA_UNROLL, DMA_UNROLL)
            for u in range(DMA_UNROLL):
                for kk in range(TOP_K):
                    d = table_ref[0, 0, TOP_K * base + (TOP_K * u + kk)]
                    pltpu.make_async_copy(yb_ref.at[pl.ds(d, 1)],
                                          buf_ref.at[into, kk, pl.ds(base + u, 1)],
                                          sems.at[into]).start(priority=kk)
            return c

        lax.fori_loop(0, tt // DMA_UNROLL, issue, 0)

    @pl.when(i == 0)
    def _():
        gather(dest_ref, 0)

    @pl.when(i + 1 < pl.num_programs(0))
    def _():
        gather(next_ref, 1 - slot)

    for kk in range(TOP_K):
        pltpu.make_async_copy(yb_ref.at[pl.ds(0, tt)], buf_ref.at[slot, kk], sems.at[slot]).wait()
    rt = rt_ref[...]
    y = (_unpack_rows(buf_ref[slot, 0]) * rt[:, 2:3]
         + _unpack_rows(buf_ref[slot, 1]) * rt[:, 3:4])
    x = x_ref[...] + gt_ref[0] * y
    if final_norm:
        ms = jnp.mean(x * x, axis=-1, keepdims=True)
        x = x * lax.rsqrt(ms + NORM_EPS) * fg_ref[...]
    o_ref[...] = x


def _combine(dest3, yb, rt, x1, mod3, final_g, layer, bsz, seq, final_norm):
    tok = x1.shape[0]
    nt, _, two_tt = dest3.shape
    tt = two_tt // TOP_K
    per_seq = seq // tt
    return pl.pallas_call(
        functools.partial(_combine_kernel, final_norm=final_norm),
        grid=(nt,),
        in_specs=[pl.BlockSpec((1, 1, two_tt), lambda i: (i, 0, 0), memory_space=pltpu.SMEM),
                  pl.BlockSpec((1, 1, two_tt), lambda i: (jnp.minimum(i + 1, nt - 1), 0, 0),
                               memory_space=pltpu.SMEM),
                  pl.BlockSpec(memory_space=pl.ANY),
                  pl.BlockSpec((tt, ROUTE_LANES), lambda i: (i, 0)),
                  pl.BlockSpec((tt, D_MODEL), lambda i: (i, 0)),
                  pl.BlockSpec((1, 1, D_MODEL), lambda i: (layer * bsz + i // per_seq, 0, 5)),
                  pl.BlockSpec((1, D_MODEL), lambda i: (0, 0))],
        out_specs=pl.BlockSpec((tt, D_MODEL), lambda i: (i, 0)),
        out_shape=jax.ShapeDtypeStruct((tok, D_MODEL), F32),
        scratch_shapes=[pltpu.VMEM((2, TOP_K, tt, PACKED), jnp.uint32),
                        pltpu.SemaphoreType.DMA((2,))],
        compiler_params=_cparams(("arbitrary",)),
        name="moe_combine",
    )(dest3, dest3, yb, rt, x1, mod3, final_g)


def _routing_tables(rt, tok):
    m = tok * TOP_K
    flat_e = rt[:, :TOP_K].astype(jnp.int32).reshape(m)
    onehot = (flat_e[:, None] == jnp.arange(N_EXPERTS, dtype=jnp.int32)[None, :]).astype(jnp.int32)
    ct = min(COUNT_TILE, m)
    tri = (jnp.arange(ct)[None, :] <= jnp.arange(ct)[:, None]).astype(BF16)
    within = jnp.einsum("ts,nse->nte", tri, onehot.reshape(m // ct, ct, N_EXPERTS).astype(BF16),
                        preferred_element_type=F32)
    tile_tot = within[:, -1, :]
    tile_base = jnp.cumsum(tile_tot, axis=0) - tile_tot
    running = (within + tile_base[:, None, :]).reshape(m, N_EXPERTS).astype(jnp.int32)
    counts = running[-1]
    padded = (counts + MOE_ROWS - 1) // MOE_ROWS * MOE_ROWS
    pad_end = jnp.cumsum(padded)
    pad_start = pad_end - padded
    dest = jnp.sum(onehot * (running - 1 + pad_start[None, :]), axis=1)
    rows = -(-m // MOE_ROWS) * MOE_ROWS + N_EXPERTS * MOE_ROWS
    nb = rows // MOE_ROWS
    block_start = jnp.arange(nb, dtype=jnp.int32) * MOE_ROWS
    block_e = jnp.minimum(jnp.sum((pad_end[None, :] <= block_start[:, None]).astype(jnp.int32), axis=1),
                          N_EXPERTS - 1).astype(jnp.int32)
    return dest.astype(jnp.int32), block_e, rows


def _rotary_tables(positions):
    half = ROT_DIM // 2
    inv_freq = ROPE_THETA ** (-jnp.arange(0, ROT_DIM, 2, dtype=F32) / ROT_DIM)
    ang = positions.astype(F32).reshape(-1, 1) * inv_freq[None, :]
    cos, sin = jnp.cos(ang), jnp.sin(ang)
    dim = jnp.arange(128)[None, :] % HEAD_DIM
    freq = jnp.arange(half)[:, None]
    lo = (dim == freq).astype(F32)
    hi = (dim == freq + half).astype(F32)
    rest = (dim >= ROT_DIM).astype(F32)
    place = functools.partial(jnp.dot, precision=lax.Precision.HIGHEST)
    return place(cos, lo + hi) + rest, place(sin, hi), place(sin, -lo)


def _lambda_init(layer):
    return 0.8 - 0.6 * math.exp(-0.3 * layer)


def kernel(x, c, positions, ada_w, ada_b, norm1_g, norm2_g, w_in, w_out, attn_lambda, attn_subln_g, rwkv_shift_mu, rwkv_w0, rwkv_w_up, rwkv_a0, rwkv_a_up, rwkv_g_up, rwkv_k_k, rwkv_k_a, rwkv_r_k, rwkv_lnx_g, rwkv_lnx_b, moe_w_group, moe_b_group, moe_w_router, moe_b_router, moe_w_gate, moe_w_up, moe_w_down, final_g):
    bsz, seq, d = x.shape
    depth = ada_w.shape[0]
    tok = bsz * seq
    x2 = x.reshape(tok, d)

    mod = _modulation(c, ada_w, ada_b)
    mod3 = mod.reshape(depth * bsz, 1, 6 * d)
    cos_t, s1_t, s2_t = _rotary_tables(positions)
    masks = _block_masks()
    tt = min(512, seq)

    for layer in range(depth):
        pad = LORA_PAD - LORA
        w_pad = jnp.pad(w_in[layer], ((0, 0), (0, pad))).astype(BF16)
        mu_pad = jnp.pad(rwkv_shift_mu[layer], (0, pad)).reshape(1, RWKV_COLS)
        qa, qb, k, v, p = _in_projection(x2, norm1_g[layer].reshape(1, d), mod3, layer, bsz, seq,
                                         w_pad, mu_pad, cos_t, s1_t, s2_t)
        attn_o = _diff_attention(qa, qb, k, v, attn_lambda[layer],
                                 attn_subln_g[layer].reshape(1, 2 * HEAD_DIM),
                                 _lambda_init(layer), bsz, seq)

        zrow = jnp.zeros((RWKV_WIDTH,), F32)
        prm = jnp.stack([rwkv_w0[layer], rwkv_a0[layer], rwkv_k_k[layer], rwkv_k_a[layer],
                         rwkv_r_k[layer].reshape(RWKV_WIDTH), rwkv_lnx_g[layer],
                         rwkv_lnx_b[layer], zrow])
        up_w = jnp.zeros((LORA_PAD, 3 * RWKV_WIDTH), F32)
        up_w = up_w.at[:W_LORA, :RWKV_WIDTH].set(rwkv_w_up[layer])
        up_w = up_w.at[W_LORA:W_LORA + A_LORA, RWKV_WIDTH:2 * RWKV_WIDTH].set(rwkv_a_up[layer])
        up_w = up_w.at[W_LORA + A_LORA:LORA, 2 * RWKV_WIDTH:].set(rwkv_g_up[layer])
        rwkv_o = _rwkv_mix(p, prm, up_w.astype(BF16), masks, bsz, seq)

        w_o = w_out[layer].astype(BF16)
        w_rt = jnp.zeros((d, ROUTE_LANES), F32)
        w_rt = w_rt.at[:, :N_GROUPS].set(moe_w_group[layer])
        w_rt = w_rt.at[:, N_GROUPS:N_GROUPS + N_EXPERTS].set(moe_w_router[layer])
        w_rt_hi = w_rt.astype(BF16)
        w_rt = jnp.stack([w_rt_hi, (w_rt - w_rt_hi.astype(F32)).astype(BF16)])
        b_rt = jnp.zeros((1, ROUTE_LANES), F32)
        b_rt = b_rt.at[0, :N_GROUPS].set(moe_b_group[layer])
        b_rt = b_rt.at[0, N_GROUPS:N_GROUPS + N_EXPERTS].set(moe_b_router[layer])
        x1, h2, rt = _out_projection(attn_o, rwkv_o, x2, w_o[:ATTN_WIDTH], w_o[ATTN_WIDTH:], mod3,
                                     norm2_g[layer].reshape(1, d), w_rt, b_rt, layer, bsz, seq)

        dest, block_e, rows = _routing_tables(rt, tok)
        dest3 = dest.reshape(tok // tt, 1, TOP_K * tt)
        xs = _dispatch(dest3, h2, jnp.zeros((rows, PACKED), jnp.uint32))
        yb = _experts(block_e, xs, moe_w_gate.reshape(-1, d, D_EXPERT),
                      moe_w_up.reshape(-1, d, D_EXPERT), moe_w_down.reshape(-1, D_EXPERT, d), layer)
        x2 = _combine(dest3, yb, rt, x1, mod3, final_g.reshape(1, d), layer, bsz, seq,
                      final_norm=(layer == depth - 1))

    return x2.reshape(bsz, seq, d)
```

```python
import functools
import math

import jax
import jax.numpy as jnp
from jax import lax
from jax.experimental import pallas as pl
from jax.experimental.pallas import tpu as pltpu

F32 = jnp.float32
BF16 = jnp.bfloat16

D_MODEL = 1024
ATTN_WIDTH = 512
ATTN_HEADS = 4
HEAD_DIM = 64
ROT_DIM = 16
ROPE_THETA = 500000.0
SUBLN_EPS = 1e-5
ATTN_CHUNK = 64

RWKV_WIDTH = 512
RWKV_HEAD = 64
W_LORA, A_LORA, G_LORA = 32, 32, 96
LORA = W_LORA + A_LORA + G_LORA
LORA_PAD = 256
GN_EPS = 64e-5
RWKV_COLS = 3 * RWKV_WIDTH + LORA_PAD
QKV_COLS = 3 * ATTN_WIDTH
WKV_CHUNK = 64
WKV_GROUP = 8
WKV_STEP = 256
SLAB = 256

N_GROUPS = 4
EXPERTS_PER_GROUP = 8
N_EXPERTS = 32
D_EXPERT = 256
TOP_K = 2
ROUTE_LANES = 128
MOE_ROWS = 512
DMA_UNROLL = 8
COUNT_TILE = 256

NORM_EPS = 1e-6
NEG_BIG = -1e30
VMEM_LIMIT = 56 * 1024 * 1024


def _cparams(sem):
    return pltpu.CompilerParams(dimension_semantics=sem, vmem_limit_bytes=VMEM_LIMIT)


def _mm(a, b):
    return jnp.dot(a.astype(BF16), b.astype(BF16), preferred_element_type=F32)


def _mm_nt(a, b):
    return lax.dot_general(a.astype(BF16), b.astype(BF16), (((1,), (1,)), ((), ())),
                           preferred_element_type=F32)


def _mm_tn(a, b):
    return lax.dot_general(a.astype(BF16), b.astype(BF16), (((0,), (0,)), ((), ())),
                           preferred_element_type=F32)


def _mm_split(m_exact, x):
    hi = x.astype(BF16)
    lo = (x - hi.astype(F32)).astype(BF16)
    return (jnp.dot(m_exact, hi, preferred_element_type=F32)
            + jnp.dot(m_exact, lo, preferred_element_type=F32))


def _split_mm(x, m_exact):
    hi = x.astype(BF16)
    lo = (x - hi.astype(F32)).astype(BF16)
    return (jnp.dot(hi, m_exact, preferred_element_type=F32)
            + jnp.dot(lo, m_exact, preferred_element_type=F32))


def _sigmoid(x):
    return 1.0 / (1.0 + jnp.exp(-x))


PACKED = D_MODEL // 2


def _pack_rows(x):
    bits = pltpu.bitcast(x.astype(BF16).astype(F32), jnp.uint32)
    return (bits[:, :PACKED] >> 16) | (bits[:, PACKED:] & jnp.uint32(0xFFFF0000))


def _unpack_rows(w):
    lo = pltpu.bitcast(w << 16, F32)
    hi = pltpu.bitcast(w & jnp.uint32(0xFFFF0000), F32)
    return jnp.concatenate([lo, hi], axis=1)


def _mod_kernel(c_ref, w_ref, b_ref, o_ref):
    c = c_ref[...]
    ca = c * _sigmoid(c)
    o_ref[0] = jnp.dot(ca, w_ref[0], preferred_element_type=F32,
                       precision=lax.Precision.HIGHEST) + b_ref[0]


def _modulation(c, ada_w, ada_b):
    depth, d, n = ada_w.shape
    bsz = c.shape[0]
    nb = n // d
    return pl.pallas_call(
        _mod_kernel,
        grid=(depth, nb),
        in_specs=[pl.BlockSpec((bsz, d), lambda l, j: (0, 0)),
                  pl.BlockSpec((1, d, d), lambda l, j: (l, 0, j)),
                  pl.BlockSpec((1, 1, d), lambda l, j: (l, 0, j))],
        out_specs=pl.BlockSpec((1, bsz, d), lambda l, j: (l, 0, j)),
        out_shape=jax.ShapeDtypeStruct((depth, bsz, n), F32),
        compiler_params=_cparams(("arbitrary", "arbitrary")),
        name="adaln_mod",
    )(c, ada_w, ada_b.reshape(depth, 1, n))


def _inproj_kernel(x_ref, g_ref, sc_ref, sh_ref, w_ref, mu_ref, cos_ref, s1_ref, s2_ref,
                   qa_ref, qb_ref, k_ref, vt_ref, p_ref, carry_ref):
    i = pl.program_id(1)
    tm = x_ref.shape[0]
    x = x_ref[...]
    ms = jnp.mean(x * x, axis=-1, keepdims=True)
    h = x * lax.rsqrt(ms + NORM_EPS) * g_ref[...]
    h = h * (1.0 + sc_ref[0]) + sh_ref[0]
    hb = h.astype(BF16)

    cosv, s1v, s2v = cos_ref[...], s1_ref[...], s2_ref[...]

    def rot128(t):
        return (t * cosv + pltpu.roll(t, ROT_DIM // 2, 1) * s1v
                + pltpu.roll(t, 128 - ROT_DIM // 2, 1) * s2v)

    lane = lax.broadcasted_iota(jnp.int32, (1, 128), 1)
    first_map = lane < HEAD_DIM
    scale = HEAD_DIM ** -0.5 * LOG2E
    qkv = jnp.dot(hb, w_ref[:, :QKV_COLS], preferred_element_type=F32)
    for hd in range(ATTN_HEADS):
        lo, hi = hd * 128, (hd + 1) * 128
        q = rot128(qkv[:, lo:hi]) * scale
        qa_ref[:, lo:hi] = jnp.where(first_map, q, 0.0).astype(BF16)
        qb_ref[:, lo:hi] = jnp.where(first_map, 0.0, q).astype(BF16)
        k_ref[:, lo:hi] = rot128(qkv[:, ATTN_WIDTH + lo:ATTN_WIDTH + hi]).astype(BF16)
    vt_ref[...] = qkv[:, 2 * ATTN_WIDTH:].T.astype(BF16)

    @pl.when(i == 0)
    def _():
        carry_ref[...] = jnp.zeros_like(carry_ref)

    p = jnp.dot(hb, w_ref[:, QKV_COLS:], preferred_element_type=F32)
    row = lax.broadcasted_iota(jnp.int32, p.shape, 0)
    prev = jnp.where(row == 0, carry_ref[...], pltpu.roll(p, 1, 0))
    carry_ref[...] = p[tm - 1:tm, :]
    p_ref[...] = (p + (prev - p) * mu_ref[...]).astype(BF16)


def _in_projection(x2, g1, mod3, layer, bsz, seq, w_pad, mu_pad, cos_t, s1_t, s2_t):
    tm = min(512, seq)
    nt = seq // tm
    tok = x2.shape[0]
    row = lambda b, i: (b * nt + i, 0)
    const = lambda b, i: (0, 0)
    wide = w_pad.shape[1]
    outs = pl.pallas_call(
        _inproj_kernel,
        grid=(bsz, nt),
        in_specs=[pl.BlockSpec((tm, D_MODEL), row),
                  pl.BlockSpec((1, D_MODEL), const),
                  pl.BlockSpec((1, 1, D_MODEL), lambda b, i: (layer * bsz + b, 0, 1)),
                  pl.BlockSpec((1, 1, D_MODEL), lambda b, i: (layer * bsz + b, 0, 0)),
                  pl.BlockSpec((D_MODEL, wide), const),
                  pl.BlockSpec((1, RWKV_COLS), const),
                  pl.BlockSpec((tm, 128), row),
                  pl.BlockSpec((tm, 128), row),
                  pl.BlockSpec((tm, 128), row)],
        out_specs=[pl.BlockSpec((tm, ATTN_WIDTH), row)] * 3
        + [pl.BlockSpec((ATTN_WIDTH, tm), lambda b, i: (b, i)), pl.BlockSpec((tm, RWKV_COLS), row)],
        out_shape=[jax.ShapeDtypeStruct((tok, ATTN_WIDTH), BF16)] * 3
        + [jax.ShapeDtypeStruct((bsz * ATTN_WIDTH, seq), BF16),
           jax.ShapeDtypeStruct((tok, RWKV_COLS), BF16)],
        scratch_shapes=[pltpu.VMEM((1, RWKV_COLS), F32)],
        compiler_params=_cparams(("arbitrary", "arbitrary")),
        name="in_proj",
    )(x2, g1, mod3, mod3, w_pad, mu_pad, cos_t, s1_t, s2_t)
    return outs


ATTN_TILE = 1024
ATTN_QLANES = 256
ATTN_KEYS = 512
ATTN_AHEAD = 4
LOG2E = 1.4426950408889634


def _attn_kernel(qi_ref, kj_ref, qa_ref, qb_ref, k_ref, vt_ref, lam_ref, g_ref, o_ref,
                 m_ref, l_ref, acc_ref, *, lambda_init):
    step = pl.program_id(1)
    i = qi_ref[step]
    j = kj_ref[step]
    tq = qa_ref.shape[0]
    tk = k_ref.shape[0]
    hq = min(ATTN_QLANES, tq)
    n_half = tq // hq

    @pl.when(j == 0)
    def _():
        m_ref[...] = jnp.full(m_ref.shape, NEG_BIG, F32)
        l_ref[...] = jnp.zeros_like(l_ref)
        acc_ref[...] = jnp.zeros_like(acc_ref)

    kb = min(ATTN_KEYS, tk)

    def accumulate(diagonal):
        def visible(half, k0, kn):
            kc = (lax.broadcasted_iota(jnp.int32, (kn, hq), 0) + k0) // ATTN_CHUNK
            qc = (lax.broadcasted_iota(jnp.int32, (kn, hq), 1) + half * hq) // ATTN_CHUNK
            return kc <= qc

        spans = []
        for half in range(n_half):
            if not diagonal:
                spans += [(half, k0, kb, False) for k0 in range(0, tk, kb)]
            else:
                if half > 0:
                    spans.append((half, 0, half * hq, False))
                spans.append((half, half * hq, hq, True))
        chains = [(half, k0, kn, masked, hd, mp) for (half, k0, kn, masked) in spans
                  for hd in range(ATTN_HEADS) for mp in range(2)]

        def scores(chain):
            half, k0, kn, _, hd, mp = chain
            q_ref = (qa_ref, qb_ref)[mp]
            return lax.dot_general(k_ref[k0:k0 + kn, hd * 128:(hd + 1) * 128],
                                   q_ref[half * hq:(half + 1) * hq, hd * 128:(hd + 1) * 128],
                                   (((1,), (1,)), ((), ())), preferred_element_type=F32)

        ahead = [scores(c) for c in chains[:ATTN_AHEAD]]
        for n, (half, k0, kn, masked, hd, mp) in enumerate(chains):
            s = ahead.pop(0)
            if n + ATTN_AHEAD < len(chains):
                ahead.append(scores(chains[n + ATTN_AHEAD]))
            idx = (2 * hd + mp) * n_half + half
            if masked:
                s = jnp.where(visible(half, k0, kn), s, NEG_BIG)
            m_old = m_ref[idx]
            m_new = jnp.maximum(m_old, jnp.max(s, axis=0, keepdims=True))
            alpha = jnp.exp2(m_old - m_new)
            p = jnp.exp2(s - m_new)
            l_ref[idx] = alpha * l_ref[idx] + jnp.sum(p, axis=0, keepdims=True)
            acc_ref[idx] = alpha * acc_ref[idx] + jnp.dot(
                vt_ref[hd * 128:(hd + 1) * 128, k0:k0 + kn], p.astype(BF16),
                preferred_element_type=F32)
            m_ref[idx] = m_new

    @pl.when(j < i)
    def _():
        accumulate(False)

    @pl.when(j == i)
    def _():
        accumulate(True)
        lv = lam_ref[...]
        lam = (jnp.exp(jnp.sum(lv[0:1] * lv[1:2], axis=-1, keepdims=True))
               - jnp.exp(jnp.sum(lv[2:3] * lv[3:4], axis=-1, keepdims=True)) + lambda_init)
        gcol = jnp.concatenate([g_ref[...]] * (hq // 128), axis=1)
        for hd in range(ATTN_HEADS):
            for half in range(n_half):
                a = (2 * hd) * n_half + half
                b = (2 * hd + 1) * n_half + half
                o = acc_ref[a] / l_ref[a] - lam * (acc_ref[b] / l_ref[b])
                ms = jnp.mean(o * o, axis=0, keepdims=True)
                o = o * lax.rsqrt(ms + SUBLN_EPS) * gcol * (1.0 - lambda_init)
                o_ref[half * hq:(half + 1) * hq, hd * 128:(hd + 1) * 128] = o.T.astype(BF16)


def _diff_attention(qa, qb, k, vt, lam_vecs, subln_g, lambda_init, bsz, seq):
    tq = min(ATTN_TILE, seq)
    nq = seq // tq
    tok = qa.shape[0]
    hq = min(ATTN_QLANES, tq)
    n_stat = 2 * ATTN_HEADS * (tq // hq)
    pairs = [(i, j) for i in range(nq) for j in range(i + 1)]
    qi = jnp.asarray([p[0] for p in pairs], jnp.int32)
    kj = jnp.asarray([p[1] for p in pairs], jnp.int32)
    qmap = lambda b, s, qi, kj: (b * nq + qi[s], 0)
    kmap = lambda b, s, qi, kj: (b * nq + kj[s], 0)
    vmap = lambda b, s, qi, kj: (b, kj[s])
    const = lambda b, s, qi, kj: (0, 0)
    g_col = jnp.broadcast_to(subln_g.reshape(2 * HEAD_DIM, 1), (2 * HEAD_DIM, 128))
    grid_spec = pltpu.PrefetchScalarGridSpec(
        num_scalar_prefetch=2,
        grid=(bsz, len(pairs)),
        in_specs=[pl.BlockSpec((tq, ATTN_WIDTH), qmap),
                  pl.BlockSpec((tq, ATTN_WIDTH), qmap),
                  pl.BlockSpec((tq, ATTN_WIDTH), kmap),
                  pl.BlockSpec((ATTN_WIDTH, tq), vmap),
                  pl.BlockSpec((4, HEAD_DIM), const),
                  pl.BlockSpec((2 * HEAD_DIM, 128), const)],
        out_specs=pl.BlockSpec((tq, ATTN_WIDTH), qmap),
        scratch_shapes=[pltpu.VMEM((n_stat, 1, hq), F32),
                        pltpu.VMEM((n_stat, 1, hq), F32),
                        pltpu.VMEM((n_stat, 2 * HEAD_DIM, hq), F32)],
    )
    return pl.pallas_call(
        functools.partial(_attn_kernel, lambda_init=lambda_init),
        grid_spec=grid_spec,
        out_shape=jax.ShapeDtypeStruct((tok, ATTN_WIDTH), BF16),
        compiler_params=_cparams(("arbitrary", "arbitrary")),
        name="diff_attn",
    )(qi, kj, qa, qb, k, vt, lam_vecs, g_col)


def _softplus(z):
    return jnp.maximum(z, 0.0) + jnp.log(1.0 + jnp.exp(-jnp.abs(z)))


def _rwkv_kernel(p_ref, prm_ref, up_ref, tri_ref, bd_ref, sl_ref, il_ref, eye_ref,
                 o_ref, h_ref):
    t = pl.program_id(1)
    C = WKV_CHUNK
    n_chunks = p_ref.shape[0] // C
    W = RWKV_WIDTH

    @pl.when(t == 0)
    def _():
        h_ref[...] = jnp.zeros_like(h_ref)

    p = p_ref[...].astype(F32)
    r, k, v, lo = p[:, :W], p[:, W:2 * W], p[:, 2 * W:3 * W], p[:, 3 * W:]
    lane = lax.broadcasted_iota(jnp.int32, lo.shape, 1)
    act = jnp.where(lane < W_LORA, jnp.tanh(lo), jnp.where(lane < W_LORA + A_LORA, lo, _sigmoid(lo)))
    up = _mm(act, up_ref[...])
    prm = prm_ref[...]
    w0, a0, k_k, k_a, r_k, ln_g, ln_b = (prm[n:n + 1] for n in range(7))
    w_raw = -_softplus(-(w0 + up[:, :W])) - 0.5
    logd = -jnp.exp(w_raw)
    a_sig = _sigmoid(a0 + up[:, W:2 * W])
    gate = up[:, 2 * W:]

    bd = bd_ref[...]
    bd_lo = bd.astype(BF16)

    def head_sum(z):
        return jnp.concatenate([_mm(z[:, g * SLAB:(g + 1) * SLAB], bd_lo)
                                for g in range(W // SLAB)], axis=1)

    kkr = k * k_k
    kk = kkr / jnp.maximum(jnp.sqrt(head_sum(kkr * kkr)), 1e-12)
    kf = k * (1.0 + (a_sig - 1.0) * k_a)
    bonus = head_sum(r * kf * r_k) * v
    a_vec = -kk
    b_vec = kk * a_sig

    cum = _mm_split(tri_ref[...], logd)
    cum_end = jnp.concatenate(
        [jnp.broadcast_to(cum[(ch + 1) * C - 1:(ch + 1) * C, :], (C, W)) for ch in range(n_chunks)],
        axis=0)
    d_in = jnp.exp(cum)
    d_ex = jnp.exp(cum - logd)
    d_inv = jnp.exp(-cum)
    d_end = jnp.exp(cum_end)
    d_rest = jnp.exp(cum_end - cum)

    at, rt = a_vec * d_ex, r * d_in
    bt, kt = b_vec * d_inv, kf * d_inv
    be, ke = b_vec * d_rest, kf * d_rest

    sl = sl_ref[...]
    il = il_ref[...]
    eye_c = il - sl
    eye_s = eye_ref[...]
    heads = SLAB // RWKV_HEAD
    n_slabs = W // SLAB
    all_probs = [(ch, g) for ch in range(n_chunks) for g in range(n_slabs)]

    def cut(z, ch, g):
        return z[ch * C:(ch + 1) * C, g * SLAB:(g + 1) * SLAB]

    def stack(zz):
        return jnp.concatenate([zz.astype(BF16)] * heads, axis=0) * bd_lo

    def solve(probs):
        ats = [cut(at, *pr) for pr in probs]
        rts = [cut(rt, *pr) for pr in probs]
        vs = [cut(v, *pr) for pr in probs]
        btm = [stack(cut(bt, *pr)) for pr in probs]
        ktm = [stack(cut(kt, *pr)) for pr in probs]
        vm = [stack(v_) for v_ in vs]
        l_ab = [sl * _mm_nt(a_, b_) for a_, b_ in zip(ats, btm)]
        l_ak = [sl * _mm_nt(a_, k_) for a_, k_ in zip(ats, ktm)]
        lpm = [stack(l_) for l_ in l_ab]
        lp = [_mm(l_, m_) for l_, m_ in zip(l_ab, lpm)]
        inv = [eye_c + l_ for l_ in l_ab]
        wv = [_mm(l_, v_) for l_, v_ in zip(l_ak, vm)]
        m_rb = [il * _mm_nt(r_, b_) for r_, b_ in zip(rts, btm)]
        m_rk = [il * _mm_nt(r_, k_) for r_, k_ in zip(rts, ktm)]
        levels = int(math.log2(C)) - 1
        for lev in range(levels):
            lpm = [stack(l_) for l_ in lp]
            inv_new = [i_ + _mm(i_, m_) for i_, m_ in zip(inv, lpm)]
            if lev + 1 < levels:
                lp = [_mm(l_, m_) for l_, m_ in zip(lp, lpm)]
            inv = inv_new
        abar = [_mm(i_, stack(a_)) for i_, a_ in zip(inv, ats)]
        vbar = [_mm(i_, stack(w_)) for i_, w_ in zip(inv, wv)]
        abm = [stack(a_) for a_ in abar]
        vbm = [stack(v_) for v_ in vbar]
        rhat = [r_ + _mm(m_, a_) for r_, m_, a_ in zip(rts, m_rb, abm)]
        yhat = [_mm(mb_, vb_) + _mm(mk_, v_) for mb_, vb_, mk_, v_ in zip(m_rb, vbm, m_rk, vm)]
        bes = [cut(be, *pr) for pr in probs]
        kes = [cut(ke, *pr) for pr in probs]
        gmat = [eye_s * cut(d_end, *pr)[0:1] + bd * _mm_tn(b_, a_)
                for pr, b_, a_ in zip(probs, bes, abar)]
        fmat = [bd * _mm_tn(jnp.concatenate([b_, k_], axis=0), jnp.concatenate([vb_, v_], axis=0))
                for b_, k_, vb_, v_ in zip(bes, kes, vbar, vs)]
        return rhat, yhat, gmat, fmat

    rhat, yhat, gmat, fmat = [], [], [], []
    for first in range(0, len(all_probs), WKV_GROUP):
        for total, part in zip((rhat, yhat, gmat, fmat), solve(all_probs[first:first + WKV_GROUP])):
            total += part

    hs = [h_ref[g] for g in range(n_slabs)]
    rows = []
    for ch in range(n_chunks):
        ys = []
        for g in range(n_slabs):
            n = ch * n_slabs + g
            ys.append(_mm(rhat[n], hs[g]) + yhat[n])
            hs[g] = _mm(gmat[n], hs[g]) + fmat[n]
        rows.append(jnp.concatenate(ys, axis=1))
    for g in range(n_slabs):
        h_ref[g] = hs[g]
    y = jnp.concatenate(rows, axis=0) if n_chunks > 1 else rows[0]

    inv_n = 1.0 / RWKV_HEAD
    mu = head_sum(y) * inv_n
    yc = y - mu
    var = head_sum(yc * yc) * inv_n
    yn = yc * lax.rsqrt(var + GN_EPS) * ln_g + ln_b
    o_ref[...] = ((yn + bonus) * gate).astype(BF16)


def _block_masks():
    n = SLAB
    r = jnp.arange(n)[:, None]
    c = jnp.arange(n)[None, :]
    same = (r // RWKV_HEAD) == (c // RWKV_HEAD)
    bd = same.astype(F32)
    eye = (r == c).astype(F32)
    t = jnp.arange(WKV_CHUNK)[:, None]
    s_in_head = c % RWKV_HEAD
    sl = (s_in_head < t).astype(F32)
    il = (s_in_head <= t).astype(F32)
    tt = jnp.arange(WKV_STEP)
    tri = ((tt[None, :] <= tt[:, None])
           & (tt[None, :] // WKV_CHUNK == tt[:, None] // WKV_CHUNK)).astype(BF16)
    return tri, bd, sl, il, eye


def _rwkv_mix(p, prm, up_w, masks, bsz, seq):
    C = WKV_STEP
    nc = seq // C
    tok = p.shape[0]
    tri, bd, sl, il, eye = masks
    row = lambda b, t: (b * nc + t, 0)
    const = lambda b, t: (0, 0)
    return pl.pallas_call(
        _rwkv_kernel,
        grid=(bsz, nc),
        in_specs=[pl.BlockSpec((C, RWKV_COLS), row),
                  pl.BlockSpec((8, RWKV_WIDTH), const),
                  pl.BlockSpec((LORA_PAD, 3 * RWKV_WIDTH), const),
                  pl.BlockSpec((C, C), const),
                  pl.BlockSpec((SLAB, SLAB), const),
                  pl.BlockSpec((WKV_CHUNK, SLAB), const),
                  pl.BlockSpec((WKV_CHUNK, SLAB), const),
                  pl.BlockSpec((SLAB, SLAB), const)],
        out_specs=pl.BlockSpec((C, RWKV_WIDTH), row),
        out_shape=jax.ShapeDtypeStruct((tok, RWKV_WIDTH), BF16),
        scratch_shapes=[pltpu.VMEM((RWKV_WIDTH // SLAB, SLAB, SLAB), F32)],
        compiler_params=_cparams(("arbitrary", "arbitrary")),
        name="rwkv7_mix",
    )(p, prm, up_w, tri, bd, sl, il, eye)


def _out_kernel(a_ref, r_ref, x_ref, wa_ref, wr_ref, gt_ref, g2_ref, sc_ref, sh_ref, wrt_ref,
                brt_ref, x1_ref, h2_ref, rt_ref):
    mixed = (jnp.dot(a_ref[...], wa_ref[...], preferred_element_type=F32)
             + jnp.dot(r_ref[...], wr_ref[...], preferred_element_type=F32))
    x1 = x_ref[...] + gt_ref[0] * mixed
    x1_ref[...] = x1
    ms = jnp.mean(x1 * x1, axis=-1, keepdims=True)
    h2 = x1 * lax.rsqrt(ms + NORM_EPS) * g2_ref[...]
    h2 = h2 * (1.0 + sc_ref[0]) + sh_ref[0]
    h2_ref[...] = _pack_rows(h2)
    h_hi = h2.astype(BF16)
    h_lo = (h2 - h_hi.astype(F32)).astype(BF16)
    logits = (jnp.dot(h_hi, wrt_ref[0], preferred_element_type=F32)
              + jnp.dot(h_lo, wrt_ref[0], preferred_element_type=F32)
              + jnp.dot(h_hi, wrt_ref[1], preferred_element_type=F32)) + brt_ref[...]

    lane_i = lax.broadcasted_iota(jnp.int32, logits.shape, 1)
    lane = lane_i.astype(F32)
    far = float(ROUTE_LANES)
    is_g = lane_i < N_GROUPS
    gmax = jnp.max(jnp.where(is_g, logits, NEG_BIG), axis=-1, keepdims=True)
    gsum = jnp.sum(jnp.where(is_g, jnp.exp(logits - gmax), 0.0), axis=-1, keepdims=True)
    g_top = 1.0 / gsum
    g_idx = jnp.min(jnp.where(is_g & (logits == gmax), lane, far), axis=-1, keepdims=True)
    base = N_GROUPS + g_idx * EXPERTS_PER_GROUP
    in_g = (lane >= base) & (lane < base + EXPERTS_PER_GROUP)
    v1 = jnp.max(jnp.where(in_g, logits, NEG_BIG), axis=-1, keepdims=True)
    i1 = jnp.min(jnp.where(in_g & (logits == v1), lane, far), axis=-1, keepdims=True)
    rest = in_g & (lane != i1)
    v2 = jnp.max(jnp.where(rest, logits, NEG_BIG), axis=-1, keepdims=True)
    i2 = jnp.min(jnp.where(rest & (logits == v2), lane, far), axis=-1, keepdims=True)
    e21 = jnp.exp(v2 - v1)
    w1 = g_top / (1.0 + e21)
    w2 = g_top * e21 / (1.0 + e21)
    rt_ref[...] = jnp.where(lane_i == 0, i1 - N_GROUPS,
                            jnp.where(lane_i == 1, i2 - N_GROUPS,
                                      jnp.where(lane_i == 2, w1,
                                                jnp.where(lane_i == 3, w2, 0.0))))


def _out_projection(attn_o, rwkv_o, x2, w_a, w_r, mod3, g2, w_rt, b_rt, layer, bsz, seq):
    tm = min(512, seq)
    nt = seq // tm
    tok = x2.shape[0]
    row = lambda b, i: (b * nt + i, 0)
    const = lambda b, i: (0, 0)
    modspec = lambda col: pl.BlockSpec((1, 1, D_MODEL), lambda b, i: (layer * bsz + b, 0, col))
    return pl.pallas_call(
        _out_kernel,
        grid=(bsz, nt),
        in_specs=[pl.BlockSpec((tm, ATTN_WIDTH), row),
                  pl.BlockSpec((tm, RWKV_WIDTH), row),
                  pl.BlockSpec((tm, D_MODEL), row),
                  pl.BlockSpec((ATTN_WIDTH, D_MODEL), const),
                  pl.BlockSpec((RWKV_WIDTH, D_MODEL), const),
                  modspec(2),
                  pl.BlockSpec((1, D_MODEL), const),
                  modspec(4),
                  modspec(3),
                  pl.BlockSpec((2, D_MODEL, ROUTE_LANES), lambda b, i: (0, 0, 0)),
                  pl.BlockSpec((1, ROUTE_LANES), const)],
        out_specs=[pl.BlockSpec((tm, D_MODEL), row),
                   pl.BlockSpec((tm, PACKED), row),
                   pl.BlockSpec((tm, ROUTE_LANES), row)],
        out_shape=[jax.ShapeDtypeStruct((tok, D_MODEL), F32),
                   jax.ShapeDtypeStruct((tok, PACKED), jnp.uint32),
                   jax.ShapeDtypeStruct((tok, ROUTE_LANES), F32)],
        compiler_params=_cparams(("arbitrary", "arbitrary")),
        name="out_proj_router",
    )(attn_o, rwkv_o, x2, w_a, w_r, mod3, g2, mod3, mod3, w_rt, b_rt)


def _dispatch_kernel(dest_ref, h_ref, xin_ref, xs_ref, stage_ref, sems):
    del xin_ref
    tt = h_ref.shape[0]
    i = pl.program_id(0)
    slot = i % 2
    stage_ref[slot] = h_ref[...]

    def issue(g, c):
        base = pl.multiple_of(g * DMA_UNROLL, DMA_UNROLL)
        group = stage_ref.at[slot, pl.ds(base, DMA_UNROLL)]
        for u in range(DMA_UNROLL):
            for kk in range(TOP_K):
                d = dest_ref[0, 0, TOP_K * base + (TOP_K * u + kk)]
                pltpu.make_async_copy(group.at[pl.ds(u, 1)],
                                      xs_ref.at[pl.ds(d, 1)], sems.at[slot]).start(priority=kk)
        return c

    def drain(which):
        for kk in range(TOP_K):
            pltpu.make_async_copy(stage_ref.at[which], xs_ref.at[pl.ds(0, tt)],
                                  sems.at[which]).wait()

    lax.fori_loop(0, tt // DMA_UNROLL, issue, 0)

    @pl.when(i > 0)
    def _():
        drain(1 - slot)

    @pl.when(i + 1 == pl.num_programs(0))
    def _():
        drain(slot)


def _dispatch(dest3, h2, xs_init):
    tok = h2.shape[0]
    nt, _, two_tt = dest3.shape
    tt = two_tt // TOP_K
    return pl.pallas_call(
        _dispatch_kernel,
        grid=(nt,),
        in_specs=[pl.BlockSpec((1, 1, two_tt), lambda i: (i, 0, 0), memory_space=pltpu.SMEM),
                  pl.BlockSpec((tt, PACKED), lambda i: (i, 0)),
                  pl.BlockSpec(memory_space=pl.ANY)],
        out_specs=pl.BlockSpec(memory_space=pl.ANY),
        out_shape=jax.ShapeDtypeStruct(xs_init.shape, jnp.uint32),
        scratch_shapes=[pltpu.VMEM((2, tt, PACKED), jnp.uint32), pltpu.SemaphoreType.DMA((2,))],
        input_output_aliases={2: 0},
        compiler_params=_cparams(("arbitrary",)),
        name="moe_dispatch",
    )(dest3, h2, xs_init)


def _expert_kernel(be_ref, x_ref, wg_ref, wu_ref, wd_ref, y_ref, wg_lo, wu_lo, wd_lo):
    i = pl.program_id(0)

    @pl.when((i == 0) | (be_ref[i] != be_ref[jnp.maximum(i - 1, 0)]))
    def _():
        wg_lo[...] = wg_ref[0].astype(BF16)
        wu_lo[...] = wu_ref[0].astype(BF16)
        wd_lo[...] = wd_ref[0].astype(BF16)

    xb = _unpack_rows(x_ref[...]).astype(BF16)
    gate = jnp.dot(xb, wg_lo[...], preferred_element_type=F32)
    up = jnp.dot(xb, wu_lo[...], preferred_element_type=F32)
    hid = gate * _sigmoid(gate) * up
    y_ref[...] = _pack_rows(jnp.dot(hid.astype(BF16), wd_lo[...], preferred_element_type=F32))


def _experts(block_e, xs, w_gate, w_up, w_down, layer):
    rows = xs.shape[0]
    nb = rows // MOE_ROWS
    grid_spec = pltpu.PrefetchScalarGridSpec(
        num_scalar_prefetch=1,
        grid=(nb,),
        in_specs=[pl.BlockSpec((MOE_ROWS, PACKED), lambda i, be: (i, 0)),
                  pl.BlockSpec((1, D_MODEL, D_EXPERT), lambda i, be: (layer * N_EXPERTS + be[i], 0, 0)),
                  pl.BlockSpec((1, D_MODEL, D_EXPERT), lambda i, be: (layer * N_EXPERTS + be[i], 0, 0)),
                  pl.BlockSpec((1, D_EXPERT, D_MODEL), lambda i, be: (layer * N_EXPERTS + be[i], 0, 0))],
        out_specs=pl.BlockSpec((MOE_ROWS, PACKED), lambda i, be: (i, 0)),
        scratch_shapes=[pltpu.VMEM((D_MODEL, D_EXPERT), BF16),
                        pltpu.VMEM((D_MODEL, D_EXPERT), BF16),
                        pltpu.VMEM((D_EXPERT, D_MODEL), BF16)],
    )
    return pl.pallas_call(
        _expert_kernel,
        grid_spec=grid_spec,
        out_shape=jax.ShapeDtypeStruct((rows, PACKED), jnp.uint32),
        compiler_params=_cparams(("arbitrary",)),
        name="moe_experts",
    )(block_e, xs, w_gate, w_up, w_down)


def _combine_kernel(dest_ref, next_ref, yb_ref, rt_ref, x_ref, gt_ref, fg_ref, o_ref, buf_ref, sems,
                    *, final_norm):
    tt = x_ref.shape[0]
    i = pl.program_id(0)
    slot = i % 2

    def gather(table_ref, into):
        def issue(g, c):
            base = pl.multiple_of(g * DMA_UNROLL, DMA_UNROLL)
            groups = [buf_ref.at[into, kk, pl.ds(base, DMA_UNROLL)] for kk in range(TOP_K)]
            for u in range(DMA_UNROLL):
                for kk in range(TOP_K):
                    d = table_ref[0, 0, TOP_K * base + (TOP_K * u + kk)]
                    pltpu.make_async_copy(yb_ref.at[pl.ds(d, 1)], groups[kk].at[pl.ds(u, 1)],
                                          sems.at[into]).start(priority=kk)
            return c

        lax.fori_loop(0, tt // DMA_UNROLL, issue, 0)

    @pl.when(i == 0)
    def _():
        gather(dest_ref, 0)

    @pl.when(i + 1 < pl.num_programs(0))
    def _():
        gather(next_ref, 1 - slot)

    for kk in range(TOP_K):
        pltpu.make_async_copy(yb_ref.at[pl.ds(0, tt)], buf_ref.at[slot, kk], sems.at[slot]).wait()
    rt = rt_ref[...]
    y = (_unpack_rows(buf_ref[slot, 0]) * rt[:, 2:3]
         + _unpack_rows(buf_ref[slot, 1]) * rt[:, 3:4])
    x = x_ref[...] + gt_ref[0] * y
    if final_norm:
        ms = jnp.mean(x * x, axis=-1, keepdims=True)
        x = x * lax.rsqrt(ms + NORM_EPS) * fg_ref[...]
    o_ref[...] = x


def _combine(dest3, yb, rt, x1, mod3, final_g, layer, bsz, seq, final_norm):
    tok = x1.shape[0]
    nt, _, two_tt = dest3.shape
    tt = two_tt // TOP_K
    per_seq = seq // tt
    return pl.pallas_call(
        functools.partial(_combine_kernel, final_norm=final_norm),
        grid=(nt,),
        in_specs=[pl.BlockSpec((1, 1, two_tt), lambda i: (i, 0, 0), memory_space=pltpu.SMEM),
                  pl.BlockSpec((1, 1, two_tt), lambda i: (jnp.minimum(i + 1, nt - 1), 0, 0),
                               memory_space=pltpu.SMEM),
                  pl.BlockSpec(memory_space=pl.ANY),
                  pl.BlockSpec((tt, ROUTE_LANES), lambda i: (i, 0)),
                  pl.BlockSpec((tt, D_MODEL), lambda i: (i, 0)),
                  pl.BlockSpec((1, 1, D_MODEL), lambda i: (layer * bsz + i // per_seq, 0, 5)),
                  pl.BlockSpec((1, D_MODEL), lambda i: (0, 0))],
        out_specs=pl.BlockSpec((tt, D_MODEL), lambda i: (i, 0)),
        out_shape=jax.ShapeDtypeStruct((tok, D_MODEL), F32),
        scratch_shapes=[pltpu.VMEM((2, TOP_K, tt, PACKED), jnp.uint32),
                        pltpu.SemaphoreType.DMA((2,))],
        compiler_params=_cparams(("arbitrary",)),
        name="moe_combine",
    )(dest3, dest3, yb, rt, x1, mod3, final_g)


def _routing_tables(rt, tok):
    m = tok * TOP_K
    flat_e = rt[:, :TOP_K].astype(jnp.int32).reshape(m)
    onehot = (flat_e[:, None] == jnp.arange(N_EXPERTS, dtype=jnp.int32)[None, :]).astype(jnp.int32)
    ct = min(COUNT_TILE, m)
    tri = (jnp.arange(ct)[None, :] <= jnp.arange(ct)[:, None]).astype(BF16)
    within = jnp.einsum("ts,nse->nte", tri, onehot.reshape(m // ct, ct, N_EXPERTS).astype(BF16),
                        preferred_element_type=F32)
    tile_tot = within[:, -1, :]
    tile_base = jnp.cumsum(tile_tot, axis=0) - tile_tot
    running = (within + tile_base[:, None, :]).reshape(m, N_EXPERTS).astype(jnp.int32)
    counts = running[-1]
    padded = (counts + MOE_ROWS - 1) // MOE_ROWS * MOE_ROWS
    pad_end = jnp.cumsum(padded)
    pad_start = pad_end - padded
    dest = jnp.sum(onehot * (running - 1 + pad_start[None, :]), axis=1)
    rows = -(-m // MOE_ROWS) * MOE_ROWS + N_EXPERTS * MOE_ROWS
    nb = rows // MOE_ROWS
    block_start = jnp.arange(nb, dtype=jnp.int32) * MOE_ROWS
    block_e = jnp.minimum(jnp.sum((pad_end[None, :] <= block_start[:, None]).astype(jnp.int32), axis=1),
                          N_EXPERTS - 1).astype(jnp.int32)
    return dest.astype(jnp.int32), block_e, rows


def _rotary_tables(positions):
    half = ROT_DIM // 2
    inv_freq = ROPE_THETA ** (-jnp.arange(0, ROT_DIM, 2, dtype=F32) / ROT_DIM)
    ang = positions.astype(F32).reshape(-1, 1) * inv_freq[None, :]
    cos, sin = jnp.cos(ang), jnp.sin(ang)
    dim = jnp.arange(128)[None, :] % HEAD_DIM
    freq = jnp.arange(half)[:, None]
    lo = (dim == freq).astype(F32)
    hi = (dim == freq + half).astype(F32)
    rest = (dim >= ROT_DIM).astype(F32)
    place = functools.partial(jnp.dot, precision=lax.Precision.HIGHEST)
    return place(cos, lo + hi) + rest, place(sin, hi), place(sin, -lo)


def _lambda_init(layer):
    return 0.8 - 0.6 * math.exp(-0.3 * layer)


def kernel(x, c, positions, ada_w, ada_b, norm1_g, norm2_g, w_in, w_out, attn_lambda, attn_subln_g, rwkv_shift_mu, rwkv_w0, rwkv_w_up, rwkv_a0, rwkv_a_up, rwkv_g_up, rwkv_k_k, rwkv_k_a, rwkv_r_k, rwkv_lnx_g, rwkv_lnx_b, moe_w_group, moe_b_group, moe_w_router, moe_b_router, moe_w_gate, moe_w_up, moe_w_down, final_g):
    bsz, seq, d = x.shape
    depth = ada_w.shape[0]
    tok = bsz * seq
    x2 = x.reshape(tok, d)

    mod = _modulation(c, ada_w, ada_b)
    mod3 = mod.reshape(depth * bsz, 1, 6 * d)
    cos_t, s1_t, s2_t = _rotary_tables(positions)
    masks = _block_masks()
    tt = min(512, seq)

    for layer in range(depth):
        pad = LORA_PAD - LORA
        w_pad = jnp.pad(w_in[layer], ((0, 0), (0, pad))).astype(BF16)
        mu_pad = jnp.pad(rwkv_shift_mu[layer], (0, pad)).reshape(1, RWKV_COLS)
        qa, qb, k, v, p = _in_projection(x2, norm1_g[layer].reshape(1, d), mod3, layer, bsz, seq,
                                         w_pad, mu_pad, cos_t, s1_t, s2_t)
        attn_o = _diff_attention(qa, qb, k, v, attn_lambda[layer],
                                 attn_subln_g[layer].reshape(1, 2 * HEAD_DIM),
                                 _lambda_init(layer), bsz, seq)

        zrow = jnp.zeros((RWKV_WIDTH,), F32)
        prm = jnp.stack([rwkv_w0[layer], rwkv_a0[layer], rwkv_k_k[layer], rwkv_k_a[layer],
                         rwkv_r_k[layer].reshape(RWKV_WIDTH), rwkv_lnx_g[layer],
                         rwkv_lnx_b[layer], zrow])
        up_w = jnp.zeros((LORA_PAD, 3 * RWKV_WIDTH), F32)
        up_w = up_w.at[:W_LORA, :RWKV_WIDTH].set(rwkv_w_up[layer])
        up_w = up_w.at[W_LORA:W_LORA + A_LORA, RWKV_WIDTH:2 * RWKV_WIDTH].set(rwkv_a_up[layer])
        up_w = up_w.at[W_LORA + A_LORA:LORA, 2 * RWKV_WIDTH:].set(rwkv_g_up[layer])
        rwkv_o = _rwkv_mix(p, prm, up_w.astype(BF16), masks, bsz, seq)

        w_o = w_out[layer].astype(BF16)
        w_rt = jnp.zeros((d, ROUTE_LANES), F32)
        w_rt = w_rt.at[:, :N_GROUPS].set(moe_w_group[layer])
        w_rt = w_rt.at[:, N_GROUPS:N_GROUPS + N_EXPERTS].set(moe_w_router[layer])
        w_rt_hi = w_rt.astype(BF16)
        w_rt = jnp.stack([w_rt_hi, (w_rt - w_rt_hi.astype(F32)).astype(BF16)])
        b_rt = jnp.zeros((1, ROUTE_LANES), F32)
        b_rt = b_rt.at[0, :N_GROUPS].set(moe_b_group[layer])
        b_rt = b_rt.at[0, N_GROUPS:N_GROUPS + N_EXPERTS].set(moe_b_router[layer])
        x1, h2, rt = _out_projection(attn_o, rwkv_o, x2, w_o[:ATTN_WIDTH], w_o[ATTN_WIDTH:], mod3,
                                     norm2_g[layer].reshape(1, d), w_rt, b_rt, layer, bsz, seq)

        dest, block_e, rows = _routing_tables(rt, tok)
        dest3 = dest.reshape(tok // tt, 1, TOP_K * tt)
        xs = _dispatch(dest3, h2, jnp.zeros((rows, PACKED), jnp.uint32))
        yb = _experts(block_e, xs, moe_w_gate.reshape(-1, d, D_EXPERT),
                      moe_w_up.reshape(-1, d, D_EXPERT), moe_w_down.reshape(-1, D_EXPERT, d), layer)
        x2 = _combine(dest3, yb, rt, x1, mod3, final_g.reshape(1, d), layer, bsz, seq,
                      final_norm=(layer == depth - 1))

    return x2.reshape(bsz, seq, d)
```
